```python
import math
import jax
import jax.numpy as jnp
from jax import lax
import numpy as np

D_MODEL = 1024
BATCH = 8
SEQ = 2048
DEPTH = 2
DEC_BATCH = 128
DEC_SEQ = 4
PAST_LEN = 16384
PAGE_SIZE = 128

N_MIXERS = 2
N_MLSTM_LAYERS = (DEPTH + 1) // 2
N_CMLP_LAYERS = DEPTH // 2

MLSTM_PROJ = 2
MLSTM_INNER = MLSTM_PROJ * D_MODEL
MLSTM_HEADS = 4
MLSTM_HEAD_DIM = MLSTM_INNER // MLSTM_HEADS
QKV_BLOCK = 4
N_QKV_BLOCKS = MLSTM_INNER // QKV_BLOCK
CONV_WIDTH = 4
MLSTM_CHUNK = 64
STAB_INIT = -1e30

CMLP_CHUNK = 128
CMLP_HIDDEN = 4 * D_MODEL
CMLP_HALF = CMLP_HIDDEN // 2
CMLP_GROUPS = 8
CMLP_GROUP_DIM = CMLP_HALF // CMLP_GROUPS

MOE_GROUPS = 4
MOE_EXPERTS_PER_GROUP = 8
MOE_TOP_GROUPS = 1
MOE_TOP_K = 2
MOE_HIDDEN = 512

kernel_name = 'mlstm_gmlp_hier_moe_decoder_step'


def rms_norm(x, g, eps=1e-6):
    xf = x.astype(jnp.float32)
    y = xf * lax.rsqrt(jnp.mean(xf * xf, axis=-1, keepdims=True) + eps)
    return (y * g.astype(jnp.float32)).astype(x.dtype)


def layer_norm(x, w, b, eps=1e-5):
    xf = x.astype(jnp.float32)
    mu = jnp.mean(xf, axis=-1, keepdims=True)
    var = jnp.mean(jnp.square(xf - mu), axis=-1, keepdims=True)
    y = (xf - mu) * lax.rsqrt(var + eps) * w.astype(jnp.float32) + b.astype(jnp.float32)
    return y.astype(x.dtype)


def head_layer_norm(h, w, eps=1e-5):
    mu = jnp.mean(h, axis=-1, keepdims=True)
    var = jnp.mean(jnp.square(h - mu), axis=-1, keepdims=True)
    return (h - mu) * lax.rsqrt(var + eps) * w.astype(jnp.float32)


def causal_depthwise_conv(x_ext, w, b, out_len):
    y = x_ext[:, 0:out_len] * w[0]
    for j in range(1, CONV_WIDTH):
        y = y + x_ext[:, j:j + out_len] * w[j]
    return y + b


def headwise_proj(t, w):
    bsz, s, _ = t.shape
    y = jnp.einsum('bsnc,ncd->bsnd', t.reshape(bsz, s, N_QKV_BLOCKS, QKV_BLOCK), w)
    return y.reshape(bsz, s, MLSTM_INNER)


def mlstm_chunk_step(carry, inp):
    c_prev, n_prev, m_prev = carry
    q, k, v, ig, lf = inp
    length = q.shape[2]
    bcum = jnp.cumsum(lf, axis=-1)
    causal = jnp.tril(jnp.ones((length, length), dtype=bool))
    log_d = jnp.where(causal, bcum[..., :, None] - bcum[..., None, :] + ig[..., None, :], -jnp.inf)
    m_inter = bcum + m_prev[..., None]
    m_t = jnp.maximum(m_inter, jnp.max(log_d, axis=-1))
    s = jnp.einsum('bhld,bhsd->bhls', q, k) * jnp.exp(log_d - m_t[..., None])
    inter = jnp.exp(m_inter - m_t)
    num = jnp.einsum('bhls,bhsv->bhlv', s, v) + inter[..., None] * jnp.einsum('bhld,bhdv->bhlv', q, c_prev)
    den = jnp.sum(s, axis=-1) + inter * jnp.einsum('bhld,bhd->bhl', q, n_prev)
    h = num / jnp.maximum(jnp.abs(den), jnp.exp(-m_t))[..., None]
    m_new = m_t[..., -1]
    w = jnp.exp(bcum[..., -1:] - bcum + ig - m_new[..., None])
    decay = jnp.exp(bcum[..., -1] + m_prev - m_new)
    wk = w[..., None] * k
    c_new = decay[..., None, None] * c_prev + jnp.einsum('bhld,bhlv->bhdv', wk, v)
    n_new = decay[..., None] * n_prev + jnp.sum(wk, axis=2)
    return (c_new, n_new, m_new), h


def mlstm_cell(q, k, v, ig, lf, c0, n0, m0):
    bsz, s = q.shape[0], q.shape[1]
    length = math.gcd(s, MLSTM_CHUNK)
    nc = s // length

    def to_chunks(t):
        t = t.reshape((bsz, nc, length) + t.shape[2:])
        return jnp.swapaxes(jnp.moveaxis(t, 1, 0), 2, 3)

    carry0 = (c0.astype(jnp.float32), n0.astype(jnp.float32), m0.astype(jnp.float32))
    xs = (to_chunks(q), to_chunks(k), to_chunks(v), to_chunks(ig), to_chunks(lf))
    (c1, n1, m1), hs = lax.scan(mlstm_chunk_step, carry0, xs)
    h = jnp.transpose(hs, (1, 0, 3, 2, 4)).reshape(bsz, s, MLSTM_HEADS, MLSTM_HEAD_DIM)
    return h, c1, n1, m1


def mlstm_mixer(x, conv_buf, c0, n0, m0, p, j):
    bsz, s, _ = x.shape
    xm, z = jnp.split(x @ p['mlstm_w_up'][j], 2, axis=-1)
    x_ext = jnp.concatenate([conv_buf.astype(xm.dtype), xm], axis=1)
    xc = jax.nn.silu(causal_depthwise_conv(x_ext, p['mlstm_conv_w'][j], p['mlstm_conv_b'][j], s))
    q = headwise_proj(xc, p['mlstm_w_q'][j])
    k = headwise_proj(xc, p['mlstm_w_k'][j])
    v = headwise_proj(xm, p['mlstm_w_v'][j])
    qkv = jnp.concatenate([q, k, v], axis=-1)
    ig = (qkv @ p['mlstm_w_ig'][j] + p['mlstm_b_ig'][j]).astype(jnp.float32)
    lf = jax.nn.log_sigmoid((qkv @ p['mlstm_w_fg'][j] + p['mlstm_b_fg'][j]).astype(jnp.float32))

    def heads(t):
        return t.astype(jnp.float32).reshape(bsz, s, MLSTM_HEADS, MLSTM_HEAD_DIM)

    h, c1, n1, m1 = mlstm_cell(heads(q), heads(k) * (MLSTM_HEAD_DIM ** -0.5), heads(v), ig, lf, c0, n0, m0)
    h = head_layer_norm(h, p['mlstm_hn_w'][j]).reshape(bsz, s, MLSTM_INNER).astype(x.dtype)
    h = (h + p['mlstm_skip'][j] * xc) * jax.nn.silu(z)
    new_conv = x_ext[:, x_ext.shape[1] - (CONV_WIDTH - 1):]
    return h @ p['mlstm_w_down'][j], new_conv, c1, n1, m1


def chunk_mlp_mixer(x, p, j):
    bsz, s, _ = x.shape
    zz = jax.nn.gelu(x @ p['cmlp_w_in'][j] + p['cmlp_b_in'][j])
    u, v = jnp.split(zz, 2, axis=-1)
    v = layer_norm(v, p['cmlp_ln_w'][j], p['cmlp_ln_b'][j])
    pad = (-s) % CMLP_CHUNK
    nc = (s + pad) // CMLP_CHUNK
    vp = jnp.pad(v, ((0, 0), (0, pad), (0, 0))).reshape(bsz, nc, CMLP_CHUNK, CMLP_GROUPS, CMLP_GROUP_DIM)
    causal = jnp.tril(jnp.ones((CMLP_CHUNK, CMLP_CHUNK), dtype=bool))
    ws = jnp.where(causal, p['cmlp_w_s'][j], jnp.zeros_like(p['cmlp_w_s'][j]))
    bias = jnp.swapaxes(p['cmlp_b_s'][j], 0, 1)[None, None, :, :, None]
    mixed = jnp.einsum('gts,bcsgd->bctgd', ws, vp) + bias
    mixed = mixed.reshape(bsz, nc * CMLP_CHUNK, CMLP_HALF)[:, :s]
    return (u * mixed) @ p['cmlp_w_out'][j] + p['cmlp_b_out'][j], v


def hier_moe(x, p, i):
    g_prob = jax.nn.softmax((x @ p['moe_w_rg'][i] + p['moe_b_rg'][i]).astype(jnp.float32), axis=-1)
    g_w, g_idx = lax.top_k(g_prob, MOE_TOP_GROUPS)
    g_onehot = jax.nn.one_hot(g_idx[:, 0], MOE_GROUPS, dtype=jnp.float32)
    e_logits = (jnp.einsum('td,gde->tge', x, p['moe_w_re'][i]) + p['moe_b_re'][i]).astype(jnp.float32)
    e_sel = jnp.einsum('tg,tge->te', g_onehot, e_logits)
    e_top, e_idx = lax.top_k(e_sel, MOE_TOP_K)
    e_w = jax.nn.softmax(e_top, axis=-1) * g_w
    within = jnp.sum(jax.nn.one_hot(e_idx, MOE_EXPERTS_PER_GROUP, dtype=jnp.float32) * e_w[..., None], axis=1)
    combine = (g_onehot[:, :, None] * within[:, None, :]).astype(x.dtype)
    y = jnp.zeros_like(x)
    for g in range(MOE_GROUPS):
        a = jnp.einsum('td,edf->tef', x, p['moe_w_gate'][i, g])
        b = jnp.einsum('td,edf->tef', x, p['moe_w_up'][i, g])
        hg = jax.nn.silu(a) * b * combine[:, g, :, None]
        y = y + jnp.einsum('tef,efd->td', hg, p['moe_w_down'][i, g])
    return y


def run_trunk(x, conv_buf, c0, n0, m0, p):
    new_c, new_n, new_m, new_conv, new_v = [], [], [], [], []
    ja = 0
    jb = 0
    for i in range(DEPTH):
        h = rms_norm(x, p['norm_mix'][i])
        if i % N_MIXERS == 0:
            out, cb, c1, n1, m1 = mlstm_mixer(h, conv_buf[ja], c0[ja], n0[ja], m0[ja], p, ja)
            new_c.append(c1)
            new_n.append(n1)
            new_m.append(m1)
            new_conv.append(cb)
            ja += 1
        else:
            out, v = chunk_mlp_mixer(h, p, jb)
            new_v.append(v)
            jb += 1
        x = x + out
        h = rms_norm(x, p['norm_ffn'][i])
        x = x + hier_moe(h.reshape(-1, D_MODEL), p, i).reshape(x.shape)
    y = rms_norm(x, p['norm_final'])
    return y, jnp.stack(new_c), jnp.stack(new_n), jnp.stack(new_m), jnp.stack(new_conv), jnp.stack(new_v)


def setup_inputs(seed: int = 0) -> dict:
    key = jax.random.key(seed)
    ks = iter(jax.random.split(key, 48))

    def nrm(shape, scale):
        return scale * jax.random.normal(next(ks), shape, jnp.float32)

    NA, NC = N_MLSTM_LAYERS, N_CMLP_LAYERS
    H, DH, E = MLSTM_HEADS, MLSTM_HEAD_DIM, MLSTM_INNER
    NG, EPG, F = MOE_GROUPS, MOE_EXPERTS_PER_GROUP, MOE_HIDDEN
    inp = {}
    inp['x_prompt'] = nrm((BATCH, SEQ, D_MODEL), 1.0)
    inp['x_sample'] = nrm((DEC_BATCH, DEC_SEQ, D_MODEL), 1.0)
    inp['state_mlstm_C'] = nrm((NA, DEC_BATCH, H, DH, DH), 0.1)
    inp['state_mlstm_n'] = nrm((NA, DEC_BATCH, H, DH), 0.1)
    inp['state_mlstm_m'] = nrm((NA, DEC_BATCH, H), 0.5)
    inp['state_mlstm_conv'] = nrm((NA, DEC_BATCH, CONV_WIDTH - 1, E), 1.0)
    inp['norm_mix'] = 1.0 + nrm((DEPTH, D_MODEL), 0.01)
    inp['norm_ffn'] = 1.0 + nrm((DEPTH, D_MODEL), 0.01)
    inp['norm_final'] = 1.0 + nrm((D_MODEL,), 0.01)
    inp['mlstm_w_up'] = nrm((NA, D_MODEL, 2 * E), D_MODEL ** -0.5)
    inp['mlstm_conv_w'] = nrm((NA, CONV_WIDTH, E), CONV_WIDTH ** -0.5)
    inp['mlstm_conv_b'] = nrm((NA, E), 0.01)
    inp['mlstm_w_q'] = nrm((NA, N_QKV_BLOCKS, QKV_BLOCK, QKV_BLOCK), QKV_BLOCK ** -0.5)
    inp['mlstm_w_k'] = nrm((NA, N_QKV_BLOCKS, QKV_BLOCK, QKV_BLOCK), QKV_BLOCK ** -0.5)
    inp['mlstm_w_v'] = nrm((NA, N_QKV_BLOCKS, QKV_BLOCK, QKV_BLOCK), QKV_BLOCK ** -0.5)
    inp['mlstm_w_ig'] = nrm((NA, 3 * E, H), (3 * E) ** -0.5)
    inp['mlstm_b_ig'] = nrm((NA, H), 0.1)
    inp['mlstm_w_fg'] = nrm((NA, 3 * E, H), (3 * E) ** -0.5)
    inp['mlstm_b_fg'] = jnp.linspace(3.0, 6.0, H, dtype=jnp.float32)[None, :] + nrm((NA, H), 0.01)
    inp['mlstm_skip'] = 1.0 + nrm((NA, E), 0.01)
    inp['mlstm_hn_w'] = 1.0 + nrm((NA, H, DH), 0.01)
    inp['mlstm_w_down'] = nrm((NA, E, D_MODEL), E ** -0.5)
    inp['cmlp_w_in'] = nrm((NC, D_MODEL, CMLP_HIDDEN), D_MODEL ** -0.5)
    inp['cmlp_b_in'] = nrm((NC, CMLP_HIDDEN), 0.01)
    inp['cmlp_ln_w'] = 1.0 + nrm((NC, CMLP_HALF), 0.01)
    inp['cmlp_ln_b'] = nrm((NC, CMLP_HALF), 0.01)
    inp['cmlp_w_s'] = nrm((NC, CMLP_GROUPS, CMLP_CHUNK, CMLP_CHUNK), CMLP_CHUNK ** -0.5)
    inp['cmlp_b_s'] = 1.0 + nrm((NC, CMLP_GROUPS, CMLP_CHUNK), 0.1)
    inp['cmlp_w_out'] = nrm((NC, CMLP_HALF, D_MODEL), CMLP_HALF ** -0.5)
    inp['cmlp_b_out'] = nrm((NC, D_MODEL), 0.01)
    inp['moe_w_rg'] = nrm((DEPTH, D_MODEL, NG), D_MODEL ** -0.5)
    inp['moe_b_rg'] = nrm((DEPTH, NG), 0.01)
    inp['moe_w_re'] = nrm((DEPTH, NG, D_MODEL, EPG), D_MODEL ** -0.5)
    inp['moe_b_re'] = nrm((DEPTH, NG, EPG), 0.01)
    inp['moe_w_gate'] = nrm((DEPTH, NG, EPG, D_MODEL, F), D_MODEL ** -0.5)
    inp['moe_w_up'] = nrm((DEPTH, NG, EPG, D_MODEL, F), D_MODEL ** -0.5)
    inp['moe_w_down'] = nrm((DEPTH, NG, EPG, F, D_MODEL), F ** -0.5)
    return inp


def reference(x_prompt, x_sample, state_mlstm_C, state_mlstm_n, state_mlstm_m, state_mlstm_conv,
              norm_mix, norm_ffn, norm_final,
              mlstm_w_up, mlstm_conv_w, mlstm_conv_b, mlstm_w_q, mlstm_w_k, mlstm_w_v,
              mlstm_w_ig, mlstm_b_ig, mlstm_w_fg, mlstm_b_fg, mlstm_skip, mlstm_hn_w, mlstm_w_down,
              cmlp_w_in, cmlp_b_in, cmlp_ln_w, cmlp_ln_b, cmlp_w_s, cmlp_b_s, cmlp_w_out, cmlp_b_out,
              moe_w_rg, moe_b_rg, moe_w_re, moe_b_re, moe_w_gate, moe_w_up, moe_w_down):
    p = {
        'norm_mix': norm_mix, 'norm_ffn': norm_ffn, 'norm_final': norm_final,
        'mlstm_w_up': mlstm_w_up, 'mlstm_conv_w': mlstm_conv_w, 'mlstm_conv_b': mlstm_conv_b,
        'mlstm_w_q': mlstm_w_q, 'mlstm_w_k': mlstm_w_k, 'mlstm_w_v': mlstm_w_v,
        'mlstm_w_ig': mlstm_w_ig, 'mlstm_b_ig': mlstm_b_ig, 'mlstm_w_fg': mlstm_w_fg, 'mlstm_b_fg': mlstm_b_fg,
        'mlstm_skip': mlstm_skip, 'mlstm_hn_w': mlstm_hn_w, 'mlstm_w_down': mlstm_w_down,
        'cmlp_w_in': cmlp_w_in, 'cmlp_b_in': cmlp_b_in, 'cmlp_ln_w': cmlp_ln_w, 'cmlp_ln_b': cmlp_ln_b,
        'cmlp_w_s': cmlp_w_s, 'cmlp_b_s': cmlp_b_s, 'cmlp_w_out': cmlp_w_out, 'cmlp_b_out': cmlp_b_out,
        'moe_w_rg': moe_w_rg, 'moe_b_rg': moe_b_rg, 'moe_w_re': moe_w_re, 'moe_b_re': moe_b_re,
        'moe_w_gate': moe_w_gate, 'moe_w_up': moe_w_up, 'moe_w_down': moe_w_down,
    }
    bp = x_prompt.shape[0]
    conv0 = jnp.zeros((N_MLSTM_LAYERS, bp, CONV_WIDTH - 1, MLSTM_INNER), x_prompt.dtype)
    c0 = jnp.zeros((N_MLSTM_LAYERS, bp, MLSTM_HEADS, MLSTM_HEAD_DIM, MLSTM_HEAD_DIM), jnp.float32)
    n0 = jnp.zeros((N_MLSTM_LAYERS, bp, MLSTM_HEADS, MLSTM_HEAD_DIM), jnp.float32)
    m0 = jnp.full((N_MLSTM_LAYERS, bp, MLSTM_HEADS), STAB_INIT, jnp.float32)
    y_prompt, c_p, n_p, m_p, conv_p, _v_p = run_trunk(x_prompt, conv0, c0, n0, m0, p)
    y_sample, c_s, n_s, m_s, conv_s, v_s = run_trunk(x_sample, state_mlstm_conv, state_mlstm_C,
                                                     state_mlstm_n, state_mlstm_m, p)
    return (y_prompt, y_sample, c_p, n_p, m_p, conv_p, c_s, n_s, m_s, conv_s, v_s)
```

```python
import functools
import math

import jax
import jax.numpy as jnp
from jax import lax
from jax.experimental import pallas as pl
from jax.experimental.pallas import tpu as pltpu

F32 = jnp.float32
BF16 = jnp.bfloat16
I32 = jnp.int32

LANES = 128
SUBLANES = 8
ROW_TILE = 256
MXU_WIDTH = 256
VMEM_LIMIT = 56 * 1024 * 1024
CONV_WIDTH = 4
QKV_BLOCK = 4
MLSTM_CHUNK_PROMPT = 256
MLSTM_CHUNK_SAMPLE = 128
CMLP_CHUNK = 128
STAB_INIT = -1e30
N_GATES = 8


def _cparams(*sem):
    return pltpu.CompilerParams(dimension_semantics=sem, vmem_limit_bytes=VMEM_LIMIT)


def _rms(x, g, eps=1e-6):
    return x * lax.rsqrt(jnp.mean(x * x, axis=-1, keepdims=True) + eps) * g


def _silu(x):
    return x * (1.0 / (1.0 + jnp.exp(-x)))


def _split3(x):
    p0 = x.astype(BF16)
    r = x - p0.astype(F32)
    p1 = r.astype(BF16)
    p2 = (r - p1.astype(F32)).astype(BF16)
    return p0, p1, p2


def _dot(a, b):
    return jnp.dot(a, b, preferred_element_type=F32)


def _dot_nt(a, b):
    return lax.dot_general(a, b, (((1,), (1,)), ((), ())), preferred_element_type=F32)


def _dot_tn(a, b):
    return lax.dot_general(a, b, (((0,), (0,)), ((), ())), preferred_element_type=F32)


def _dot_nt_f32(a, b):
    a0, a1, a2 = _split3(a)
    b0, b1, b2 = _split3(b)
    acc = _dot_nt(a0, b0)
    acc += _dot_nt(a0, b1) + _dot_nt(a1, b0)
    acc += _dot_nt(a1, b1) + _dot_nt(a0, b2) + _dot_nt(a2, b0)
    return acc


def _norm_up_body(xp_ref, xs_ref, g_ref, w_ref, xm_ref, z_ref, *, n_prompt_tiles, inner):
    i = pl.program_id(0)
    x = jnp.where(i < n_prompt_tiles, xp_ref[...], xs_ref[...])
    h = _rms(x, g_ref[...]).astype(BF16)
    u = _dot(h, w_ref[...])
    xm_ref[...] = u[:, :inner]
    z_ref[...] = u[:, inner:].astype(BF16)


def _two_src_specs(n_prompt_tiles, width):
    sp = pl.BlockSpec((ROW_TILE, width), lambda i: (jnp.minimum(i, n_prompt_tiles - 1), 0))
    ss = pl.BlockSpec((ROW_TILE, width), lambda i: (jnp.maximum(i - n_prompt_tiles, 0), 0))
    return sp, ss


def _norm_up(xp, xs, g, w):
    tp, d = xp.shape
    ts = xs.shape[0]
    tt = tp + ts
    n2 = w.shape[1]
    inner = n2 // 2
    npt = tp // ROW_TILE
    sp, ss = _two_src_specs(npt, d)
    return pl.pallas_call(
        functools.partial(_norm_up_body, n_prompt_tiles=npt, inner=inner),
        grid=(tt // ROW_TILE,),
        in_specs=[sp, ss,
                  pl.BlockSpec((1, d), lambda i: (0, 0)),
                  pl.BlockSpec((d, n2), lambda i: (0, 0))],
        out_specs=[pl.BlockSpec((ROW_TILE, inner), lambda i: (i, 0)),
                   pl.BlockSpec((ROW_TILE, inner), lambda i: (i, 0))],
        out_shape=[jax.ShapeDtypeStruct((tt, inner), F32),
                   jax.ShapeDtypeStruct((tt, inner), BF16)],
        compiler_params=_cparams("parallel"),
        name="norm_up",
    )(xp, xs, g, w)


def _conv_qkv_body(xm_ref, st_ref, cw_ref, cb_ref, wqk_ref, wv_ref, wg_ref, bg_ref,
                   q_ref, k_ref, v_ref, xc_ref, gc_ref, st_out_ref,
                   *, row_stride, k_scale):
    t = pl.program_id(1)
    halo = st_ref.shape[1]
    rows, inner = xm_ref.shape

    @pl.when(t == 0)
    def _():
        st_out_ref[0] = st_ref[0]

    x = xm_ref[...]
    xe = jnp.concatenate([st_out_ref[0], x], axis=0)
    acc = x * cw_ref[CONV_WIDTH - 1:CONV_WIDTH, :] + cb_ref[...]
    for j in range(CONV_WIDTH - 1):
        lo = halo - (CONV_WIDTH - 1 - j) * row_stride
        acc = acc + xe[lo:lo + rows, :] * cw_ref[j:j + 1, :]
    st_out_ref[0] = xe[rows:rows + halo, :]
    xc = _silu(acc)
    xc_b = xc.astype(BF16)
    xm_b = x.astype(BF16)
    xc_ref[...] = xc_b

    gacc = jnp.zeros((rows, LANES), F32)
    for c in range(inner // MXU_WIDTH):
        sl = slice(c * MXU_WIDTH, (c + 1) * MXU_WIDTH)
        qk = _dot(xc_b[:, sl], wqk_ref[c])
        vv = _dot(xm_b[:, sl], wv_ref[c])
        qb = qk[:, :MXU_WIDTH].astype(BF16)
        kb = qk[:, MXU_WIDTH:].astype(BF16)
        vb = vv.astype(BF16)
        q_ref[:, sl] = qb
        k_ref[:, sl] = (qk[:, MXU_WIDTH:] * k_scale).astype(BF16)
        v_ref[:, sl] = vb
        gacc += _dot(qb, wg_ref[0, sl, :]) + _dot(kb, wg_ref[1, sl, :]) + _dot(vb, wg_ref[2, sl, :])
    g = gacc + bg_ref[...]
    lane = lax.broadcasted_iota(I32, g.shape, 1)
    logsig = jnp.minimum(g, 0.0) - jnp.log1p(jnp.exp(-jnp.abs(g)))
    gc_ref[...] = jnp.where(lane < N_GATES // 2, g, logsig)


def _conv_qkv(xm, row_block_offset, state, cw, cb, wqk, wv, wg, bg, *, n_groups, tiles_per_group,
              row_stride, k_scale):
    inner = xm.shape[1]
    rows = n_groups * tiles_per_group * ROW_TILE
    halo = state.shape[1]
    nt = tiles_per_group
    row_spec = pl.BlockSpec((ROW_TILE, inner), lambda b, t: (b * nt + t, 0))
    full = lambda a: pl.BlockSpec(a.shape, lambda b, t: (0,) * a.ndim)
    return pl.pallas_call(
        functools.partial(_conv_qkv_body, row_stride=row_stride, k_scale=k_scale),
        grid=(n_groups, nt),
        in_specs=[pl.BlockSpec((ROW_TILE, inner), lambda b, t: (row_block_offset + b * nt + t, 0)),
                  pl.BlockSpec((1, halo, inner), lambda b, t: (b, 0, 0)),
                  full(cw), full(cb), full(wqk), full(wv), full(wg), full(bg)],
        out_specs=[row_spec, row_spec, row_spec, row_spec,
                   pl.BlockSpec((ROW_TILE, LANES), lambda b, t: (b * nt + t, 0)),
                   pl.BlockSpec((1, halo, inner), lambda b, t: (b, 0, 0))],
        out_shape=[jax.ShapeDtypeStruct((rows, inner), BF16)] * 4
        + [jax.ShapeDtypeStruct((rows, LANES), F32),
           jax.ShapeDtypeStruct((n_groups, halo, inner), F32)],
        compiler_params=_cparams("parallel", "arbitrary"),
        name="conv_qkv",
    )(xm, state, cw, cb, wqk, wv, wg, bg)


def _cumsum_rows(x):
    n = x.shape[0]
    tri = (lax.broadcasted_iota(I32, (n, n), 1) <= lax.broadcasted_iota(I32, (n, n), 0)).astype(BF16)
    p0, p1, p2 = _split3(x)
    return _dot(tri, p0) + _dot(tri, p1) + _dot(tri, p2)


def _cumsum_lanes(x):
    n = x.shape[1]
    tri = (lax.broadcasted_iota(I32, (n, n), 0) <= lax.broadcasted_iota(I32, (n, n), 1)).astype(BF16)
    p0, p1, p2 = _split3(x)
    return _dot(p0, tri) + _dot(p1, tri) + _dot(p2, tri)


def _mlstm_chunk(q, k, v, ig_col, lf_col, bcum_col, ig_row, bcum_row, c_prev, n_prev, m_prev):
    n = q.shape[0]
    ri = lax.broadcasted_iota(I32, (n, n), 0)
    ci = lax.broadcasted_iota(I32, (n, n), 1)
    causal = ci <= ri
    log_d = jnp.where(causal, bcum_col - bcum_row + ig_row, -jnp.inf)
    m_inter = bcum_col + m_prev
    m_t = jnp.maximum(m_inter, jnp.max(log_d, axis=-1, keepdims=True))
    s = _dot_nt(q, k) * jnp.exp(log_d - m_t)
    inter = jnp.exp(m_inter - m_t)
    num = _dot(s.astype(BF16), v) + inter * _dot(q, c_prev.astype(BF16))
    den = jnp.sum(s, axis=-1, keepdims=True) + inter * jnp.sum(q.astype(F32) * n_prev, axis=-1, keepdims=True)
    h = num * (1.0 / jnp.maximum(jnp.abs(den), jnp.exp(-m_t)))
    m_new = m_t[n - 1:n, :]
    b_last = bcum_col[n - 1:n, :]
    w_col = jnp.exp(b_last - bcum_col + ig_col - m_new)
    decay = jnp.exp(b_last + m_prev - m_new)
    wk = w_col * k.astype(F32)
    c_new = decay * c_prev + _dot_tn(wk.astype(BF16), v)
    n_new = decay * n_prev + jnp.sum(wk, axis=0, keepdims=True)
    del lf_col
    return h, c_new, n_new, m_new


def _head_norm(h, w, eps=1e-5):
    mu = jnp.mean(h, axis=-1, keepdims=True)
    var = jnp.mean(jnp.square(h - mu), axis=-1, keepdims=True)
    return (h - mu) * lax.rsqrt(var + eps) * w


def _select_lane(x, idx):
    lane = lax.broadcasted_iota(I32, x.shape, 1)
    return jnp.sum(jnp.where(lane == idx, x, 0.0), axis=1, keepdims=True)


def _select_row(x, idx):
    row = lax.broadcasted_iota(I32, x.shape, 0)
    return jnp.sum(jnp.where(row == idx, x, 0.0), axis=0, keepdims=True)


def _cell_prompt_body(q_ref, k_ref, v_ref, gc_ref, gr_ref, hw_ref,
                      h_ref, c_ref, n_ref, m_ref):
    hd = pl.program_id(1)
    c = pl.program_id(2)

    @pl.when(c == 0)
    def _():
        c_ref[...] = jnp.zeros(c_ref.shape, F32)
        n_ref[...] = jnp.zeros(n_ref.shape, F32)
        m_ref[...] = jnp.full(m_ref.shape, STAB_INIT, F32)

    gc = gc_ref[...]
    gr = gr_ref[...]
    bc = _cumsum_rows(gc)
    br = _cumsum_lanes(gr)
    half = N_GATES // 2
    h, c_new, n_new, m_new = _mlstm_chunk(
        q_ref[...], k_ref[...], v_ref[...],
        _select_lane(gc, hd), _select_lane(gc, hd + half), _select_lane(bc, hd + half),
        _select_row(gr, hd), _select_row(br, hd + half),
        c_ref[0, 0], n_ref[0, 0], m_ref[0, 0, :, 0:1])
    h_ref[...] = _head_norm(h, hw_ref[0]).astype(BF16)
    c_ref[0, 0] = c_new
    n_ref[0, 0] = n_new
    m_ref[0, 0] = jnp.broadcast_to(m_new, m_ref.shape[2:])


def _cell_prompt(q, k, v, gc, gr, hn_w, *, batch, heads):
    tp, inner = q.shape
    dh = inner // heads
    seq = tp // batch
    ln = MLSTM_CHUNK_PROMPT
    nc = seq // ln
    qspec = pl.BlockSpec((ln, dh), lambda b, h, c: (b * nc + c, h))
    return pl.pallas_call(
        _cell_prompt_body,
        grid=(batch, heads, nc),
        in_specs=[qspec, qspec, qspec,
                  pl.BlockSpec((ln, LANES), lambda b, h, c: (b * nc + c, 0)),
                  pl.BlockSpec((SUBLANES, ln), lambda b, h, c: (0, b * nc + c)),
                  pl.BlockSpec((1, 1, dh), lambda b, h, c: (h, 0, 0))],
        out_specs=[qspec,
                   pl.BlockSpec((1, 1, dh, dh), lambda b, h, c: (b, h, 0, 0)),
                   pl.BlockSpec((1, 1, 1, dh), lambda b, h, c: (b, h, 0, 0)),
                   pl.BlockSpec((1, 1, 1, LANES), lambda b, h, c: (b, h, 0, 0))],
        out_shape=[jax.ShapeDtypeStruct((tp, inner), BF16),
                   jax.ShapeDtypeStruct((batch, heads, dh, dh), F32),
                   jax.ShapeDtypeStruct((batch, heads, 1, dh), F32),
                   jax.ShapeDtypeStruct((batch, heads, 1, LANES), F32)],
        compiler_params=_cparams("parallel", "parallel", "arbitrary"),
        name="mlstm_cell_prompt",
    )(q, k, v, gc, gr, hn_w)


def _cell_sample_body(q_ref, k_ref, v_ref, gc_ref, gr_ref, hw_ref, c0_ref, n0_ref, m0_ref,
                      h_ref, c_ref, n_ref, m_ref, *, heads, valid):
    ln = MLSTM_CHUNK_SAMPLE
    rows = q_ref.shape[1]
    dh = q_ref.shape[2] // heads
    half = N_GATES // 2
    gc = gc_ref[0]
    gr = gr_ref[0]
    row_c = lax.broadcasted_iota(I32, gc.shape, 0)
    lane_c = lax.broadcasted_iota(I32, gc.shape, 1)
    gc = jnp.where(row_c < valid, gc, jnp.where(lane_c < half, -jnp.inf, 0.0))
    row_r = lax.broadcasted_iota(I32, gr.shape, 0)
    lane_r = lax.broadcasted_iota(I32, gr.shape, 1)
    gr = jnp.where(lane_r < valid, gr, jnp.where(row_r < half, -jnp.inf, 0.0))
    bc = _cumsum_rows(jnp.where(lane_c < half, 0.0, gc))
    br = _cumsum_lanes(jnp.where(row_r < half, 0.0, gr))
    pad = jnp.zeros((ln - rows, dh), F32)

    def padded(ref, sl):
        return jnp.concatenate([ref[0, :, sl].astype(F32), pad], axis=0).astype(BF16)

    for hd in range(heads):
        sl = slice(hd * dh, (hd + 1) * dh)
        qh, kh, vh = padded(q_ref, sl), padded(k_ref, sl), padded(v_ref, sl)
        h, c_new, n_new, m_new = _mlstm_chunk(
            qh, kh, vh,
            gc[:, hd:hd + 1], gc[:, hd + half:hd + half + 1], bc[:, hd + half:hd + half + 1],
            gr[hd:hd + 1, :], br[hd + half:hd + half + 1, :],
            c0_ref[0, hd], n0_ref[0, hd], m0_ref[0, hd, :, 0:1])
        h_ref[0, :, sl] = _head_norm(h[:rows], hw_ref[hd]).astype(BF16)
        c_ref[0, hd] = c_new
        n_ref[0, hd] = n_new
        m_ref[0, hd] = jnp.broadcast_to(m_new, m_ref.shape[2:])


def _cell_sample(q, k, v, gc, gr, hn_w, c0, n0, m0, *, heads, valid, group_batch):
    n_groups, rows, wide = q.shape
    inner = wide // group_batch
    dh = inner // heads
    nb = n_groups * group_batch
    ln = MLSTM_CHUNK_SAMPLE
    qspec = pl.BlockSpec((1, rows, inner), lambda b: (b // group_batch, 0, b % group_batch))
    cspec = pl.BlockSpec((1, heads, dh, dh), lambda b: (b, 0, 0, 0))
    nspec = pl.BlockSpec((1, heads, 1, dh), lambda b: (b, 0, 0, 0))
    mspec = pl.BlockSpec((1, heads, 1, LANES), lambda b: (b, 0, 0, 0))
    return pl.pallas_call(
        functools.partial(_cell_sample_body, heads=heads, valid=valid),
        grid=(nb,),
        in_specs=[qspec, qspec, qspec,
                  pl.BlockSpec((1, ln, LANES), lambda b: (b, 0, 0)),
                  pl.BlockSpec((1, SUBLANES, ln), lambda b: (b, 0, 0)),
                  pl.BlockSpec((heads, 1, dh), lambda b: (0, 0, 0)),
                  cspec, nspec, mspec],
        out_specs=[qspec, cspec, nspec, mspec],
        out_shape=[jax.ShapeDtypeStruct(q.shape, BF16),
                   jax.ShapeDtypeStruct((nb, heads, dh, dh), F32),
                   jax.ShapeDtypeStruct((nb, heads, 1, dh), F32),
                   jax.ShapeDtypeStruct((nb, heads, 1, LANES), F32)],
        compiler_params=_cparams("parallel"),
        name="mlstm_cell_sample",
    )(q, k, v, gc, gr, hn_w, c0, n0, m0)


def _gate_down_body(hp_ref, hs_ref, cp_ref, cs_ref, z_ref, skip_ref, w_ref, xp_ref, xs_ref, o_ref,
                    *, n_prompt_tiles):
    i = pl.program_id(0)
    is_p = i < n_prompt_tiles
    hn = jnp.where(is_p, hp_ref[...], hs_ref[...]).astype(F32)
    xc = jnp.where(is_p, cp_ref[...], cs_ref[...]).astype(F32)
    xres = jnp.where(is_p, xp_ref[...], xs_ref[...])
    a = (hn + skip_ref[...] * xc) * _silu(z_ref[...].astype(F32))
    o_ref[...] = xres + _dot(a.astype(BF16), w_ref[...])


def _gate_down(hn_p, hn_s, xc_p, xc_s, z, skip, w, xp, xs):
    tp, inner = hn_p.shape
    tt = z.shape[0]
    d = w.shape[1]
    npt = tp // ROW_TILE
    ip, is_ = _two_src_specs(npt, inner)
    dp, ds = _two_src_specs(npt, d)
    return pl.pallas_call(
        functools.partial(_gate_down_body, n_prompt_tiles=npt),
        grid=(tt // ROW_TILE,),
        in_specs=[ip, is_, ip, is_,
                  pl.BlockSpec((ROW_TILE, inner), lambda i: (i, 0)),
                  pl.BlockSpec((1, inner), lambda i: (0, 0)),
                  pl.BlockSpec((inner, d), lambda i: (0, 0)),
                  dp, ds],
        out_specs=pl.BlockSpec((ROW_TILE, d), lambda i: (i, 0)),
        out_shape=jax.ShapeDtypeStruct((tt, d), F32),
        compiler_params=_cparams("parallel"),
        name="gate_down",
    )(hn_p, hn_s, xc_p, xc_s, z, skip, w, xp, xs)


def _cmlp_in_body(x_ref, g_ref, w_ref, b_ref, lw_ref, lb_ref, u_ref, v_ref, *, half):
    h = _rms(x_ref[...], g_ref[...]).astype(BF16)
    y = _dot(h, w_ref[...]) + b_ref[...]
    zz = y * (0.5 * (1.0 + jnp.tanh(math.sqrt(2.0 / math.pi) * (y + 0.044715 * (y * y * y)))))
    u_ref[...] = zz[:, :half].astype(BF16)
    v = zz[:, half:]
    mu = jnp.mean(v, axis=-1, keepdims=True)
    var = jnp.mean(jnp.square(v - mu), axis=-1, keepdims=True)
    v_ref[...] = (v - mu) * lax.rsqrt(var + 1e-5) * lw_ref[...] + lb_ref[...]


def _cmlp_in(x, g, w, b, lw, lb):
    tt, d = x.shape
    n2 = w.shape[1]
    half = n2 // 2
    vec = lambda n: pl.BlockSpec((1, n), lambda i: (0, 0))
    return pl.pallas_call(
        functools.partial(_cmlp_in_body, half=half),
        grid=(tt // ROW_TILE,),
        in_specs=[pl.BlockSpec((ROW_TILE, d), lambda i: (i, 0)), vec(d),
                  pl.BlockSpec((d, n2), lambda i: (0, 0)), vec(n2), vec(half), vec(half)],
        out_specs=[pl.BlockSpec((ROW_TILE, half), lambda i: (i, 0))] * 2,
        out_shape=[jax.ShapeDtypeStruct((tt, half), BF16), jax.ShapeDtypeStruct((tt, half), F32)],
        compiler_params=_cparams("parallel"),
        name="cmlp_in",
    )(x, g, w, b, lw, lb)


def _cmlp_mix_out_body(u_ref, v_ref, mix_ref, bias_ref, w_ref, bo_ref, x_ref, o_ref, *, groups):
    half = u_ref.shape[1]
    gd = half // groups
    pieces = []
    for g in range(groups):
        sl = slice(g * gd, (g + 1) * gd)
        mixed = _dot(mix_ref[0, g], v_ref[:, sl].astype(BF16)) + bias_ref[0, :, g:g + 1]
        pieces.append((u_ref[:, sl].astype(F32) * mixed).astype(BF16))
    a = jnp.concatenate(pieces, axis=1)
    o_ref[...] = x_ref[...] + _dot(a, w_ref[...]) + bo_ref[...]


def _cmlp_mix_out(u, v, mix, bias, w, bo, x, *, n_prompt_tiles):
    tt, half = u.shape
    d = w.shape[1]
    groups = mix.shape[1]
    kind = lambda i: jnp.where(i < n_prompt_tiles, 0, 1)
    return pl.pallas_call(
        functools.partial(_cmlp_mix_out_body, groups=groups),
        grid=(tt // ROW_TILE,),
        in_specs=[pl.BlockSpec((ROW_TILE, half), lambda i: (i, 0)),
                  pl.BlockSpec((ROW_TILE, half), lambda i: (i, 0)),
                  pl.BlockSpec((1, groups, ROW_TILE, ROW_TILE), lambda i: (kind(i), 0, 0, 0)),
                  pl.BlockSpec((1, ROW_TILE, groups), lambda i: (kind(i), 0, 0)),
                  pl.BlockSpec((half, d), lambda i: (0, 0)),
                  pl.BlockSpec((1, d), lambda i: (0, 0)),
                  pl.BlockSpec((ROW_TILE, d), lambda i: (i, 0))],
        out_specs=pl.BlockSpec((ROW_TILE, d), lambda i: (i, 0)),
        out_shape=jax.ShapeDtypeStruct((tt, d), F32),
        compiler_params=_cparams("parallel"),
        name="cmlp_mix_out",
    )(u, v, mix, bias, w, bo, x)


GROUP_ROW_OFFSET = SUBLANES


def _first_argmax_rows(x, n):
    row = lax.broadcasted_iota(I32, x.shape, 0)
    mx = jnp.max(x, axis=0, keepdims=True)
    idx = jnp.min(jnp.where(x == mx, row, n), axis=0, keepdims=True)
    return idx, mx


def _router_body(x_ref, g_ref, wt_ref, bt_ref, route_ref, cnt_ref, *, n_groups, per_group):
    i = pl.program_id(0)
    n_exp = n_groups * per_group

    @pl.when(i == 0)
    def _():
        cnt_ref[...] = jnp.zeros(cnt_ref.shape, F32)

    h = _rms(x_ref[...], g_ref[...])
    logits = _dot_nt_f32(wt_ref[...], h) + bt_ref[:, 0:1]
    tokens = logits.shape[1]
    gl = logits[0:n_groups, :]
    gmax = jnp.max(gl, axis=0, keepdims=True)
    ge = jnp.exp(gl - gmax)
    gp = ge / jnp.sum(ge, axis=0, keepdims=True)
    g_idx, g_w = _first_argmax_rows(gp, n_groups)
    e_sel = jnp.zeros((per_group, tokens), F32)
    for g in range(n_groups):
        lo = GROUP_ROW_OFFSET + g * per_group
        e_sel = e_sel + jnp.where(g_idx == g, logits[lo:lo + per_group, :], 0.0)
    i1, m1 = _first_argmax_rows(e_sel, per_group)
    row = lax.broadcasted_iota(I32, e_sel.shape, 0)
    i2, m2 = _first_argmax_rows(jnp.where(row == i1, -jnp.inf, e_sel), per_group)
    t = jnp.exp(m2 - m1)
    w1 = (1.0 / (1.0 + t)) * g_w
    w2 = (t / (1.0 + t)) * g_w
    e1 = g_idx * per_group + i1
    e2 = g_idx * per_group + i2

    erow = lax.broadcasted_iota(I32, (n_exp, tokens), 0)
    oh1 = erow == e1
    oh2 = erow == e2
    onehot = jnp.where(oh1 | oh2, 1.0, 0.0)
    strict_upper = (lax.broadcasted_iota(I32, (tokens, tokens), 0)
                    < lax.broadcasted_iota(I32, (tokens, tokens), 1)).astype(BF16)
    before = _dot(onehot.astype(BF16), strict_upper) + cnt_ref[:, 0:1]
    r1 = jnp.sum(jnp.where(oh1, before, 0.0), axis=0, keepdims=True)
    r2 = jnp.sum(jnp.where(oh2, before, 0.0), axis=0, keepdims=True)
    cnt_ref[...] = cnt_ref[...] + jnp.sum(onehot, axis=1, keepdims=True)
    zero = jnp.zeros_like(w1)
    route_ref[...] = jnp.concatenate(
        [e1.astype(F32), e2.astype(F32), r1, r2, w1, w2, zero, zero], axis=0)


def _router(x, g, wt, bt, *, n_groups, per_group):
    tt, d = x.shape
    n_exp = n_groups * per_group
    return pl.pallas_call(
        functools.partial(_router_body, n_groups=n_groups, per_group=per_group),
        grid=(tt // ROW_TILE,),
        in_specs=[pl.BlockSpec((ROW_TILE, d), lambda i: (i, 0)),
                  pl.BlockSpec((1, d), lambda i: (0, 0)),
                  pl.BlockSpec(wt.shape, lambda i: (0, 0)),
                  pl.BlockSpec(bt.shape, lambda i: (0, 0))],
        out_specs=[pl.BlockSpec((SUBLANES, ROW_TILE), lambda i: (0, i)),
                   pl.BlockSpec((n_exp, LANES), lambda i: (0, 0))],
        out_shape=[jax.ShapeDtypeStruct((SUBLANES, tt), F32),
                   jax.ShapeDtypeStruct((n_exp, LANES), F32)],
        compiler_params=_cparams("arbitrary"),
        name="moe_router",
    )(x, g, wt, bt)


def _positions_body(route_ref, off_ref, pos_ref):
    r = route_ref[...]
    n_exp = off_ref.shape[0]
    tokens = r.shape[1]
    erow = lax.broadcasted_iota(I32, (n_exp, tokens), 0)
    off = off_ref[:, 0:1]
    out = []
    for k in range(2):
        e = r[k:k + 1, :].astype(I32)
        base = jnp.sum(jnp.where(erow == e, off, 0.0), axis=0, keepdims=True)
        out.append((base + r[2 + k:3 + k, :]).astype(I32))
    pos_ref[...] = jnp.concatenate(out, axis=0)


def _positions(route, off):
    tt = route.shape[1]
    return pl.pallas_call(
        _positions_body,
        grid=(tt // ROW_TILE,),
        in_specs=[pl.BlockSpec((SUBLANES, ROW_TILE), lambda i: (0, i)),
                  pl.BlockSpec(off.shape, lambda i: (0, 0))],
        out_specs=pl.BlockSpec((2, ROW_TILE), lambda i: (0, i)),
        out_shape=jax.ShapeDtypeStruct((2, tt), I32),
        compiler_params=_cparams("parallel"),
        name="moe_positions",
    )(route, off)


def _row_copy(src, src_row, dst, dst_row, sem):
    return pltpu.make_async_copy(src.at[pl.ds(src_row, 1)], dst.at[pl.ds(dst_row, 1)], sem)


def _dispatch_body(pos_ref, x_hbm, xs_hbm, sem, *, n_tokens):
    i = pl.program_id(0)
    base = i * ROW_TILE

    def copies(r):
        t = base + r
        return [_row_copy(x_hbm, t, xs_hbm, pos_ref[k * n_tokens + t], sem) for k in range(2)]

    def start(r, carry):
        for cp in copies(r):
            cp.start()
        return carry

    def wait(r, carry):
        for cp in copies(r):
            cp.wait()
        return carry

    lax.fori_loop(0, ROW_TILE, start, 0, unroll=8)
    lax.fori_loop(0, ROW_TILE, wait, 0, unroll=8)


def _dispatch(pos_flat, x):
    tt, d = x.shape
    return pl.pallas_call(
        functools.partial(_dispatch_body, n_tokens=tt),
        grid_spec=pltpu.PrefetchScalarGridSpec(
            num_scalar_prefetch=1,
            grid=(tt // ROW_TILE,),
            in_specs=[pl.BlockSpec(memory_space=pl.ANY)],
            out_specs=pl.BlockSpec(memory_space=pl.ANY),
            scratch_shapes=[pltpu.SemaphoreType.DMA]),
        out_shape=jax.ShapeDtypeStruct((2 * tt, d), F32),
        compiler_params=_cparams("arbitrary"),
        name="moe_dispatch",
    )(pos_flat, x)


def _experts_body(tile_ref, exp_ref, lo_ref, hi_ref, nwork_ref,
                  xs_ref, g_ref, wg_ref, wu_ref, wd_ref, o_ref):
    w = pl.program_id(0)
    first = jnp.logical_or(w == 0, tile_ref[w] != tile_ref[jnp.maximum(w - 1, 0)])

    @pl.when(w < nwork_ref[0])
    def _():
        xb = _rms(xs_ref[...], g_ref[...]).astype(BF16)
        a = _dot(xb, wg_ref[0])
        b = _dot(xb, wu_ref[0])
        hg = (_silu(a) * b).astype(BF16)
        y = _dot(hg, wd_ref[0])
        row = lax.broadcasted_iota(I32, y.shape, 0)
        y = jnp.where((row >= lo_ref[w]) & (row < hi_ref[w]), y, 0.0)

        @pl.when(first)
        def _():
            o_ref[...] = y

        @pl.when(jnp.logical_not(first))
        def _():
            o_ref[...] += y


def _experts(meta, xs, g, wg, wu, wd):
    rows, d = xs.shape
    n_work = meta[0].shape[0]
    f = wg.shape[2]
    return pl.pallas_call(
        _experts_body,
        grid_spec=pltpu.PrefetchScalarGridSpec(
            num_scalar_prefetch=5,
            grid=(n_work,),
            in_specs=[pl.BlockSpec((ROW_TILE, d), lambda w, tl, ex, lo, hi, nw: (tl[w], 0)),
                      pl.BlockSpec((1, d), lambda w, tl, ex, lo, hi, nw: (0, 0)),
                      pl.BlockSpec((1, d, f), lambda w, tl, ex, lo, hi, nw: (ex[w], 0, 0)),
                      pl.BlockSpec((1, d, f), lambda w, tl, ex, lo, hi, nw: (ex[w], 0, 0)),
                      pl.BlockSpec((1, f, d), lambda w, tl, ex, lo, hi, nw: (ex[w], 0, 0))],
            out_specs=pl.BlockSpec((ROW_TILE, d), lambda w, tl, ex, lo, hi, nw: (tl[w], 0))),
        out_shape=jax.ShapeDtypeStruct((rows, d), F32),
        compiler_params=_cparams("arbitrary"),
        name="moe_experts",
    )(*meta, xs, g, wg, wu, wd)


def _combine_body(pos_ref, x_ref, route_ref, ys_hbm, o_ref, buf, sem, *, n_tokens):
    i = pl.program_id(0)
    base = i * ROW_TILE

    def copies(r):
        t = base + r
        return [_row_copy(ys_hbm, pos_ref[k * n_tokens + t], buf.at[k], r, sem) for k in range(2)]

    def start(r, carry):
        for cp in copies(r):
            cp.start()
        return carry

    def wait(r, carry):
        for cp in copies(r):
            cp.wait()
        return carry

    lax.fori_loop(0, ROW_TILE, start, 0, unroll=8)
    r = route_ref[...]
    rt = jnp.concatenate([r, jnp.zeros((LANES - SUBLANES, r.shape[1]), F32)], axis=0).T
    lax.fori_loop(0, ROW_TILE, wait, 0, unroll=8)
    o_ref[...] = x_ref[...] + rt[:, 4:5] * buf[0] + rt[:, 5:6] * buf[1]


def _combine(pos_flat, x, route, ys):
    tt, d = x.shape
    return pl.pallas_call(
        functools.partial(_combine_body, n_tokens=tt),
        grid_spec=pltpu.PrefetchScalarGridSpec(
            num_scalar_prefetch=1,
            grid=(tt // ROW_TILE,),
            in_specs=[pl.BlockSpec((ROW_TILE, d), lambda i, p: (i, 0)),
                      pl.BlockSpec((SUBLANES, ROW_TILE), lambda i, p: (0, i)),
                      pl.BlockSpec(memory_space=pl.ANY)],
            out_specs=pl.BlockSpec((ROW_TILE, d), lambda i, p: (i, 0)),
            scratch_shapes=[pltpu.VMEM((2, ROW_TILE, d), F32), pltpu.SemaphoreType.DMA]),
        out_shape=jax.ShapeDtypeStruct((tt, d), F32),
        compiler_params=_cparams("arbitrary"),
        name="moe_combine",
    )(pos_flat, x, route, ys)


def _work_items(counts, n_rows):
    n_exp = counts.shape[0]
    n_tiles = n_rows // ROW_TILE
    n_work = n_tiles + n_exp - 1
    off = jnp.concatenate([jnp.zeros((1,), I32), jnp.cumsum(counts)])
    first_tile = off[:-1] // ROW_TILE
    last_tile = (off[1:] - 1) // ROW_TILE
    per_exp = jnp.where(counts > 0, last_tile - first_tile + 1, 0)
    wend = jnp.cumsum(per_exp)
    wstart = wend - per_exp
    total = wend[-1]
    w = jnp.arange(n_work, dtype=I32)
    e = jnp.minimum(jnp.sum((wend[None, :] <= w[:, None]).astype(I32), axis=1), n_exp - 1)
    tile = first_tile[e] + (w - wstart[e])
    lo = jnp.maximum(off[e], tile * ROW_TILE) - tile * ROW_TILE
    hi = jnp.minimum(off[e + 1], (tile + 1) * ROW_TILE) - tile * ROW_TILE
    live = w < total
    tile = jnp.where(live, tile, n_tiles - 1)
    lo = jnp.where(live, lo, 0)
    hi = jnp.where(live, hi, 0)
    return off, (tile.astype(I32), e.astype(I32), lo.astype(I32), hi.astype(I32),
                 total.reshape(1).astype(I32))


def _moe(x, g, w_rg, b_rg, w_re, b_re, w_gate, w_up, w_down):
    tt, d = x.shape
    n_groups, per_group = w_re.shape[0], w_re.shape[2]
    n_exp = n_groups * per_group
    f = w_gate.shape[-1]
    rows = GROUP_ROW_OFFSET + n_exp
    wt = jnp.zeros((rows, d), F32)
    wt = wt.at[:n_groups].set(w_rg.T)
    wt = wt.at[GROUP_ROW_OFFSET:].set(jnp.transpose(w_re, (0, 2, 1)).reshape(n_exp, d))
    bt = jnp.zeros((rows,), F32).at[:n_groups].set(b_rg).at[GROUP_ROW_OFFSET:].set(b_re.reshape(n_exp))
    bt = jnp.broadcast_to(bt[:, None], (rows, LANES))
    g2 = g.reshape(1, d)

    route, cnt = _router(x, g2, wt, bt, n_groups=n_groups, per_group=per_group)
    counts = cnt[:, 0].astype(I32)
    off, meta = _work_items(counts, 2 * tt)
    off_b = jnp.broadcast_to(off[:n_exp].astype(F32)[:, None], (n_exp, LANES))
    pos = _positions(route, off_b).reshape(2 * tt)
    xs = _dispatch(pos, x)
    ys = _experts(meta, xs, g2,
                  w_gate.reshape(n_exp, d, f).astype(BF16),
                  w_up.reshape(n_exp, d, f).astype(BF16),
                  w_down.reshape(n_exp, f, d).astype(BF16))
    return _combine(pos, x, route, ys)


def _final_norm_body(x_ref, g_ref, o_ref):
    o_ref[...] = _rms(x_ref[...], g_ref[...])


def _final_norm(x, g, row_block_offset, n_rows):
    d = x.shape[1]
    return pl.pallas_call(
        _final_norm_body,
        grid=(n_rows // ROW_TILE,),
        in_specs=[pl.BlockSpec((ROW_TILE, d), lambda i: (row_block_offset + i, 0)),
                  pl.BlockSpec((1, d), lambda i: (0, 0))],
        out_specs=pl.BlockSpec((ROW_TILE, d), lambda i: (i, 0)),
        out_shape=jax.ShapeDtypeStruct((n_rows, d), F32),
        compiler_params=_cparams("parallel"),
        name="final_norm",
    )(x, g)


def _block_diag_tiles(w, tile):
    nb, c, _ = w.shape
    per = tile // c
    wt = w.reshape(nb // per, per, c, c)
    eye = jnp.eye(per, dtype=w.dtype)
    return jnp.einsum('tpcd,pq->tpcqd', wt, eye).reshape(nb // per, tile, tile)


def _sample_to_rows(x, group_batch):
    db, s, w = x.shape
    return x.reshape(db // group_batch, group_batch, s, w).transpose(0, 2, 1, 3).reshape(db * s, w)


def _rows_to_sample(x, group_batch, s):
    rows, w = x.shape
    db = rows // s
    return x.reshape(db // group_batch, s, group_batch, w).transpose(0, 2, 1, 3).reshape(db, s, w)


def kernel(x_prompt, x_sample, state_mlstm_C, state_mlstm_n, state_mlstm_m, state_mlstm_conv,
           norm_mix, norm_ffn, norm_final,
           mlstm_w_up, mlstm_conv_w, mlstm_conv_b, mlstm_w_q, mlstm_w_k, mlstm_w_v,
           mlstm_w_ig, mlstm_b_ig, mlstm_w_fg, mlstm_b_fg, mlstm_skip, mlstm_hn_w, mlstm_w_down,
           cmlp_w_in, cmlp_b_in, cmlp_ln_w, cmlp_ln_b, cmlp_w_s, cmlp_b_s, cmlp_w_out, cmlp_b_out,
           moe_w_rg, moe_b_rg, moe_w_re, moe_b_re, moe_w_gate, moe_w_up, moe_w_down):
    batch, seq, d = x_prompt.shape
    dec_batch, dec_seq, _ = x_sample.shape
    heads = state_mlstm_C.shape[2]
    dh = state_mlstm_C.shape[3]
    inner = heads * dh
    tp, ts = batch * seq, dec_batch * dec_seq
    npt = tp // ROW_TILE
    gb = ROW_TILE // dec_seq
    n_sgroups = dec_batch // gb
    half_g = N_GATES // 2
    assert heads == half_g and tp % ROW_TILE == 0 and ts % ROW_TILE == 0
    assert ROW_TILE % dec_seq == 0 and dec_batch % gb == 0 and seq % MLSTM_CHUNK_PROMPT == 0
    assert seq % ROW_TILE == 0 and ROW_TILE % CMLP_CHUNK == 0 and dec_seq <= SUBLANES

    xp = x_prompt.reshape(tp, d)
    xs = _sample_to_rows(x_sample, gb)

    xm, z = _norm_up(xp, xs, norm_mix[0].reshape(1, d), mlstm_w_up[0].astype(BF16))

    wq = _block_diag_tiles(mlstm_w_q[0], MXU_WIDTH)
    wk = _block_diag_tiles(mlstm_w_k[0], MXU_WIDTH)
    wqk = jnp.concatenate([wq, wk], axis=2).astype(BF16)
    wv = _block_diag_tiles(mlstm_w_v[0], MXU_WIDTH).astype(BF16)
    wg = jnp.concatenate([mlstm_w_ig[0], mlstm_w_fg[0]], axis=1)
    wg = jnp.pad(wg, ((0, 0), (0, LANES - N_GATES))).reshape(3, inner, LANES).astype(BF16)
    bg = jnp.pad(jnp.concatenate([mlstm_b_ig[0], mlstm_b_fg[0]]), (0, LANES - N_GATES)).reshape(1, LANES)
    cw, cb = mlstm_conv_w[0], mlstm_conv_b[0].reshape(1, inner)
    k_scale = float(dh) ** -0.5

    conv0_p = jnp.zeros((batch, SUBLANES, inner), F32)
    q_p, k_p, v_p, xc_p, gc_p, conv_p = _conv_qkv(
        xm, 0, conv0_p, cw, cb, wqk, wv, wg, bg,
        n_groups=batch, tiles_per_group=seq // ROW_TILE, row_stride=1, k_scale=k_scale)
    halo_s = (CONV_WIDTH - 1) * gb
    conv0_s = state_mlstm_conv[0].reshape(n_sgroups, gb, CONV_WIDTH - 1, inner)
    conv0_s = conv0_s.transpose(0, 2, 1, 3).reshape(n_sgroups, halo_s, inner)
    q_s, k_s, v_s, xc_s, gc_s, conv_s = _conv_qkv(
        xm, npt, conv0_s, cw, cb, wqk, wv, wg, bg,
        n_groups=n_sgroups, tiles_per_group=1, row_stride=gb, k_scale=k_scale)

    hn_w = mlstm_hn_w[0].reshape(heads, 1, dh)
    gr_p = jnp.transpose(gc_p[:, :SUBLANES])
    hn_p, c_p, n_p, m_p = _cell_prompt(q_p, k_p, v_p, gc_p, gr_p, hn_w, batch=batch, heads=heads)

    def pad_steps(a):
        a = a.reshape(n_sgroups, dec_seq, gb * inner)
        return jnp.pad(a, ((0, 0), (0, SUBLANES - dec_seq), (0, 0)))

    ln_s = MLSTM_CHUNK_SAMPLE
    gcs = gc_s.reshape(n_sgroups, dec_seq, gb, LANES).transpose(0, 2, 1, 3).reshape(dec_batch, dec_seq, LANES)
    gcs = jnp.pad(gcs, ((0, 0), (0, ln_s - dec_seq), (0, 0)))
    grs = jnp.transpose(gcs[:, :, :SUBLANES], (0, 2, 1))
    m0 = jnp.broadcast_to(state_mlstm_m[0][:, :, None, None], (dec_batch, heads, 1, LANES))
    hn_s, c_s, n_s, m_s = _cell_sample(
        pad_steps(q_s), pad_steps(k_s), pad_steps(v_s), gcs, grs, hn_w,
        state_mlstm_C[0], state_mlstm_n[0].reshape(dec_batch, heads, 1, dh), m0,
        heads=heads, valid=dec_seq, group_batch=gb)
    hn_s = hn_s[:, :dec_seq].reshape(ts, inner)

    x1 = _gate_down(hn_p, hn_s, xc_p, xc_s, z, mlstm_skip[0].reshape(1, inner),
                    mlstm_w_down[0].astype(BF16), xp, xs)
    x2 = _moe(x1, norm_ffn[0], moe_w_rg[0], moe_b_rg[0], moe_w_re[0], moe_b_re[0],
              moe_w_gate[0], moe_w_up[0], moe_w_down[0])

    half = cmlp_w_in.shape[2] // 2
    groups = cmlp_w_s.shape[1]
    u, vn = _cmlp_in(x2, norm_mix[1].reshape(1, d), cmlp_w_in[0].astype(BF16),
                     cmlp_b_in[0].reshape(1, 2 * half), cmlp_ln_w[0].reshape(1, half),
                     cmlp_ln_b[0].reshape(1, half))
    causal = jnp.tril(jnp.ones((CMLP_CHUNK, CMLP_CHUNK), dtype=bool))
    ws = jnp.where(causal, cmlp_w_s[0], 0.0)
    reps = ROW_TILE // CMLP_CHUNK
    mix_p = jnp.einsum('gts,pq->gptqs', ws, jnp.eye(reps, dtype=F32)).reshape(groups, ROW_TILE, ROW_TILE)
    mix_s = jnp.einsum('gts,pq->gtpsq', ws[:, :dec_seq, :dec_seq],
                       jnp.eye(gb, dtype=F32)).reshape(groups, ROW_TILE, ROW_TILE)
    mix = jnp.stack([mix_p, mix_s]).astype(BF16)
    bias_t = jnp.transpose(cmlp_b_s[0])
    bias = jnp.stack([jnp.tile(bias_t, (reps, 1)), jnp.repeat(bias_t[:dec_seq], gb, axis=0)])
    x3 = _cmlp_mix_out(u, vn, mix, bias, cmlp_w_out[0].astype(BF16), cmlp_b_out[0].reshape(1, d), x2,
                       n_prompt_tiles=npt)
    x4 = _moe(x3, norm_ffn[1], moe_w_rg[1], moe_b_rg[1], moe_w_re[1], moe_b_re[1],
              moe_w_gate[1], moe_w_up[1], moe_w_down[1])

    gfin = norm_final.reshape(1, d)
    y_prompt = _final_norm(x4, gfin, 0, tp).reshape(batch, seq, d)
    y_sample = _rows_to_sample(_final_norm(x4, gfin, npt, ts), gb, dec_seq)

    conv_prompt = conv_p[:, SUBLANES - (CONV_WIDTH - 1):, :][None]
    conv_sample = conv_s.reshape(n_sgroups, CONV_WIDTH - 1, gb, inner).transpose(0, 2, 1, 3)
    conv_sample = conv_sample.reshape(dec_batch, CONV_WIDTH - 1, inner)[None]
    v_sample = _rows_to_sample(vn[tp:], gb, dec_seq)[None]
    return (y_prompt, y_sample,
            c_p[None], n_p[:, :, 0, :][None], m_p[:, :, 0, 0][None], conv_prompt,
            c_s[None], n_s[:, :, 0, :][None], m_s[:, :, 0, 0][None], conv_sample,
            v_sample)
```

```python
import functools
import math

import jax
import jax.numpy as jnp
from jax import lax
from jax.experimental import pallas as pl
from jax.experimental.pallas import tpu as pltpu

F32 = jnp.float32
BF16 = jnp.bfloat16
I32 = jnp.int32

LANES = 128
SUBLANES = 8
ROW_TILE = 256
MXU_WIDTH = 256
VMEM_LIMIT = 56 * 1024 * 1024
CONV_WIDTH = 4
QKV_BLOCK = 4
MLSTM_CHUNK_PROMPT = 256
MLSTM_CHUNK_SAMPLE = 128
CMLP_CHUNK = 128
STAB_INIT = -1e30
N_GATES = 8


def _cparams(*sem):
    return pltpu.CompilerParams(dimension_semantics=sem, vmem_limit_bytes=VMEM_LIMIT)


def _rms(x, g, eps=1e-6):
    return x * lax.rsqrt(jnp.mean(x * x, axis=-1, keepdims=True) + eps) * g


def _silu(x):
    return x * (1.0 / (1.0 + jnp.exp(-x)))


def _split3(x):
    p0 = x.astype(BF16)
    r = x - p0.astype(F32)
    p1 = r.astype(BF16)
    p2 = (r - p1.astype(F32)).astype(BF16)
    return p0, p1, p2


def _dot(a, b):
    return jnp.dot(a, b, preferred_element_type=F32)


def _dot_nt(a, b):
    return lax.dot_general(a, b, (((1,), (1,)), ((), ())), preferred_element_type=F32)


def _dot_tn(a, b):
    return lax.dot_general(a, b, (((0,), (0,)), ((), ())), preferred_element_type=F32)


def _dot_nt_f32(a, b):
    a0, a1, a2 = _split3(a)
    b0, b1, b2 = _split3(b)
    acc = _dot_nt(a0, b0)
    acc += _dot_nt(a0, b1) + _dot_nt(a1, b0)
    acc += _dot_nt(a1, b1) + _dot_nt(a0, b2) + _dot_nt(a2, b0)
    return acc


def _norm_up_body(xp_ref, xs_ref, g_ref, w_ref, xm_ref, z_ref, *, n_prompt_tiles, inner):
    i = pl.program_id(0)
    x = jnp.where(i < n_prompt_tiles, xp_ref[...], xs_ref[...])
    h = _rms(x, g_ref[...]).astype(BF16)
    u = _dot(h, w_ref[...])
    xm_ref[...] = u[:, :inner]
    z_ref[...] = u[:, inner:].astype(BF16)


def _two_src_specs(n_prompt_tiles, width):
    sp = pl.BlockSpec((ROW_TILE, width), lambda i: (jnp.minimum(i, n_prompt_tiles - 1), 0))
    ss = pl.BlockSpec((ROW_TILE, width), lambda i: (jnp.maximum(i - n_prompt_tiles, 0), 0))
    return sp, ss


def _norm_up(xp, xs, g, w):
    tp, d = xp.shape
    ts = xs.shape[0]
    tt = tp + ts
    n2 = w.shape[1]
    inner = n2 // 2
    npt = tp // ROW_TILE
    sp, ss = _two_src_specs(npt, d)
    return pl.pallas_call(
        functools.partial(_norm_up_body, n_prompt_tiles=npt, inner=inner),
        grid=(tt // ROW_TILE,),
        in_specs=[sp, ss,
                  pl.BlockSpec((1, d), lambda i: (0, 0)),
                  pl.BlockSpec((d, n2), lambda i: (0, 0))],
        out_specs=[pl.BlockSpec((ROW_TILE, inner), lambda i: (i, 0)),
                   pl.BlockSpec((ROW_TILE, inner), lambda i: (i, 0))],
        out_shape=[jax.ShapeDtypeStruct((tt, inner), F32),
                   jax.ShapeDtypeStruct((tt, inner), BF16)],
        compiler_params=_cparams("parallel"),
        name="norm_up",
    )(xp, xs, g, w)


def _conv_qkv_body(xm_ref, st_ref, cw_ref, cb_ref, wqk_ref, wv_ref, wg_ref, bg_ref,
                   q_ref, k_ref, v_ref, xc_ref, gc_ref, st_out_ref,
                   *, row_stride, k_scale):
    t = pl.program_id(1)
    halo = st_ref.shape[1]
    rows, inner = xm_ref.shape

    @pl.when(t == 0)
    def _():
        st_out_ref[0] = st_ref[0]

    x = xm_ref[...]
    xe = jnp.concatenate([st_out_ref[0], x], axis=0)
    acc = x * cw_ref[CONV_WIDTH - 1:CONV_WIDTH, :] + cb_ref[...]
    for j in range(CONV_WIDTH - 1):
        lo = halo - (CONV_WIDTH - 1 - j) * row_stride
        acc = acc + xe[lo:lo + rows, :] * cw_ref[j:j + 1, :]
    st_out_ref[0] = xe[rows:rows + halo, :]
    xc = _silu(acc)
    xc_b = xc.astype(BF16)
    xm_b = x.astype(BF16)
    xc_ref[...] = xc_b

    gacc = jnp.zeros((rows, LANES), F32)
    for c in range(inner // MXU_WIDTH):
        sl = slice(c * MXU_WIDTH, (c + 1) * MXU_WIDTH)
        qk = _dot(xc_b[:, sl], wqk_ref[c])
        vv = _dot(xm_b[:, sl], wv_ref[c])
        qb = qk[:, :MXU_WIDTH].astype(BF16)
        kb = qk[:, MXU_WIDTH:].astype(BF16)
        vb = vv.astype(BF16)
        q_ref[:, sl] = qb
        k_ref[:, sl] = (qk[:, MXU_WIDTH:] * k_scale).astype(BF16)
        v_ref[:, sl] = vb
        gacc += _dot(qb, wg_ref[0, sl, :]) + _dot(kb, wg_ref[1, sl, :]) + _dot(vb, wg_ref[2, sl, :])
    g = gacc + bg_ref[...]
    lane = lax.broadcasted_iota(I32, g.shape, 1)
    logsig = jnp.minimum(g, 0.0) - jnp.log1p(jnp.exp(-jnp.abs(g)))
    gc_ref[...] = jnp.where(lane < N_GATES // 2, g, logsig)


def _conv_qkv(xm, row_block_offset, state, cw, cb, wqk, wv, wg, bg, *, n_groups, tiles_per_group,
              row_stride, k_scale):
    inner = xm.shape[1]
    rows = n_groups * tiles_per_group * ROW_TILE
    halo = state.shape[1]
    nt = tiles_per_group
    row_spec = pl.BlockSpec((ROW_TILE, inner), lambda b, t: (b * nt + t, 0))
    full = lambda a: pl.BlockSpec(a.shape, lambda b, t: (0,) * a.ndim)
    return pl.pallas_call(
        functools.partial(_conv_qkv_body, row_stride=row_stride, k_scale=k_scale),
        grid=(n_groups, nt),
        in_specs=[pl.BlockSpec((ROW_TILE, inner), lambda b, t: (row_block_offset + b * nt + t, 0)),
                  pl.BlockSpec((1, halo, inner), lambda b, t: (b, 0, 0)),
                  full(cw), full(cb), full(wqk), full(wv), full(wg), full(bg)],
        out_specs=[row_spec, row_spec, row_spec, row_spec,
                   pl.BlockSpec((ROW_TILE, LANES), lambda b, t: (b * nt + t, 0)),
                   pl.BlockSpec((1, halo, inner), lambda b, t: (b, 0, 0))],
        out_shape=[jax.ShapeDtypeStruct((rows, inner), BF16)] * 4
        + [jax.ShapeDtypeStruct((rows, LANES), F32),
           jax.ShapeDtypeStruct((n_groups, halo, inner), F32)],
        compiler_params=_cparams("parallel", "arbitrary"),
        name="conv_qkv",
    )(xm, state, cw, cb, wqk, wv, wg, bg)


def _cumsum_rows(x):
    n = x.shape[0]
    tri = (lax.broadcasted_iota(I32, (n, n), 1) <= lax.broadcasted_iota(I32, (n, n), 0)).astype(BF16)
    p0, p1, p2 = _split3(x)
    return _dot(tri, p0) + _dot(tri, p1) + _dot(tri, p2)


def _cumsum_lanes(x):
    n = x.shape[1]
    tri = (lax.broadcasted_iota(I32, (n, n), 0) <= lax.broadcasted_iota(I32, (n, n), 1)).astype(BF16)
    p0, p1, p2 = _split3(x)
    return _dot(p0, tri) + _dot(p1, tri) + _dot(p2, tri)


def _mlstm_chunk(q, k, v, ig_col, lf_col, bcum_col, ig_row, bcum_row, c_prev, n_prev, m_prev):
    n = q.shape[0]
    ri = lax.broadcasted_iota(I32, (n, n), 0)
    ci = lax.broadcasted_iota(I32, (n, n), 1)
    causal = ci <= ri
    log_d = jnp.where(causal, bcum_col - bcum_row + ig_row, -jnp.inf)
    m_inter = bcum_col + m_prev
    m_t = jnp.maximum(m_inter, jnp.max(log_d, axis=-1, keepdims=True))
    s = _dot_nt(q, k) * jnp.exp(log_d - m_t)
    inter = jnp.exp(m_inter - m_t)
    num = _dot(s.astype(BF16), v) + inter * _dot(q, c_prev.astype(BF16))
    den = jnp.sum(s, axis=-1, keepdims=True) + inter * jnp.sum(q.astype(F32) * n_prev, axis=-1, keepdims=True)
    h = num * (1.0 / jnp.maximum(jnp.abs(den), jnp.exp(-m_t)))
    m_new = m_t[n - 1:n, :]
    b_last = bcum_col[n - 1:n, :]
    w_col = jnp.exp(b_last - bcum_col + ig_col - m_new)
    decay = jnp.exp(b_last + m_prev - m_new)
    wk = w_col * k.astype(F32)
    c_new = decay * c_prev + _dot_tn(wk.astype(BF16), v)
    n_new = decay * n_prev + jnp.sum(wk, axis=0, keepdims=True)
    del lf_col
    return h, c_new, n_new, m_new


def _head_norm(h, w, eps=1e-5):
    mu = jnp.mean(h, axis=-1, keepdims=True)
    var = jnp.mean(jnp.square(h - mu), axis=-1, keepdims=True)
    return (h - mu) * lax.rsqrt(var + eps) * w


def _select_lane(x, idx):
    lane = lax.broadcasted_iota(I32, x.shape, 1)
    return jnp.sum(jnp.where(lane == idx, x, 0.0), axis=1, keepdims=True)


def _select_row(x, idx):
    row = lax.broadcasted_iota(I32, x.shape, 0)
    return jnp.sum(jnp.where(row == idx, x, 0.0), axis=0, keepdims=True)


def _cell_prompt_body(q_ref, k_ref, v_ref, gc_ref, gr_ref, hw_ref,
                      h_ref, c_ref, n_ref, m_ref):
    hd = pl.program_id(1)
    c = pl.program_id(2)

    @pl.when(c == 0)
    def _():
        c_ref[...] = jnp.zeros(c_ref.shape, F32)
        n_ref[...] = jnp.zeros(n_ref.shape, F32)
        m_ref[...] = jnp.full(m_ref.shape, STAB_INIT, F32)

    gc = gc_ref[...]
    gr = gr_ref[...]
    bc = _cumsum_rows(gc)
    br = _cumsum_lanes(gr)
    half = N_GATES // 2
    h, c_new, n_new, m_new = _mlstm_chunk(
        q_ref[...], k_ref[...], v_ref[...],
        _select_lane(gc, hd), _select_lane(gc, hd + half), _select_lane(bc, hd + half),
        _select_row(gr, hd), _select_row(br, hd + half),
        c_ref[0, 0], n_ref[0, 0], m_ref[0, 0, :, 0:1])
    h_ref[...] = _head_norm(h, hw_ref[0]).astype(BF16)
    c_ref[0, 0] = c_new
    n_ref[0, 0] = n_new
    m_ref[0, 0] = jnp.broadcast_to(m_new, m_ref.shape[2:])


def _cell_prompt(q, k, v, gc, gr, hn_w, *, batch, heads):
    tp, inner = q.shape
    dh = inner // heads
    seq = tp // batch
    ln = MLSTM_CHUNK_PROMPT
    nc = seq // ln
    qspec = pl.BlockSpec((ln, dh), lambda b, h, c: (b * nc + c, h))
    return pl.pallas_call(
        _cell_prompt_body,
        grid=(batch, heads, nc),
        in_specs=[qspec, qspec, qspec,
                  pl.BlockSpec((ln, LANES), lambda b, h, c: (b * nc + c, 0)),
                  pl.BlockSpec((SUBLANES, ln), lambda b, h, c: (0, b * nc + c)),
                  pl.BlockSpec((1, 1, dh), lambda b, h, c: (h, 0, 0))],
        out_specs=[qspec,
                   pl.BlockSpec((1, 1, dh, dh), lambda b, h, c: (b, h, 0, 0)),
                   pl.BlockSpec((1, 1, 1, dh), lambda b, h, c: (b, h, 0, 0)),
                   pl.BlockSpec((1, 1, 1, LANES), lambda b, h, c: (b, h, 0, 0))],
        out_shape=[jax.ShapeDtypeStruct((tp, inner), BF16),
                   jax.ShapeDtypeStruct((batch, heads, dh, dh), F32),
                   jax.ShapeDtypeStruct((batch, heads, 1, dh), F32),
                   jax.ShapeDtypeStruct((batch, heads, 1, LANES), F32)],
        compiler_params=_cparams("parallel", "parallel", "arbitrary"),
        name="mlstm_cell_prompt",
    )(q, k, v, gc, gr, hn_w)


def _cell_sample_body(q_ref, k_ref, v_ref, gc_ref, gr_ref, hw_ref, c0_ref, n0_ref, m0_ref,
                      h_ref, c_ref, n_ref, m_ref, *, heads, valid):
    ln = MLSTM_CHUNK_SAMPLE
    rows = q_ref.shape[1]
    dh = q_ref.shape[2] // heads
    half = N_GATES // 2
    gc = gc_ref[0]
    gr = gr_ref[0]
    row_c = lax.broadcasted_iota(I32, gc.shape, 0)
    lane_c = lax.broadcasted_iota(I32, gc.shape, 1)
    gc = jnp.where(row_c < valid, gc, jnp.where(lane_c < half, -jnp.inf, 0.0))
    row_r = lax.broadcasted_iota(I32, gr.shape, 0)
    lane_r = lax.broadcasted_iota(I32, gr.shape, 1)
    gr = jnp.where(lane_r < valid, gr, jnp.where(row_r < half, -jnp.inf, 0.0))
    bc = _cumsum_rows(jnp.where(lane_c < half, 0.0, gc))
    br = _cumsum_lanes(jnp.where(row_r < half, 0.0, gr))
    pad = jnp.zeros((ln - rows, dh), F32)

    def padded(ref, sl):
        return jnp.concatenate([ref[0, :, sl].astype(F32), pad], axis=0).astype(BF16)

    for hd in range(heads):
        sl = slice(hd * dh, (hd + 1) * dh)
        qh, kh, vh = padded(q_ref, sl), padded(k_ref, sl), padded(v_ref, sl)
        h, c_new, n_new, m_new = _mlstm_chunk(
            qh, kh, vh,
            gc[:, hd:hd + 1], gc[:, hd + half:hd + half + 1], bc[:, hd + half:hd + half + 1],
            gr[hd:hd + 1, :], br[hd + half:hd + half + 1, :],
            c0_ref[0, hd], n0_ref[0, hd], m0_ref[0, hd, :, 0:1])
        h_ref[0, :, sl] = _head_norm(h[:rows], hw_ref[hd]).astype(BF16)
        c_ref[0, hd] = c_new
        n_ref[0, hd] = n_new
        m_ref[0, hd] = jnp.broadcast_to(m_new, m_ref.shape[2:])


def _cell_sample(q, k, v, gc, gr, hn_w, c0, n0, m0, *, heads, valid, group_batch):
    n_groups, rows, wide = q.shape
    inner = wide // group_batch
    dh = inner // heads
    nb = n_groups * group_batch
    ln = MLSTM_CHUNK_SAMPLE
    qspec = pl.BlockSpec((1, rows, inner), lambda b: (b // group_batch, 0, b % group_batch))
    cspec = pl.BlockSpec((1, heads, dh, dh), lambda b: (b, 0, 0, 0))
    nspec = pl.BlockSpec((1, heads, 1, dh), lambda b: (b, 0, 0, 0))
    mspec = pl.BlockSpec((1, heads, 1, LANES), lambda b: (b, 0, 0, 0))
    return pl.pallas_call(
        functools.partial(_cell_sample_body, heads=heads, valid=valid),
        grid=(nb,),
        in_specs=[qspec, qspec, qspec,
                  pl.BlockSpec((1, ln, LANES), lambda b: (b, 0, 0)),
                  pl.BlockSpec((1, SUBLANES, ln), lambda b: (b, 0, 0)),
                  pl.BlockSpec((heads, 1, dh), lambda b: (0, 0, 0)),
                  cspec, nspec, mspec],
        out_specs=[qspec, cspec, nspec, mspec],
        out_shape=[jax.ShapeDtypeStruct(q.shape, BF16),
                   jax.ShapeDtypeStruct((nb, heads, dh, dh), F32),
                   jax.ShapeDtypeStruct((nb, heads, 1, dh), F32),
                   jax.ShapeDtypeStruct((nb, heads, 1, LANES), F32)],
        compiler_params=_cparams("parallel"),
        name="mlstm_cell_sample",
    )(q, k, v, gc, gr, hn_w, c0, n0, m0)


def _gate_down_body(hp_ref, hs_ref, cp_ref, cs_ref, z_ref, skip_ref, w_ref, xp_ref, xs_ref, o_ref,
                    *, n_prompt_tiles):
    i = pl.program_id(0)
    is_p = i < n_prompt_tiles
    hn = jnp.where(is_p, hp_ref[...], hs_ref[...]).astype(F32)
    xc = jnp.where(is_p, cp_ref[...], cs_ref[...]).astype(F32)
    xres = jnp.where(is_p, xp_ref[...], xs_ref[...])
    a = (hn + skip_ref[...] * xc) * _silu(z_ref[...].astype(F32))
    o_ref[...] = xres + _dot(a.astype(BF16), w_ref[...])


def _gate_down(hn_p, hn_s, xc_p, xc_s, z, skip, w, xp, xs):
    tp, inner = hn_p.shape
    tt = z.shape[0]
    d = w.shape[1]
    npt = tp // ROW_TILE
    ip, is_ = _two_src_specs(npt, inner)
    dp, ds = _two_src_specs(npt, d)
    return pl.pallas_call(
        functools.partial(_gate_down_body, n_prompt_tiles=npt),
        grid=(tt // ROW_TILE,),
        in_specs=[ip, is_, ip, is_,
                  pl.BlockSpec((ROW_TILE, inner), lambda i: (i, 0)),
                  pl.BlockSpec((1, inner), lambda i: (0, 0)),
                  pl.BlockSpec((inner, d), lambda i: (0, 0)),
                  dp, ds],
        out_specs=pl.BlockSpec((ROW_TILE, d), lambda i: (i, 0)),
        out_shape=jax.ShapeDtypeStruct((tt, d), F32),
        compiler_params=_cparams("parallel"),
        name="gate_down",
    )(hn_p, hn_s, xc_p, xc_s, z, skip, w, xp, xs)


def _cmlp_in_body(x_ref, g_ref, w_ref, b_ref, lw_ref, lb_ref, u_ref, v_ref, *, half):
    h = _rms(x_ref[...], g_ref[...]).astype(BF16)
    y = _dot(h, w_ref[...]) + b_ref[...]
    zz = y * (0.5 * (1.0 + jnp.tanh(math.sqrt(2.0 / math.pi) * (y + 0.044715 * (y * y * y)))))
    u_ref[...] = zz[:, :half].astype(BF16)
    v = zz[:, half:]
    mu = jnp.mean(v, axis=-1, keepdims=True)
    var = jnp.mean(jnp.square(v - mu), axis=-1, keepdims=True)
    v_ref[...] = (v - mu) * lax.rsqrt(var + 1e-5) * lw_ref[...] + lb_ref[...]


def _cmlp_in(x, g, w, b, lw, lb):
    tt, d = x.shape
    n2 = w.shape[1]
    half = n2 // 2
    vec = lambda n: pl.BlockSpec((1, n), lambda i: (0, 0))
    return pl.pallas_call(
        functools.partial(_cmlp_in_body, half=half),
        grid=(tt // ROW_TILE,),
        in_specs=[pl.BlockSpec((ROW_TILE, d), lambda i: (i, 0)), vec(d),
                  pl.BlockSpec((d, n2), lambda i: (0, 0)), vec(n2), vec(half), vec(half)],
        out_specs=[pl.BlockSpec((ROW_TILE, half), lambda i: (i, 0))] * 2,
        out_shape=[jax.ShapeDtypeStruct((tt, half), BF16), jax.ShapeDtypeStruct((tt, half), F32)],
        compiler_params=_cparams("parallel"),
        name="cmlp_in",
    )(x, g, w, b, lw, lb)


def _cmlp_mix_out_body(u_ref, v_ref, mix_ref, bias_ref, w_ref, bo_ref, x_ref, o_ref, *, groups):
    half = u_ref.shape[1]
    gd = half // groups
    pieces = []
    for g in range(groups):
        sl = slice(g * gd, (g + 1) * gd)
        mixed = _dot(mix_ref[0, g], v_ref[:, sl].astype(BF16)) + bias_ref[0, :, g:g + 1]
        pieces.append((u_ref[:, sl].astype(F32) * mixed).astype(BF16))
    a = jnp.concatenate(pieces, axis=1)
    o_ref[...] = x_ref[...] + _dot(a, w_ref[...]) + bo_ref[...]


def _cmlp_mix_out(u, v, mix, bias, w, bo, x, *, n_prompt_tiles):
    tt, half = u.shape
    d = w.shape[1]
    groups = mix.shape[1]
    kind = lambda i: jnp.where(i < n_prompt_tiles, 0, 1)
    return pl.pallas_call(
        functools.partial(_cmlp_mix_out_body, groups=groups),
        grid=(tt // ROW_TILE,),
        in_specs=[pl.BlockSpec((ROW_TILE, half), lambda i: (i, 0)),
                  pl.BlockSpec((ROW_TILE, half), lambda i: (i, 0)),
                  pl.BlockSpec((1, groups, ROW_TILE, ROW_TILE), lambda i: (kind(i), 0, 0, 0)),
                  pl.BlockSpec((1, ROW_TILE, groups), lambda i: (kind(i), 0, 0)),
                  pl.BlockSpec((half, d), lambda i: (0, 0)),
                  pl.BlockSpec((1, d), lambda i: (0, 0)),
                  pl.BlockSpec((ROW_TILE, d), lambda i: (i, 0))],
        out_specs=pl.BlockSpec((ROW_TILE, d), lambda i: (i, 0)),
        out_shape=jax.ShapeDtypeStruct((tt, d), F32),
        compiler_params=_cparams("parallel"),
        name="cmlp_mix_out",
    )(u, v, mix, bias, w, bo, x)


GROUP_ROW_OFFSET = SUBLANES


def _first_argmax_rows(x, n):
    row = lax.broadcasted_iota(I32, x.shape, 0)
    mx = jnp.max(x, axis=0, keepdims=True)
    idx = jnp.min(jnp.where(x == mx, row, n), axis=0, keepdims=True)
    return idx, mx


def _router_body(x_ref, g_ref, wt_ref, bt_ref, route_ref, cnt_ref, *, n_groups, per_group):
    i = pl.program_id(0)
    n_exp = n_groups * per_group

    @pl.when(i == 0)
    def _():
        cnt_ref[...] = jnp.zeros(cnt_ref.shape, F32)

    h = _rms(x_ref[...], g_ref[...])
    logits = _dot_nt_f32(wt_ref[...], h) + bt_ref[:, 0:1]
    tokens = logits.shape[1]
    gl = logits[0:n_groups, :]
    gmax = jnp.max(gl, axis=0, keepdims=True)
    ge = jnp.exp(gl - gmax)
    gp = ge / jnp.sum(ge, axis=0, keepdims=True)
    g_idx, g_w = _first_argmax_rows(gp, n_groups)
    e_sel = jnp.zeros((per_group, tokens), F32)
    for g in range(n_groups):
        lo = GROUP_ROW_OFFSET + g * per_group
        e_sel = e_sel + jnp.where(g_idx == g, logits[lo:lo + per_group, :], 0.0)
    i1, m1 = _first_argmax_rows(e_sel, per_group)
    row = lax.broadcasted_iota(I32, e_sel.shape, 0)
    i2, m2 = _first_argmax_rows(jnp.where(row == i1, -jnp.inf, e_sel), per_group)
    t = jnp.exp(m2 - m1)
    w1 = (1.0 / (1.0 + t)) * g_w
    w2 = (t / (1.0 + t)) * g_w
    e1 = g_idx * per_group + i1
    e2 = g_idx * per_group + i2

    erow = lax.broadcasted_iota(I32, (n_exp, tokens), 0)
    oh1 = erow == e1
    oh2 = erow == e2
    onehot = jnp.where(oh1 | oh2, 1.0, 0.0)
    strict_upper = (lax.broadcasted_iota(I32, (tokens, tokens), 0)
                    < lax.broadcasted_iota(I32, (tokens, tokens), 1)).astype(BF16)
    before = _dot(onehot.astype(BF16), strict_upper) + cnt_ref[:, 0:1]
    r1 = jnp.sum(jnp.where(oh1, before, 0.0), axis=0, keepdims=True)
    r2 = jnp.sum(jnp.where(oh2, before, 0.0), axis=0, keepdims=True)
    cnt_ref[...] = cnt_ref[...] + jnp.sum(onehot, axis=1, keepdims=True)
    zero = jnp.zeros_like(w1)
    route_ref[...] = jnp.concatenate(
        [e1.astype(F32), e2.astype(F32), r1, r2, w1, w2, zero, zero], axis=0)


def _router(x, g, wt, bt, *, n_groups, per_group):
    tt, d = x.shape
    n_exp = n_groups * per_group
    return pl.pallas_call(
        functools.partial(_router_body, n_groups=n_groups, per_group=per_group),
        grid=(tt // ROW_TILE,),
        in_specs=[pl.BlockSpec((ROW_TILE, d), lambda i: (i, 0)),
                  pl.BlockSpec((1, d), lambda i: (0, 0)),
                  pl.BlockSpec(wt.shape, lambda i: (0, 0)),
                  pl.BlockSpec(bt.shape, lambda i: (0, 0))],
        out_specs=[pl.BlockSpec((SUBLANES, ROW_TILE), lambda i: (0, i)),
                   pl.BlockSpec((n_exp, LANES), lambda i: (0, 0))],
        out_shape=[jax.ShapeDtypeStruct((SUBLANES, tt), F32),
                   jax.ShapeDtypeStruct((n_exp, LANES), F32)],
        compiler_params=_cparams("arbitrary"),
        name="moe_router",
    )(x, g, wt, bt)


def _positions_body(route_ref, off_ref, pos_ref):
    r = route_ref[...]
    n_exp = off_ref.shape[0]
    tokens = r.shape[1]
    erow = lax.broadcasted_iota(I32, (n_exp, tokens), 0)
    off = off_ref[:, 0:1]
    out = []
    for k in range(2):
        e = r[k:k + 1, :].astype(I32)
        base = jnp.sum(jnp.where(erow == e, off, 0.0), axis=0, keepdims=True)
        out.append((base + r[2 + k:3 + k, :]).astype(I32))
    pos_ref[...] = jnp.concatenate(out, axis=0)


def _positions(route, off):
    tt = route.shape[1]
    return pl.pallas_call(
        _positions_body,
        grid=(tt // ROW_TILE,),
        in_specs=[pl.BlockSpec((SUBLANES, ROW_TILE), lambda i: (0, i)),
                  pl.BlockSpec(off.shape, lambda i: (0, 0))],
        out_specs=pl.BlockSpec((2, ROW_TILE), lambda i: (0, i)),
        out_shape=jax.ShapeDtypeStruct((2, tt), I32),
        compiler_params=_cparams("parallel"),
        name="moe_positions",
    )(route, off)


def _row_copy(src, src_row, dst, dst_row, sem):
    return pltpu.make_async_copy(src.at[pl.ds(src_row, 1)], dst.at[pl.ds(dst_row, 1)], sem)


def _dispatch_body(pos_ref, x_ref, xs_hbm, sem, *, n_tokens):
    i = pl.program_id(0)
    base = i * ROW_TILE

    def copies(r):
        return [_row_copy(x_ref, r, xs_hbm, pos_ref[k * n_tokens + base + r], sem) for k in range(2)]

    def start(r, carry):
        for cp in copies(r):
            cp.start()
        return carry

    def wait(r, carry):
        for cp in copies(r):
            cp.wait()
        return carry

    lax.fori_loop(0, ROW_TILE, start, 0, unroll=8)
    lax.fori_loop(0, ROW_TILE, wait, 0, unroll=8)


def _dispatch(pos_flat, x):
    tt, d = x.shape
    return pl.pallas_call(
        functools.partial(_dispatch_body, n_tokens=tt),
        grid_spec=pltpu.PrefetchScalarGridSpec(
            num_scalar_prefetch=1,
            grid=(tt // ROW_TILE,),
            in_specs=[pl.BlockSpec((ROW_TILE, d), lambda i, p: (i, 0))],
            out_specs=pl.BlockSpec(memory_space=pl.ANY),
            scratch_shapes=[pltpu.SemaphoreType.DMA]),
        out_shape=jax.ShapeDtypeStruct((2 * tt, d), F32),
        compiler_params=_cparams("arbitrary"),
        name="moe_dispatch",
    )(pos_flat, x)


def _experts_body(tile_ref, exp_ref, lo_ref, hi_ref, nwork_ref,
                  xs_ref, g_ref, wg_ref, wu_ref, wd_ref, o_ref, wg_b, wu_b, wd_b):
    w = pl.program_id(0)
    prev = jnp.maximum(w - 1, 0)
    first = jnp.logical_or(w == 0, tile_ref[w] != tile_ref[prev])
    new_expert = jnp.logical_or(w == 0, exp_ref[w] != exp_ref[prev])

    @pl.when(w < nwork_ref[0])
    def _():
        @pl.when(new_expert)
        def _():
            wg_b[...] = wg_ref[0].astype(BF16)
            wu_b[...] = wu_ref[0].astype(BF16)
            wd_b[...] = wd_ref[0].astype(BF16)

        xb = _rms(xs_ref[...], g_ref[...]).astype(BF16)
        a = _dot(xb, wg_b[...])
        b = _dot(xb, wu_b[...])
        hg = (_silu(a) * b).astype(BF16)
        y = _dot(hg, wd_b[...])
        row = lax.broadcasted_iota(I32, y.shape, 0)
        y = jnp.where((row >= lo_ref[w]) & (row < hi_ref[w]), y, 0.0)

        @pl.when(first)
        def _():
            o_ref[...] = y

        @pl.when(jnp.logical_not(first))
        def _():
            o_ref[...] += y


def _experts(meta, xs, g, wg, wu, wd):
    rows, d = xs.shape
    n_work = meta[0].shape[0]
    f = wg.shape[2]
    return pl.pallas_call(
        _experts_body,
        grid_spec=pltpu.PrefetchScalarGridSpec(
            num_scalar_prefetch=5,
            grid=(n_work,),
            in_specs=[pl.BlockSpec((ROW_TILE, d), lambda w, tl, ex, lo, hi, nw: (tl[w], 0)),
                      pl.BlockSpec((1, d), lambda w, tl, ex, lo, hi, nw: (0, 0)),
                      pl.BlockSpec((1, d, f), lambda w, tl, ex, lo, hi, nw: (ex[w], 0, 0)),
                      pl.BlockSpec((1, d, f), lambda w, tl, ex, lo, hi, nw: (ex[w], 0, 0)),
                      pl.BlockSpec((1, f, d), lambda w, tl, ex, lo, hi, nw: (ex[w], 0, 0))],
            out_specs=pl.BlockSpec((ROW_TILE, d), lambda w, tl, ex, lo, hi, nw: (tl[w], 0)),
            scratch_shapes=[pltpu.VMEM((d, f), BF16), pltpu.VMEM((d, f), BF16), pltpu.VMEM((f, d), BF16)]),
        out_shape=jax.ShapeDtypeStruct((rows, d), F32),
        compiler_params=_cparams("arbitrary"),
        name="moe_experts",
    )(*meta, xs, g, wg, wu, wd)


def _combine_body(pos_ref, x_ref, route_ref, ys_hbm, o_ref, buf, sem, *, n_tokens):
    i = pl.program_id(0)
    base = i * ROW_TILE

    def copies(r):
        t = base + r
        return [_row_copy(ys_hbm, pos_ref[k * n_tokens + t], buf.at[k], r, sem) for k in range(2)]

    def start(r, carry):
        for cp in copies(r):
            cp.start()
        return carry

    def wait(r, carry):
        for cp in copies(r):
            cp.wait()
        return carry

    lax.fori_loop(0, ROW_TILE, start, 0, unroll=8)
    r = route_ref[...]
    rt = jnp.concatenate([r, jnp.zeros((LANES - SUBLANES, r.shape[1]), F32)], axis=0).T
    lax.fori_loop(0, ROW_TILE, wait, 0, unroll=8)
    o_ref[...] = x_ref[...] + rt[:, 4:5] * buf[0] + rt[:, 5:6] * buf[1]


def _combine(pos_flat, x, route, ys):
    tt, d = x.shape
    return pl.pallas_call(
        functools.partial(_combine_body, n_tokens=tt),
        grid_spec=pltpu.PrefetchScalarGridSpec(
            num_scalar_prefetch=1,
            grid=(tt // ROW_TILE,),
            in_specs=[pl.BlockSpec((ROW_TILE, d), lambda i, p: (i, 0)),
                      pl.BlockSpec((SUBLANES, ROW_TILE), lambda i, p: (0, i)),
                      pl.BlockSpec(memory_space=pl.ANY)],
            out_specs=pl.BlockSpec((ROW_TILE, d), lambda i, p: (i, 0)),
            scratch_shapes=[pltpu.VMEM((2, ROW_TILE, d), F32), pltpu.SemaphoreType.DMA]),
        out_shape=jax.ShapeDtypeStruct((tt, d), F32),
        compiler_params=_cparams("arbitrary"),
        name="moe_combine",
    )(pos_flat, x, route, ys)


def _work_items(counts, n_rows):
    n_exp = counts.shape[0]
    n_tiles = n_rows // ROW_TILE
    n_work = n_tiles + n_exp - 1
    off = jnp.concatenate([jnp.zeros((1,), I32), jnp.cumsum(counts)])
    first_tile = off[:-1] // ROW_TILE
    last_tile = (off[1:] - 1) // ROW_TILE
    per_exp = jnp.where(counts > 0, last_tile - first_tile + 1, 0)
    wend = jnp.cumsum(per_exp)
    wstart = wend - per_exp
    total = wend[-1]
    w = jnp.arange(n_work, dtype=I32)
    e = jnp.minimum(jnp.sum((wend[None, :] <= w[:, None]).astype(I32), axis=1), n_exp - 1)
    tile = first_tile[e] + (w - wstart[e])
    lo = jnp.maximum(off[e], tile * ROW_TILE) - tile * ROW_TILE
    hi = jnp.minimum(off[e + 1], (tile + 1) * ROW_TILE) - tile * ROW_TILE
    live = w < total
    tile = jnp.where(live, tile, n_tiles - 1)
    lo = jnp.where(live, lo, 0)
    hi = jnp.where(live, hi, 0)
    return off, (tile.astype(I32), e.astype(I32), lo.astype(I32), hi.astype(I32),
                 total.reshape(1).astype(I32))


def _moe(x, g, w_rg, b_rg, w_re, b_re, w_gate, w_up, w_down):
    tt, d = x.shape
    n_groups, per_group = w_re.shape[0], w_re.shape[2]
    n_exp = n_groups * per_group
    f = w_gate.shape[-1]
    rows = GROUP_ROW_OFFSET + n_exp
    wt = jnp.zeros((rows, d), F32)
    wt = wt.at[:n_groups].set(w_rg.T)
    wt = wt.at[GROUP_ROW_OFFSET:].set(jnp.transpose(w_re, (0, 2, 1)).reshape(n_exp, d))
    bt = jnp.zeros((rows,), F32).at[:n_groups].set(b_rg).at[GROUP_ROW_OFFSET:].set(b_re.reshape(n_exp))
    bt = jnp.broadcast_to(bt[:, None], (rows, LANES))
    g2 = g.reshape(1, d)

    route, cnt = _router(x, g2, wt, bt, n_groups=n_groups, per_group=per_group)
    counts = cnt[:, 0].astype(I32)
    off, meta = _work_items(counts, 2 * tt)
    off_b = jnp.broadcast_to(off[:n_exp].astype(F32)[:, None], (n_exp, LANES))
    pos = _positions(route, off_b).reshape(2 * tt)
    xs = _dispatch(pos, x)
    ys = _experts(meta, xs, g2,
                  w_gate.reshape(n_exp, d, f), w_up.reshape(n_exp, d, f), w_down.reshape(n_exp, f, d))
    return _combine(pos, x, route, ys)


def _final_norm_body(x_ref, g_ref, o_ref):
    o_ref[...] = _rms(x_ref[...], g_ref[...])


def _final_norm(x, g, row_block_offset, n_rows):
    d = x.shape[1]
    return pl.pallas_call(
        _final_norm_body,
        grid=(n_rows // ROW_TILE,),
        in_specs=[pl.BlockSpec((ROW_TILE, d), lambda i: (row_block_offset + i, 0)),
                  pl.BlockSpec((1, d), lambda i: (0, 0))],
        out_specs=pl.BlockSpec((ROW_TILE, d), lambda i: (i, 0)),
        out_shape=jax.ShapeDtypeStruct((n_rows, d), F32),
        compiler_params=_cparams("parallel"),
        name="final_norm",
    )(x, g)


def _block_diag_tiles(w, tile):
    nb, c, _ = w.shape
    per = tile // c
    wt = w.reshape(nb // per, per, c, c)
    eye = jnp.eye(per, dtype=w.dtype)
    return jnp.einsum('tpcd,pq->tpcqd', wt, eye).reshape(nb // per, tile, tile)


def _sample_to_rows(x, group_batch):
    db, s, w = x.shape
    return x.reshape(db // group_batch, group_batch, s, w).transpose(0, 2, 1, 3).reshape(db * s, w)


def _rows_to_sample(x, group_batch, s):
    rows, w = x.shape
    db = rows // s
    return x.reshape(db // group_batch, s, group_batch, w).transpose(0, 2, 1, 3).reshape(db, s, w)


def kernel(x_prompt, x_sample, state_mlstm_C, state_mlstm_n, state_mlstm_m, state_mlstm_conv,
           norm_mix, norm_ffn, norm_final,
           mlstm_w_up, mlstm_conv_w, mlstm_conv_b, mlstm_w_q, mlstm_w_k, mlstm_w_v,
           mlstm_w_ig, mlstm_b_ig, mlstm_w_fg, mlstm_b_fg, mlstm_skip, mlstm_hn_w, mlstm_w_down,
           cmlp_w_in, cmlp_b_in, cmlp_ln_w, cmlp_ln_b, cmlp_w_s, cmlp_b_s, cmlp_w_out, cmlp_b_out,
           moe_w_rg, moe_b_rg, moe_w_re, moe_b_re, moe_w_gate, moe_w_up, moe_w_down):
    batch, seq, d = x_prompt.shape
    dec_batch, dec_seq, _ = x_sample.shape
    heads = state_mlstm_C.shape[2]
    dh = state_mlstm_C.shape[3]
    inner = heads * dh
    tp, ts = batch * seq, dec_batch * dec_seq
    npt = tp // ROW_TILE
    gb = ROW_TILE // dec_seq
    n_sgroups = dec_batch // gb
    half_g = N_GATES // 2
    assert heads == half_g and tp % ROW_TILE == 0 and ts % ROW_TILE == 0
    assert ROW_TILE % dec_seq == 0 and dec_batch % gb == 0 and seq % MLSTM_CHUNK_PROMPT == 0
    assert seq % ROW_TILE == 0 and ROW_TILE % CMLP_CHUNK == 0 and dec_seq <= SUBLANES

    xp = x_prompt.reshape(tp, d)
    xs = _sample_to_rows(x_sample, gb)

    xm, z = _norm_up(xp, xs, norm_mix[0].reshape(1, d), mlstm_w_up[0].astype(BF16))

    wq = _block_diag_tiles(mlstm_w_q[0], MXU_WIDTH)
    wk = _block_diag_tiles(mlstm_w_k[0], MXU_WIDTH)
    wqk = jnp.concatenate([wq, wk], axis=2).astype(BF16)
    wv = _block_diag_tiles(mlstm_w_v[0], MXU_WIDTH).astype(BF16)
    wg = jnp.concatenate([mlstm_w_ig[0], mlstm_w_fg[0]], axis=1)
    wg = jnp.pad(wg, ((0, 0), (0, LANES - N_GATES))).reshape(3, inner, LANES).astype(BF16)
    bg = jnp.pad(jnp.concatenate([mlstm_b_ig[0], mlstm_b_fg[0]]), (0, LANES - N_GATES)).reshape(1, LANES)
    cw, cb = mlstm_conv_w[0], mlstm_conv_b[0].reshape(1, inner)
    k_scale = float(dh) ** -0.5

    conv0_p = jnp.zeros((batch, SUBLANES, inner), F32)
    q_p, k_p, v_p, xc_p, gc_p, conv_p = _conv_qkv(
        xm, 0, conv0_p, cw, cb, wqk, wv, wg, bg,
        n_groups=batch, tiles_per_group=seq // ROW_TILE, row_stride=1, k_scale=k_scale)
    halo_s = (CONV_WIDTH - 1) * gb
    conv0_s = state_mlstm_conv[0].reshape(n_sgroups, gb, CONV_WIDTH - 1, inner)
    conv0_s = conv0_s.transpose(0, 2, 1, 3).reshape(n_sgroups, halo_s, inner)
    q_s, k_s, v_s, xc_s, gc_s, conv_s = _conv_qkv(
        xm, npt, conv0_s, cw, cb, wqk, wv, wg, bg,
        n_groups=n_sgroups, tiles_per_group=1, row_stride=gb, k_scale=k_scale)

    hn_w = mlstm_hn_w[0].reshape(heads, 1, dh)
    gr_p = jnp.transpose(gc_p[:, :SUBLANES])
    hn_p, c_p, n_p, m_p = _cell_prompt(q_p, k_p, v_p, gc_p, gr_p, hn_w, batch=batch, heads=heads)

    def pad_steps(a):
        a = a.reshape(n_sgroups, dec_seq, gb * inner)
        return jnp.pad(a, ((0, 0), (0, SUBLANES - dec_seq), (0, 0)))

    ln_s = MLSTM_CHUNK_SAMPLE
    gcs = gc_s.reshape(n_sgroups, dec_seq, gb, LANES).transpose(0, 2, 1, 3).reshape(dec_batch, dec_seq, LANES)
    gcs = jnp.pad(gcs, ((0, 0), (0, ln_s - dec_seq), (0, 0)))
    grs = jnp.transpose(gcs[:, :, :SUBLANES], (0, 2, 1))
    m0 = jnp.broadcast_to(state_mlstm_m[0][:, :, None, None], (dec_batch, heads, 1, LANES))
    hn_s, c_s, n_s, m_s = _cell_sample(
        pad_steps(q_s), pad_steps(k_s), pad_steps(v_s), gcs, grs, hn_w,
        state_mlstm_C[0], state_mlstm_n[0].reshape(dec_batch, heads, 1, dh), m0,
        heads=heads, valid=dec_seq, group_batch=gb)
    hn_s = hn_s[:, :dec_seq].reshape(ts, inner)

    x1 = _gate_down(hn_p, hn_s, xc_p, xc_s, z, mlstm_skip[0].reshape(1, inner),
                    mlstm_w_down[0].astype(BF16), xp, xs)
    x2 = _moe(x1, norm_ffn[0], moe_w_rg[0], moe_b_rg[0], moe_w_re[0], moe_b_re[0],
              moe_w_gate[0], moe_w_up[0], moe_w_down[0])

    half = cmlp_w_in.shape[2] // 2
    groups = cmlp_w_s.shape[1]
    u, vn = _cmlp_in(x2, norm_mix[1].reshape(1, d), cmlp_w_in[0].astype(BF16),
                     cmlp_b_in[0].reshape(1, 2 * half), cmlp_ln_w[0].reshape(1, half),
                     cmlp_ln_b[0].reshape(1, half))
    causal = jnp.tril(jnp.ones((CMLP_CHUNK, CMLP_CHUNK), dtype=bool))
    ws = jnp.where(causal, cmlp_w_s[0], 0.0)
    reps = ROW_TILE // CMLP_CHUNK
    mix_p = jnp.einsum('gts,pq->gptqs', ws, jnp.eye(reps, dtype=F32)).reshape(groups, ROW_TILE, ROW_TILE)
    mix_s = jnp.einsum('gts,pq->gtpsq', ws[:, :dec_seq, :dec_seq],
                       jnp.eye(gb, dtype=F32)).reshape(groups, ROW_TILE, ROW_TILE)
    mix = jnp.stack([mix_p, mix_s]).astype(BF16)
    bias_t = jnp.transpose(cmlp_b_s[0])
    bias = jnp.stack([jnp.tile(bias_t, (reps, 1)), jnp.repeat(bias_t[:dec_seq], gb, axis=0)])
    x3 = _cmlp_mix_out(u, vn, mix, bias, cmlp_w_out[0].astype(BF16), cmlp_b_out[0].reshape(1, d), x2,
                       n_prompt_tiles=npt)
    x4 = _moe(x3, norm_ffn[1], moe_w_rg[1], moe_b_rg[1], moe_w_re[1], moe_b_re[1],
              moe_w_gate[1], moe_w_up[1], moe_w_down[1])

    gfin = norm_final.reshape(1, d)
    y_prompt = _final_norm(x4, gfin, 0, tp).reshape(batch, seq, d)
    y_sample = _rows_to_sample(_final_norm(x4, gfin, npt, ts), gb, dec_seq)

    conv_prompt = conv_p[:, SUBLANES - (CONV_WIDTH - 1):, :][None]
    conv_sample = conv_s.reshape(n_sgroups, CONV_WIDTH - 1, gb, inner).transpose(0, 2, 1, 3)
    conv_sample = conv_sample.reshape(dec_batch, CONV_WIDTH - 1, inner)[None]
    v_sample = _rows_to_sample(vn[tp:], gb, dec_seq)[None]
    return (y_prompt, y_sample,
            c_p[None], n_p[:, :, 0, :][None], m_p[:, :, 0, 0][None], conv_prompt,
            c_s[None], n_s[:, :, 0, :][None], m_s[:, :, 0, 0][None], conv_sample,
            v_sample)
```

```python
import functools
import math

import jax
import jax.numpy as jnp
from jax import lax
from jax.experimental import pallas as pl
from jax.experimental.pallas import tpu as pltpu

F32 = jnp.float32
BF16 = jnp.bfloat16
I32 = jnp.int32

LANES = 128
SUBLANES = 8
ROW_TILE = 256
EXPERT_TILE = 512
MXU_WIDTH = 256
VMEM_LIMIT = 56 * 1024 * 1024
CONV_WIDTH = 4
QKV_BLOCK = 4
MLSTM_CHUNK_PROMPT = 256
MLSTM_CHUNK_SAMPLE = 128
CMLP_CHUNK = 128
STAB_INIT = -1e30
N_GATES = 8


def _cparams(*sem):
    return pltpu.CompilerParams(dimension_semantics=sem, vmem_limit_bytes=VMEM_LIMIT)


def _rms(x, g, eps=1e-6):
    return x * lax.rsqrt(jnp.mean(x * x, axis=-1, keepdims=True) + eps) * g


def _silu(x):
    return x * (1.0 / (1.0 + jnp.exp(-x)))


def _split3(x):
    p0 = x.astype(BF16)
    r = x - p0.astype(F32)
    p1 = r.astype(BF16)
    p2 = (r - p1.astype(F32)).astype(BF16)
    return p0, p1, p2


def _dot(a, b):
    return jnp.dot(a, b, preferred_element_type=F32)


def _dot_nt(a, b):
    return lax.dot_general(a, b, (((1,), (1,)), ((), ())), preferred_element_type=F32)


def _dot_tn(a, b):
    return lax.dot_general(a, b, (((0,), (0,)), ((), ())), preferred_element_type=F32)


def _dot_nt_f32(a, b):
    a0, a1, a2 = _split3(a)
    b0, b1, b2 = _split3(b)
    acc = _dot_nt(a0, b0)
    acc += _dot_nt(a0, b1) + _dot_nt(a1, b0)
    acc += _dot_nt(a1, b1) + _dot_nt(a0, b2) + _dot_nt(a2, b0)
    return acc


def _norm_up_body(xp_ref, xs_ref, g_ref, w_ref, xm_ref, z_ref, *, n_prompt_tiles, inner):
    i = pl.program_id(0)
    x = jnp.where(i < n_prompt_tiles, xp_ref[...], xs_ref[...])
    h = _rms(x, g_ref[...]).astype(BF16)
    u = _dot(h, w_ref[...])
    xm_ref[...] = u[:, :inner]
    z_ref[...] = u[:, inner:].astype(BF16)


def _two_src_specs(n_prompt_tiles, width):
    sp = pl.BlockSpec((ROW_TILE, width), lambda i: (jnp.minimum(i, n_prompt_tiles - 1), 0))
    ss = pl.BlockSpec((ROW_TILE, width), lambda i: (jnp.maximum(i - n_prompt_tiles, 0), 0))
    return sp, ss


def _norm_up(xp, xs, g, w):
    tp, d = xp.shape
    ts = xs.shape[0]
    tt = tp + ts
    n2 = w.shape[1]
    inner = n2 // 2
    npt = tp // ROW_TILE
    sp, ss = _two_src_specs(npt, d)
    return pl.pallas_call(
        functools.partial(_norm_up_body, n_prompt_tiles=npt, inner=inner),
        grid=(tt // ROW_TILE,),
        in_specs=[sp, ss,
                  pl.BlockSpec((1, d), lambda i: (0, 0)),
                  pl.BlockSpec((d, n2), lambda i: (0, 0))],
        out_specs=[pl.BlockSpec((ROW_TILE, inner), lambda i: (i, 0)),
                   pl.BlockSpec((ROW_TILE, inner), lambda i: (i, 0))],
        out_shape=[jax.ShapeDtypeStruct((tt, inner), F32),
                   jax.ShapeDtypeStruct((tt, inner), BF16)],
        compiler_params=_cparams("parallel"),
        name="norm_up",
    )(xp, xs, g, w)


def _conv_qkv_body(xm_ref, st_ref, cw_ref, cb_ref, wqk_ref, wv_ref, wg_ref, bg_ref,
                   q_ref, k_ref, v_ref, xc_ref, gc_ref, st_out_ref,
                   *, row_stride, k_scale):
    t = pl.program_id(1)
    halo = st_ref.shape[1]
    rows, inner = xm_ref.shape

    @pl.when(t == 0)
    def _():
        st_out_ref[0] = st_ref[0]

    x = xm_ref[...]
    xe = jnp.concatenate([st_out_ref[0], x], axis=0)
    acc = x * cw_ref[CONV_WIDTH - 1:CONV_WIDTH, :] + cb_ref[...]
    for j in range(CONV_WIDTH - 1):
        lo = halo - (CONV_WIDTH - 1 - j) * row_stride
        acc = acc + xe[lo:lo + rows, :] * cw_ref[j:j + 1, :]
    st_out_ref[0] = xe[rows:rows + halo, :]
    xc = _silu(acc)
    xc_b = xc.astype(BF16)
    xm_b = x.astype(BF16)
    xc_ref[...] = xc_b

    gacc = jnp.zeros((rows, LANES), F32)
    for c in range(inner // MXU_WIDTH):
        sl = slice(c * MXU_WIDTH, (c + 1) * MXU_WIDTH)
        qk = _dot(xc_b[:, sl], wqk_ref[c])
        vv = _dot(xm_b[:, sl], wv_ref[c])
        qb = qk[:, :MXU_WIDTH].astype(BF16)
        kb = qk[:, MXU_WIDTH:].astype(BF16)
        vb = vv.astype(BF16)
        q_ref[:, sl] = qb
        k_ref[:, sl] = (qk[:, MXU_WIDTH:] * k_scale).astype(BF16)
        v_ref[:, sl] = vb
        gacc += _dot(qb, wg_ref[0, sl, :]) + _dot(kb, wg_ref[1, sl, :]) + _dot(vb, wg_ref[2, sl, :])
    g = gacc + bg_ref[...]
    lane = lax.broadcasted_iota(I32, g.shape, 1)
    logsig = jnp.minimum(g, 0.0) - jnp.log1p(jnp.exp(-jnp.abs(g)))
    gc_ref[...] = jnp.where(lane < N_GATES // 2, g, logsig)


def _conv_qkv(xm, row_block_offset, state, cw, cb, wqk, wv, wg, bg, *, n_groups, tiles_per_group,
              row_stride, k_scale):
    inner = xm.shape[1]
    rows = n_groups * tiles_per_group * ROW_TILE
    halo = state.shape[1]
    nt = tiles_per_group
    row_spec = pl.BlockSpec((ROW_TILE, inner), lambda b, t: (b * nt + t, 0))
    full = lambda a: pl.BlockSpec(a.shape, lambda b, t: (0,) * a.ndim)
    return pl.pallas_call(
        functools.partial(_conv_qkv_body, row_stride=row_stride, k_scale=k_scale),
        grid=(n_groups, nt),
        in_specs=[pl.BlockSpec((ROW_TILE, inner), lambda b, t: (row_block_offset + b * nt + t, 0)),
                  pl.BlockSpec((1, halo, inner), lambda b, t: (b, 0, 0)),
                  full(cw), full(cb), full(wqk), full(wv), full(wg), full(bg)],
        out_specs=[row_spec, row_spec, row_spec, row_spec,
                   pl.BlockSpec((ROW_TILE, LANES), lambda b, t: (b * nt + t, 0)),
                   pl.BlockSpec((1, halo, inner), lambda b, t: (b, 0, 0))],
        out_shape=[jax.ShapeDtypeStruct((rows, inner), BF16)] * 4
        + [jax.ShapeDtypeStruct((rows, LANES), F32),
           jax.ShapeDtypeStruct((n_groups, halo, inner), F32)],
        compiler_params=_cparams("parallel", "arbitrary"),
        name="conv_qkv",
    )(xm, state, cw, cb, wqk, wv, wg, bg)


def _cumsum_rows(x):
    n = x.shape[0]
    tri = (lax.broadcasted_iota(I32, (n, n), 1) <= lax.broadcasted_iota(I32, (n, n), 0)).astype(BF16)
    p0, p1, p2 = _split3(x)
    return _dot(tri, p0) + _dot(tri, p1) + _dot(tri, p2)


def _cumsum_lanes(x):
    n = x.shape[1]
    tri = (lax.broadcasted_iota(I32, (n, n), 0) <= lax.broadcasted_iota(I32, (n, n), 1)).astype(BF16)
    p0, p1, p2 = _split3(x)
    return _dot(p0, tri) + _dot(p1, tri) + _dot(p2, tri)


def _mlstm_chunk(q, k, v, ig_col, bcum_col, ig_row, bcum_row, c_in, c_out, n_prev, m_prev):
    lq, lk = q.shape[0], k.shape[0]
    ri = lax.broadcasted_iota(I32, (lq, lk), 0)
    ci = lax.broadcasted_iota(I32, (lq, lk), 1)
    bcum_q = bcum_col[:lq]
    log_d = jnp.where(ci <= ri, bcum_q - bcum_row + ig_row, -jnp.inf)
    m_inter = bcum_q + m_prev
    m_t = jnp.maximum(m_inter, jnp.max(log_d, axis=-1, keepdims=True))
    s = _dot_nt(q, k) * jnp.exp(log_d - m_t)
    inter = jnp.exp(m_inter - m_t)
    m_new = m_t[lq - 1:lq, :]
    b_last = bcum_col[lk - 1:lk, :]
    w_col = jnp.exp(b_last - bcum_col + ig_col - m_new)
    decay = jnp.exp(b_last + m_prev - m_new)
    wk = w_col * k.astype(F32)
    c_prev = c_in[...]
    num = _dot(s.astype(BF16), v) + inter * _dot(q, c_prev.astype(BF16))
    c_out[...] = decay * c_prev + _dot_tn(wk.astype(BF16), v)
    den = jnp.sum(s, axis=-1, keepdims=True) + inter * jnp.sum(q.astype(F32) * n_prev, axis=-1, keepdims=True)
    h = num * (1.0 / jnp.maximum(jnp.abs(den), jnp.exp(-m_t)))
    n_new = decay * n_prev + jnp.sum(wk, axis=0, keepdims=True)
    return h, n_new, m_new


def _head_norm(h, w, eps=1e-5):
    mu = jnp.mean(h, axis=-1, keepdims=True)
    var = jnp.mean(jnp.square(h - mu), axis=-1, keepdims=True)
    return (h - mu) * lax.rsqrt(var + eps) * w


def _select_lane(x, idx):
    lane = lax.broadcasted_iota(I32, x.shape, 1)
    return jnp.sum(jnp.where(lane == idx, x, 0.0), axis=1, keepdims=True)


def _select_row(x, idx):
    row = lax.broadcasted_iota(I32, x.shape, 0)
    return jnp.sum(jnp.where(row == idx, x, 0.0), axis=0, keepdims=True)


def _cell_prompt_body(q_ref, k_ref, v_ref, gc_ref, gr_ref, hw_ref,
                      h_ref, c_ref, n_ref, m_ref):
    hd = pl.program_id(1)
    c = pl.program_id(2)

    @pl.when(c == 0)
    def _():
        c_ref[...] = jnp.zeros(c_ref.shape, F32)
        n_ref[...] = jnp.zeros(n_ref.shape, F32)
        m_ref[...] = jnp.full(m_ref.shape, STAB_INIT, F32)

    gc = gc_ref[...]
    gr = gr_ref[...]
    bc = _cumsum_rows(gc)
    br = _cumsum_lanes(gr)
    half = N_GATES // 2
    h, n_new, m_new = _mlstm_chunk(
        q_ref[...], k_ref[...], v_ref[...],
        _select_lane(gc, hd), _select_lane(bc, hd + half),
        _select_row(gr, hd), _select_row(br, hd + half),
        c_ref.at[0, 0], c_ref.at[0, 0], n_ref[0, 0], m_ref[0, 0, :, 0:1])
    h_ref[...] = _head_norm(h, hw_ref[0]).astype(BF16)
    n_ref[0, 0] = n_new
    m_ref[0, 0] = jnp.broadcast_to(m_new, m_ref.shape[2:])


def _cell_prompt(q, k, v, gc, gr, hn_w, *, batch, heads):
    tp, inner = q.shape
    dh = inner // heads
    seq = tp // batch
    ln = MLSTM_CHUNK_PROMPT
    nc = seq // ln
    qspec = pl.BlockSpec((ln, dh), lambda b, h, c: (b * nc + c, h))
    return pl.pallas_call(
        _cell_prompt_body,
        grid=(batch, heads, nc),
        in_specs=[qspec, qspec, qspec,
                  pl.BlockSpec((ln, LANES), lambda b, h, c: (b * nc + c, 0)),
                  pl.BlockSpec((SUBLANES, ln), lambda b, h, c: (0, b * nc + c)),
                  pl.BlockSpec((1, 1, dh), lambda b, h, c: (h, 0, 0))],
        out_specs=[qspec,
                   pl.BlockSpec((1, 1, dh, dh), lambda b, h, c: (b, h, 0, 0)),
                   pl.BlockSpec((1, 1, 1, dh), lambda b, h, c: (b, h, 0, 0)),
                   pl.BlockSpec((1, 1, 1, LANES), lambda b, h, c: (b, h, 0, 0))],
        out_shape=[jax.ShapeDtypeStruct((tp, inner), BF16),
                   jax.ShapeDtypeStruct((batch, heads, dh, dh), F32),
                   jax.ShapeDtypeStruct((batch, heads, 1, dh), F32),
                   jax.ShapeDtypeStruct((batch, heads, 1, LANES), F32)],
        compiler_params=_cparams("parallel", "parallel", "arbitrary"),
        name="mlstm_cell_prompt",
    )(q, k, v, gc, gr, hn_w)


def _cell_sample_body(q_ref, k_ref, v_ref, gc_ref, gr_ref, hw_ref, c0_ref, n0_ref, m0_ref,
                      h_ref, c_ref, n_ref, m_ref, *, heads, valid):
    ln = MLSTM_CHUNK_SAMPLE
    rows = q_ref.shape[1]
    dh = q_ref.shape[2] // heads
    half = N_GATES // 2
    gc = gc_ref[0]
    gr = gr_ref[0]
    row_c = lax.broadcasted_iota(I32, gc.shape, 0)
    lane_c = lax.broadcasted_iota(I32, gc.shape, 1)
    gc = jnp.where(row_c < valid, gc, jnp.where(lane_c < half, -jnp.inf, 0.0))
    row_r = lax.broadcasted_iota(I32, gr.shape, 0)
    lane_r = lax.broadcasted_iota(I32, gr.shape, 1)
    gr = jnp.where(lane_r < valid, gr, jnp.where(row_r < half, -jnp.inf, 0.0))
    bc = _cumsum_rows(jnp.where(lane_c < half, 0.0, gc))
    br = _cumsum_lanes(jnp.where(row_r < half, 0.0, gr))
    pad = jnp.zeros((ln - rows, dh), F32)

    def padded(ref, sl):
        return jnp.concatenate([ref[0, :, sl].astype(F32), pad], axis=0).astype(BF16)

    for hd in range(heads):
        sl = slice(hd * dh, (hd + 1) * dh)
        h, n_new, m_new = _mlstm_chunk(
            q_ref[0, :, sl], padded(k_ref, sl), padded(v_ref, sl),
            gc[:, hd:hd + 1], bc[:, hd + half:hd + half + 1],
            gr[hd:hd + 1, :], br[hd + half:hd + half + 1, :],
            c0_ref.at[0, hd], c_ref.at[0, hd], n0_ref[0, hd], m0_ref[0, hd, :, 0:1])
        h_ref[0, :, sl] = _head_norm(h, hw_ref[hd]).astype(BF16)
        n_ref[0, hd] = n_new
        m_ref[0, hd] = jnp.broadcast_to(m_new, m_ref.shape[2:])


def _cell_sample(q, k, v, gc, gr, hn_w, c0, n0, m0, *, heads, valid, group_batch):
    n_groups, rows, wide = q.shape
    inner = wide // group_batch
    dh = inner // heads
    nb = n_groups * group_batch
    ln = MLSTM_CHUNK_SAMPLE
    qspec = pl.BlockSpec((1, rows, inner), lambda b: (b // group_batch, 0, b % group_batch))
    cspec = pl.BlockSpec((1, heads, dh, dh), lambda b: (b, 0, 0, 0))
    nspec = pl.BlockSpec((1, heads, 1, dh), lambda b: (b, 0, 0, 0))
    mspec = pl.BlockSpec((1, heads, 1, LANES), lambda b: (b, 0, 0, 0))
    return pl.pallas_call(
        functools.partial(_cell_sample_body, heads=heads, valid=valid),
        grid=(nb,),
        in_specs=[qspec, qspec, qspec,
                  pl.BlockSpec((1, ln, LANES), lambda b: (b, 0, 0)),
                  pl.BlockSpec((1, SUBLANES, ln), lambda b: (b, 0, 0)),
                  pl.BlockSpec((heads, 1, dh), lambda b: (0, 0, 0)),
                  cspec, nspec, mspec],
        out_specs=[qspec, cspec, nspec, mspec],
        out_shape=[jax.ShapeDtypeStruct(q.shape, BF16),
                   jax.ShapeDtypeStruct((nb, heads, dh, dh), F32),
                   jax.ShapeDtypeStruct((nb, heads, 1, dh), F32),
                   jax.ShapeDtypeStruct((nb, heads, 1, LANES), F32)],
        compiler_params=_cparams("parallel"),
        name="mlstm_cell_sample",
    )(q, k, v, gc, gr, hn_w, c0, n0, m0)


def _gate_down_body(hp_ref, hs_ref, cp_ref, cs_ref, z_ref, skip_ref, w_ref, xp_ref, xs_ref, o_ref,
                    *, n_prompt_tiles):
    i = pl.program_id(0)
    is_p = i < n_prompt_tiles
    hn = jnp.where(is_p, hp_ref[...], hs_ref[...]).astype(F32)
    xc = jnp.where(is_p, cp_ref[...], cs_ref[...]).astype(F32)
    xres = jnp.where(is_p, xp_ref[...], xs_ref[...])
    a = (hn + skip_ref[...] * xc) * _silu(z_ref[...].astype(F32))
    o_ref[...] = xres + _dot(a.astype(BF16), w_ref[...])


def _gate_down(hn_p, hn_s, xc_p, xc_s, z, skip, w, xp, xs):
    tp, inner = hn_p.shape
    tt = z.shape[0]
    d = w.shape[1]
    npt = tp // ROW_TILE
    ip, is_ = _two_src_specs(npt, inner)
    dp, ds = _two_src_specs(npt, d)
    return pl.pallas_call(
        functools.partial(_gate_down_body, n_prompt_tiles=npt),
        grid=(tt // ROW_TILE,),
        in_specs=[ip, is_, ip, is_,
                  pl.BlockSpec((ROW_TILE, inner), lambda i: (i, 0)),
                  pl.BlockSpec((1, inner), lambda i: (0, 0)),
                  pl.BlockSpec((inner, d), lambda i: (0, 0)),
                  dp, ds],
        out_specs=pl.BlockSpec((ROW_TILE, d), lambda i: (i, 0)),
        out_shape=jax.ShapeDtypeStruct((tt, d), F32),
        compiler_params=_cparams("parallel"),
        name="gate_down",
    )(hn_p, hn_s, xc_p, xc_s, z, skip, w, xp, xs)


def _cmlp_in_body(x_ref, g_ref, w_ref, b_ref, lw_ref, lb_ref, u_ref, v_ref, *, half):
    h = _rms(x_ref[...], g_ref[...]).astype(BF16)
    y = _dot(h, w_ref[...]) + b_ref[...]
    zz = y * (0.5 * (1.0 + jnp.tanh(math.sqrt(2.0 / math.pi) * (y + 0.044715 * (y * y * y)))))
    u_ref[...] = zz[:, :half].astype(BF16)
    v = zz[:, half:]
    mu = jnp.mean(v, axis=-1, keepdims=True)
    var = jnp.mean(jnp.square(v - mu), axis=-1, keepdims=True)
    v_ref[...] = (v - mu) * lax.rsqrt(var + 1e-5) * lw_ref[...] + lb_ref[...]


def _cmlp_in(x, g, w, b, lw, lb):
    tt, d = x.shape
    n2 = w.shape[1]
    half = n2 // 2
    vec = lambda n: pl.BlockSpec((1, n), lambda i: (0, 0))
    return pl.pallas_call(
        functools.partial(_cmlp_in_body, half=half),
        grid=(tt // ROW_TILE,),
        in_specs=[pl.BlockSpec((ROW_TILE, d), lambda i: (i, 0)), vec(d),
                  pl.BlockSpec((d, n2), lambda i: (0, 0)), vec(n2), vec(half), vec(half)],
        out_specs=[pl.BlockSpec((ROW_TILE, half), lambda i: (i, 0))] * 2,
        out_shape=[jax.ShapeDtypeStruct((tt, half), BF16), jax.ShapeDtypeStruct((tt, half), F32)],
        compiler_params=_cparams("parallel"),
        name="cmlp_in",
    )(x, g, w, b, lw, lb)


def _cmlp_mix_out_body(u_ref, v_ref, mix_ref, bias_ref, w_ref, bo_ref, x_ref, o_ref, *, groups):
    half = u_ref.shape[1]
    gd = half // groups
    pieces = []
    for g in range(groups):
        sl = slice(g * gd, (g + 1) * gd)
        mixed = _dot(mix_ref[0, g], v_ref[:, sl].astype(BF16)) + bias_ref[0, :, g:g + 1]
        pieces.append((u_ref[:, sl].astype(F32) * mixed).astype(BF16))
    a = jnp.concatenate(pieces, axis=1)
    o_ref[...] = x_ref[...] + _dot(a, w_ref[...]) + bo_ref[...]


def _cmlp_mix_out(u, v, mix, bias, w, bo, x, *, n_prompt_tiles):
    tt, half = u.shape
    d = w.shape[1]
    groups = mix.shape[1]
    kind = lambda i: jnp.where(i < n_prompt_tiles, 0, 1)
    return pl.pallas_call(
        functools.partial(_cmlp_mix_out_body, groups=groups),
        grid=(tt // ROW_TILE,),
        in_specs=[pl.BlockSpec((ROW_TILE, half), lambda i: (i, 0)),
                  pl.BlockSpec((ROW_TILE, half), lambda i: (i, 0)),
                  pl.BlockSpec((1, groups, ROW_TILE, ROW_TILE), lambda i: (kind(i), 0, 0, 0)),
                  pl.BlockSpec((1, ROW_TILE, groups), lambda i: (kind(i), 0, 0)),
                  pl.BlockSpec((half, d), lambda i: (0, 0)),
                  pl.BlockSpec((1, d), lambda i: (0, 0)),
                  pl.BlockSpec((ROW_TILE, d), lambda i: (i, 0))],
        out_specs=pl.BlockSpec((ROW_TILE, d), lambda i: (i, 0)),
        out_shape=jax.ShapeDtypeStruct((tt, d), F32),
        compiler_params=_cparams("parallel"),
        name="cmlp_mix_out",
    )(u, v, mix, bias, w, bo, x)


GROUP_ROW_OFFSET = SUBLANES


def _first_argmax_rows(x, n):
    row = lax.broadcasted_iota(I32, x.shape, 0)
    mx = jnp.max(x, axis=0, keepdims=True)
    idx = jnp.min(jnp.where(x == mx, row, n), axis=0, keepdims=True)
    return idx, mx


def _router_body(x_ref, g_ref, wt_ref, bt_ref, route_ref, cnt_ref, *, n_groups, per_group):
    i = pl.program_id(0)
    n_exp = n_groups * per_group

    @pl.when(i == 0)
    def _():
        cnt_ref[...] = jnp.zeros(cnt_ref.shape, F32)

    h = _rms(x_ref[...], g_ref[...])
    logits = _dot_nt_f32(wt_ref[...], h) + bt_ref[:, 0:1]
    tokens = logits.shape[1]
    gl = logits[0:n_groups, :]
    gmax = jnp.max(gl, axis=0, keepdims=True)
    ge = jnp.exp(gl - gmax)
    gp = ge / jnp.sum(ge, axis=0, keepdims=True)
    g_idx, g_w = _first_argmax_rows(gp, n_groups)
    e_sel = jnp.zeros((per_group, tokens), F32)
    for g in range(n_groups):
        lo = GROUP_ROW_OFFSET + g * per_group
        e_sel = e_sel + jnp.where(g_idx == g, logits[lo:lo + per_group, :], 0.0)
    i1, m1 = _first_argmax_rows(e_sel, per_group)
    row = lax.broadcasted_iota(I32, e_sel.shape, 0)
    i2, m2 = _first_argmax_rows(jnp.where(row == i1, -jnp.inf, e_sel), per_group)
    t = jnp.exp(m2 - m1)
    w1 = (1.0 / (1.0 + t)) * g_w
    w2 = (t / (1.0 + t)) * g_w
    e1 = g_idx * per_group + i1
    e2 = g_idx * per_group + i2

    erow = lax.broadcasted_iota(I32, (n_exp, tokens), 0)
    oh1 = erow == e1
    oh2 = erow == e2
    onehot = jnp.where(oh1 | oh2, 1.0, 0.0)
    strict_upper = (lax.broadcasted_iota(I32, (tokens, tokens), 0)
                    < lax.broadcasted_iota(I32, (tokens, tokens), 1)).astype(BF16)
    before = _dot(onehot.astype(BF16), strict_upper) + cnt_ref[:, 0:1]
    r1 = jnp.sum(jnp.where(oh1, before, 0.0), axis=0, keepdims=True)
    r2 = jnp.sum(jnp.where(oh2, before, 0.0), axis=0, keepdims=True)
    cnt_ref[...] = cnt_ref[...] + jnp.sum(onehot, axis=1, keepdims=True)
    zero = jnp.zeros_like(w1)
    route_ref[...] = jnp.concatenate(
        [e1.astype(F32), e2.astype(F32), r1, r2, w1, w2, zero, zero], axis=0)


def _wide_tile(n_rows, max_tiles):
    n = n_rows // ROW_TILE
    return ROW_TILE * max(k for k in range(1, max_tiles + 1) if n % k == 0)


def _router(x, g, wt, bt, *, n_groups, per_group):
    tt, d = x.shape
    n_exp = n_groups * per_group
    rt = _wide_tile(tt, 2)
    return pl.pallas_call(
        functools.partial(_router_body, n_groups=n_groups, per_group=per_group),
        grid=(tt // rt,),
        in_specs=[pl.BlockSpec((rt, d), lambda i: (i, 0)),
                  pl.BlockSpec((1, d), lambda i: (0, 0)),
                  pl.BlockSpec(wt.shape, lambda i: (0, 0)),
                  pl.BlockSpec(bt.shape, lambda i: (0, 0))],
        out_specs=[pl.BlockSpec((SUBLANES, rt), lambda i: (0, i)),
                   pl.BlockSpec((n_exp, LANES), lambda i: (0, 0))],
        out_shape=[jax.ShapeDtypeStruct((SUBLANES, tt), F32),
                   jax.ShapeDtypeStruct((n_exp, LANES), F32)],
        compiler_params=_cparams("arbitrary"),
        name="moe_router",
    )(x, g, wt, bt)


def _positions_body(route_ref, off_ref, pos_ref):
    r = route_ref[...]
    n_exp = off_ref.shape[0]
    tokens = r.shape[1]
    erow = lax.broadcasted_iota(I32, (n_exp, tokens), 0)
    off = off_ref[:, 0:1]
    out = []
    for k in range(2):
        e = r[k:k + 1, :].astype(I32)
        base = jnp.sum(jnp.where(erow == e, off, 0.0), axis=0, keepdims=True)
        out.append((base + r[2 + k:3 + k, :]).astype(I32))
    pos_ref[...] = jnp.concatenate(out, axis=0)


def _positions(route, off):
    tt = route.shape[1]
    pt = _wide_tile(tt, 16)
    return pl.pallas_call(
        _positions_body,
        grid=(tt // pt,),
        in_specs=[pl.BlockSpec((SUBLANES, pt), lambda i: (0, i)),
                  pl.BlockSpec(off.shape, lambda i: (0, 0))],
        out_specs=pl.BlockSpec((2, pt), lambda i: (0, i)),
        out_shape=jax.ShapeDtypeStruct((2, tt), I32),
        compiler_params=_cparams("parallel"),
        name="moe_positions",
    )(route, off)


def _row_copy(src, src_row, dst, dst_row, sem):
    return pltpu.make_async_copy(src.at[pl.ds(src_row, 1)], dst.at[pl.ds(dst_row, 1)], sem)


def _dispatch_body(pos_ref, x_ref, xs_hbm, sem, *, n_tokens):
    i = pl.program_id(0)
    base = i * ROW_TILE

    def copies(r):
        return [_row_copy(x_ref, r, xs_hbm, pos_ref[k * n_tokens + base + r], sem) for k in range(2)]

    def start(r, carry):
        for cp in copies(r):
            cp.start()
        return carry

    def wait(r, carry):
        for cp in copies(r):
            cp.wait()
        return carry

    lax.fori_loop(0, ROW_TILE, start, 0, unroll=8)
    lax.fori_loop(0, ROW_TILE, wait, 0, unroll=8)


def _dispatch(pos_flat, x):
    tt, d = x.shape
    return pl.pallas_call(
        functools.partial(_dispatch_body, n_tokens=tt),
        grid_spec=pltpu.PrefetchScalarGridSpec(
            num_scalar_prefetch=1,
            grid=(tt // ROW_TILE,),
            in_specs=[pl.BlockSpec((ROW_TILE, d), lambda i, p: (i, 0))],
            out_specs=pl.BlockSpec(memory_space=pl.ANY),
            scratch_shapes=[pltpu.SemaphoreType.DMA]),
        out_shape=jax.ShapeDtypeStruct((2 * tt, d), F32),
        compiler_params=_cparams("arbitrary"),
        name="moe_dispatch",
    )(pos_flat, x)


def _experts_body(tile_ref, exp_ref, lo_ref, hi_ref, nwork_ref,
                  xs_ref, g_ref, wg_ref, wu_ref, wd_ref, o_ref, wg_b, wu_b, wd_b):
    w = pl.program_id(0)
    prev = jnp.maximum(w - 1, 0)
    first = jnp.logical_or(w == 0, tile_ref[w] != tile_ref[prev])
    new_expert = jnp.logical_or(w == 0, exp_ref[w] != exp_ref[prev])

    @pl.when(w < nwork_ref[0])
    def _():
        @pl.when(new_expert)
        def _():
            wg_b[...] = wg_ref[0].astype(BF16)
            wu_b[...] = wu_ref[0].astype(BF16)
            wd_b[...] = wd_ref[0].astype(BF16)

        xb = _rms(xs_ref[...], g_ref[...]).astype(BF16)
        a = _dot(xb, wg_b[...])
        b = _dot(xb, wu_b[...])
        hg = (_silu(a) * b).astype(BF16)
        y = _dot(hg, wd_b[...])
        row = lax.broadcasted_iota(I32, y.shape, 0)
        y = jnp.where((row >= lo_ref[w]) & (row < hi_ref[w]), y, 0.0)

        @pl.when(first)
        def _():
            o_ref[...] = y

        @pl.when(jnp.logical_not(first))
        def _():
            o_ref[...] += y


def _experts(meta, xs, g, wg, wu, wd):
    rows, d = xs.shape
    n_work = meta[0].shape[0]
    f = wg.shape[2]
    return pl.pallas_call(
        _experts_body,
        grid_spec=pltpu.PrefetchScalarGridSpec(
            num_scalar_prefetch=5,
            grid=(n_work,),
            in_specs=[pl.BlockSpec((EXPERT_TILE, d), lambda w, tl, ex, lo, hi, nw: (tl[w], 0)),
                      pl.BlockSpec((1, d), lambda w, tl, ex, lo, hi, nw: (0, 0)),
                      pl.BlockSpec((1, d, f), lambda w, tl, ex, lo, hi, nw: (ex[w], 0, 0)),
                      pl.BlockSpec((1, d, f), lambda w, tl, ex, lo, hi, nw: (ex[w], 0, 0)),
                      pl.BlockSpec((1, f, d), lambda w, tl, ex, lo, hi, nw: (ex[w], 0, 0))],
            out_specs=pl.BlockSpec((EXPERT_TILE, d), lambda w, tl, ex, lo, hi, nw: (tl[w], 0)),
            scratch_shapes=[pltpu.VMEM((d, f), BF16), pltpu.VMEM((d, f), BF16), pltpu.VMEM((f, d), BF16)]),
        out_shape=jax.ShapeDtypeStruct((rows, d), F32),
        compiler_params=_cparams("arbitrary"),
        name="moe_experts",
    )(*meta, xs, g, wg, wu, wd)


def _combine_rows(pos_ref, x_ref, route_ref, ys_hbm, buf, sem, n_tokens):
    base = pl.program_id(0) * ROW_TILE

    def copies(r):
        t = base + r
        return [_row_copy(ys_hbm, pos_ref[k * n_tokens + t], buf.at[k], r, sem) for k in range(2)]

    def start(r, carry):
        for cp in copies(r):
            cp.start()
        return carry

    def wait(r, carry):
        for cp in copies(r):
            cp.wait()
        return carry

    lax.fori_loop(0, ROW_TILE, start, 0, unroll=8)
    r = route_ref[...]
    rt = jnp.concatenate([r, jnp.zeros((LANES - SUBLANES, r.shape[1]), F32)], axis=0).T
    lax.fori_loop(0, ROW_TILE, wait, 0, unroll=8)
    return x_ref[...] + rt[:, 4:5] * buf[0] + rt[:, 5:6] * buf[1]


def _combine_body(pos_ref, x_ref, route_ref, ys_hbm, o_ref, buf, sem, *, n_tokens):
    o_ref[...] = _combine_rows(pos_ref, x_ref, route_ref, ys_hbm, buf, sem, n_tokens)


def _combine_norm_body(pos_ref, x_ref, route_ref, g_ref, ys_hbm, op_ref, os_ref, buf, sem,
                       *, n_tokens, n_prompt_tiles):
    y = _rms(_combine_rows(pos_ref, x_ref, route_ref, ys_hbm, buf, sem, n_tokens), g_ref[...])
    is_prompt = pl.program_id(0) < n_prompt_tiles

    @pl.when(is_prompt)
    def _():
        op_ref[...] = y

    @pl.when(jnp.logical_not(is_prompt))
    def _():
        os_ref[...] = y


def _combine(pos_flat, x, route, ys, final_g=None, n_prompt_tiles=None):
    tt, d = x.shape
    in_specs = [pl.BlockSpec((ROW_TILE, d), lambda i, p: (i, 0)),
                pl.BlockSpec((SUBLANES, ROW_TILE), lambda i, p: (0, i))]
    args = [pos_flat, x, route]
    if final_g is None:
        body = functools.partial(_combine_body, n_tokens=tt)
        out_specs = pl.BlockSpec((ROW_TILE, d), lambda i, p: (i, 0))
        out_shape = jax.ShapeDtypeStruct((tt, d), F32)
    else:
        npt = n_prompt_tiles
        body = functools.partial(_combine_norm_body, n_tokens=tt, n_prompt_tiles=npt)
        in_specs.append(pl.BlockSpec((1, d), lambda i, p: (0, 0)))
        args.append(final_g)
        out_specs = [pl.BlockSpec((ROW_TILE, d), lambda i, p: (jnp.minimum(i, npt - 1), 0)),
                     pl.BlockSpec((ROW_TILE, d), lambda i, p: (jnp.maximum(i - npt, 0), 0))]
        out_shape = [jax.ShapeDtypeStruct((npt * ROW_TILE, d), F32),
                     jax.ShapeDtypeStruct((tt - npt * ROW_TILE, d), F32)]
    in_specs.append(pl.BlockSpec(memory_space=pl.ANY))
    args.append(ys)
    return pl.pallas_call(
        body,
        grid_spec=pltpu.PrefetchScalarGridSpec(
            num_scalar_prefetch=1,
            grid=(tt // ROW_TILE,),
            in_specs=in_specs,
            out_specs=out_specs,
            scratch_shapes=[pltpu.VMEM((2, ROW_TILE, d), F32), pltpu.SemaphoreType.DMA]),
        out_shape=out_shape,
        compiler_params=_cparams("arbitrary"),
        name="moe_combine",
    )(*args)


def _work_items(counts, n_rows):
    n_exp = counts.shape[0]
    n_tiles = n_rows // EXPERT_TILE
    n_work = n_tiles + n_exp - 1
    ids = jnp.arange(n_exp, dtype=I32)
    lower = ids[None, :] <= ids[:, None]

    def cumsum(v):
        return jnp.sum(jnp.where(lower, v[None, :], 0), axis=1)

    end = cumsum(counts)
    off = end - counts
    first_tile = off // EXPERT_TILE
    last_tile = (end - 1) // EXPERT_TILE
    per_exp = jnp.where(counts > 0, last_tile - first_tile + 1, 0)
    wend = cumsum(per_exp)
    wstart = wend - per_exp
    total = jnp.sum(per_exp)
    w = jnp.arange(n_work, dtype=I32)
    e = jnp.minimum(jnp.sum((wend[None, :] <= w[:, None]).astype(I32), axis=1), n_exp - 1)
    sel = e[:, None] == ids[None, :]

    def pick(v):
        return jnp.sum(jnp.where(sel, v[None, :], 0), axis=1)

    tile = pick(first_tile) + (w - pick(wstart))
    lo = jnp.maximum(pick(off), tile * EXPERT_TILE) - tile * EXPERT_TILE
    hi = jnp.minimum(pick(end), (tile + 1) * EXPERT_TILE) - tile * EXPERT_TILE
    live = w < total
    tile = jnp.where(live, tile, n_tiles - 1)
    lo = jnp.where(live, lo, 0)
    hi = jnp.where(live, hi, 0)
    return off, (tile.astype(I32), e.astype(I32), lo.astype(I32), hi.astype(I32),
                 total.reshape(1).astype(I32))


def _moe(x, g, layer, w_rg, b_rg, w_re, b_re, w_gate, w_up, w_down, final_g=None, n_prompt_tiles=None):
    tt, d = x.shape
    n_groups, per_group = w_re.shape[0], w_re.shape[2]
    n_exp = n_groups * per_group
    f = w_gate.shape[-1]
    pad_rows = jnp.zeros((GROUP_ROW_OFFSET - n_groups, d), F32)
    wt = jnp.concatenate([w_rg.T, pad_rows, jnp.transpose(w_re, (0, 2, 1)).reshape(n_exp, d)], axis=0)
    bt = jnp.concatenate([b_rg, jnp.zeros((GROUP_ROW_OFFSET - n_groups,), F32), b_re.reshape(n_exp)])
    bt = jnp.broadcast_to(bt[:, None], (wt.shape[0], LANES))
    g2 = g.reshape(1, d)

    route, cnt = _router(x, g2, wt, bt, n_groups=n_groups, per_group=per_group)
    counts = cnt[:, 0].astype(I32)
    off, (tile, exp, lo, hi, total) = _work_items(counts, 2 * tt)
    off_b = jnp.broadcast_to(off.astype(F32)[:, None], (n_exp, LANES))
    pos = _positions(route, off_b).reshape(2 * tt)
    xs = _dispatch(pos, x)
    ys = _experts((tile, exp + layer * n_exp, lo, hi, total), xs, g2,
                  w_gate.reshape(-1, d, f), w_up.reshape(-1, d, f), w_down.reshape(-1, f, d))
    return _combine(pos, x, route, ys, final_g, n_prompt_tiles)


def _block_diag_tiles(w, tile):
    nb, c, _ = w.shape
    rows = w.reshape(nb * c // tile, tile, c)
    blk = jnp.arange(tile, dtype=I32) // c
    return jnp.where(blk[:, None] == blk[None, :], jnp.tile(rows, (1, 1, tile // c)), 0.0)


def _sample_to_rows(x, group_batch):
    db, s, w = x.shape
    return x.reshape(db // group_batch, group_batch, s, w).transpose(0, 2, 1, 3).reshape(db * s, w)


def _rows_to_sample(x, group_batch, s):
    rows, w = x.shape
    db = rows // s
    return x.reshape(db // group_batch, s, group_batch, w).transpose(0, 2, 1, 3).reshape(db, s, w)


def kernel(x_prompt, x_sample, state_mlstm_C, state_mlstm_n, state_mlstm_m, state_mlstm_conv,
           norm_mix, norm_ffn, norm_final,
           mlstm_w_up, mlstm_conv_w, mlstm_conv_b, mlstm_w_q, mlstm_w_k, mlstm_w_v,
           mlstm_w_ig, mlstm_b_ig, mlstm_w_fg, mlstm_b_fg, mlstm_skip, mlstm_hn_w, mlstm_w_down,
           cmlp_w_in, cmlp_b_in, cmlp_ln_w, cmlp_ln_b, cmlp_w_s, cmlp_b_s, cmlp_w_out, cmlp_b_out,
           moe_w_rg, moe_b_rg, moe_w_re, moe_b_re, moe_w_gate, moe_w_up, moe_w_down):
    batch, seq, d = x_prompt.shape
    dec_batch, dec_seq, _ = x_sample.shape
    heads = state_mlstm_C.shape[2]
    dh = state_mlstm_C.shape[3]
    inner = heads * dh
    tp, ts = batch * seq, dec_batch * dec_seq
    npt = tp // ROW_TILE
    gb = ROW_TILE // dec_seq
    n_sgroups = dec_batch // gb
    half_g = N_GATES // 2
    assert heads == half_g and tp % ROW_TILE == 0 and ts % ROW_TILE == 0
    assert ROW_TILE % dec_seq == 0 and dec_batch % gb == 0 and seq % MLSTM_CHUNK_PROMPT == 0
    assert seq % ROW_TILE == 0 and ROW_TILE % CMLP_CHUNK == 0 and dec_seq <= SUBLANES

    xp = x_prompt.reshape(tp, d)
    xs = _sample_to_rows(x_sample, gb)

    xm, z = _norm_up(xp, xs, norm_mix[0].reshape(1, d), mlstm_w_up[0].astype(BF16))

    wq = _block_diag_tiles(mlstm_w_q[0], MXU_WIDTH)
    wk = _block_diag_tiles(mlstm_w_k[0], MXU_WIDTH)
    wqk = jnp.concatenate([wq, wk], axis=2).astype(BF16)
    wv = _block_diag_tiles(mlstm_w_v[0], MXU_WIDTH).astype(BF16)
    wg = jnp.concatenate([mlstm_w_ig[0], mlstm_w_fg[0]], axis=1)
    wg = jnp.pad(wg, ((0, 0), (0, LANES - N_GATES))).reshape(3, inner, LANES).astype(BF16)
    bg = jnp.pad(jnp.concatenate([mlstm_b_ig[0], mlstm_b_fg[0]]), (0, LANES - N_GATES)).reshape(1, LANES)
    cw, cb = mlstm_conv_w[0], mlstm_conv_b[0].reshape(1, inner)
    k_scale = float(dh) ** -0.5

    conv0_p = jnp.zeros((batch, SUBLANES, inner), F32)
    q_p, k_p, v_p, xc_p, gc_p, conv_p = _conv_qkv(
        xm, 0, conv0_p, cw, cb, wqk, wv, wg, bg,
        n_groups=batch, tiles_per_group=seq // ROW_TILE, row_stride=1, k_scale=k_scale)
    halo_s = (CONV_WIDTH - 1) * gb
    conv0_s = state_mlstm_conv[0].reshape(n_sgroups, gb, CONV_WIDTH - 1, inner)
    conv0_s = conv0_s.transpose(0, 2, 1, 3).reshape(n_sgroups, halo_s, inner)
    q_s, k_s, v_s, xc_s, gc_s, conv_s = _conv_qkv(
        xm, npt, conv0_s, cw, cb, wqk, wv, wg, bg,
        n_groups=n_sgroups, tiles_per_group=1, row_stride=gb, k_scale=k_scale)

    hn_w = mlstm_hn_w[0].reshape(heads, 1, dh)
    gr_p = jnp.transpose(gc_p[:, :SUBLANES])
    hn_p, c_p, n_p, m_p = _cell_prompt(q_p, k_p, v_p, gc_p, gr_p, hn_w, batch=batch, heads=heads)

    def pad_steps(a):
        a = a.reshape(n_sgroups, dec_seq, gb * inner)
        return jnp.pad(a, ((0, 0), (0, SUBLANES - dec_seq), (0, 0)))

    ln_s = MLSTM_CHUNK_SAMPLE
    gcs = gc_s.reshape(n_sgroups, dec_seq, gb, LANES).transpose(0, 2, 1, 3).reshape(dec_batch, dec_seq, LANES)
    gcs = jnp.pad(gcs, ((0, 0), (0, ln_s - dec_seq), (0, 0)))
    grs = jnp.transpose(gcs[:, :, :SUBLANES], (0, 2, 1))
    m0 = jnp.broadcast_to(state_mlstm_m[0][:, :, None, None], (dec_batch, heads, 1, LANES))
    hn_s, c_s, n_s, m_s = _cell_sample(
        pad_steps(q_s), pad_steps(k_s), pad_steps(v_s), gcs, grs, hn_w,
        state_mlstm_C[0], state_mlstm_n[0].reshape(dec_batch, heads, 1, dh), m0,
        heads=heads, valid=dec_seq, group_batch=gb)
    hn_s = hn_s[:, :dec_seq].reshape(ts, inner)

    x1 = _gate_down(hn_p, hn_s, xc_p, xc_s, z, mlstm_skip[0].reshape(1, inner),
                    mlstm_w_down[0].astype(BF16), xp, xs)
    x2 = _moe(x1, norm_ffn[0], 0, moe_w_rg[0], moe_b_rg[0], moe_w_re[0], moe_b_re[0],
              moe_w_gate, moe_w_up, moe_w_down)

    half = cmlp_w_in.shape[2] // 2
    groups = cmlp_w_s.shape[1]
    u, vn = _cmlp_in(x2, norm_mix[1].reshape(1, d), cmlp_w_in[0].astype(BF16),
                     cmlp_b_in[0].reshape(1, 2 * half), cmlp_ln_w[0].reshape(1, half),
                     cmlp_ln_b[0].reshape(1, half))
    causal = jnp.tril(jnp.ones((CMLP_CHUNK, CMLP_CHUNK), dtype=bool))
    ws = jnp.where(causal, cmlp_w_s[0], 0.0)
    reps = ROW_TILE // CMLP_CHUNK
    rid = jnp.arange(ROW_TILE, dtype=I32)
    same_chunk = (rid[:, None] // CMLP_CHUNK) == (rid[None, :] // CMLP_CHUNK)
    same_seq = (rid[:, None] % gb) == (rid[None, :] % gb)
    mix_p = jnp.where(same_chunk, jnp.tile(ws, (1, reps, reps)), 0.0)
    ws_s = ws[:, :dec_seq, :dec_seq]
    mix_s = jnp.where(same_seq, jnp.repeat(jnp.repeat(ws_s, gb, axis=1), gb, axis=2), 0.0)
    mix = jnp.stack([mix_p, mix_s]).astype(BF16)
    bias_t = jnp.transpose(cmlp_b_s[0])
    bias = jnp.stack([jnp.tile(bias_t, (reps, 1)), jnp.repeat(bias_t[:dec_seq], gb, axis=0)])
    x3 = _cmlp_mix_out(u, vn, mix, bias, cmlp_w_out[0].astype(BF16), cmlp_b_out[0].reshape(1, d), x2,
                       n_prompt_tiles=npt)
    y_p, y_s = _moe(x3, norm_ffn[1], 1, moe_w_rg[1], moe_b_rg[1], moe_w_re[1], moe_b_re[1],
                    moe_w_gate, moe_w_up, moe_w_down,
                    final_g=norm_final.reshape(1, d), n_prompt_tiles=npt)
    y_prompt = y_p.reshape(batch, seq, d)
    y_sample = _rows_to_sample(y_s, gb, dec_seq)

    conv_prompt = conv_p[:, SUBLANES - (CONV_WIDTH - 1):, :][None]
    conv_sample = conv_s.reshape(n_sgroups, CONV_WIDTH - 1, gb, inner).transpose(0, 2, 1, 3)
    conv_sample = conv_sample.reshape(dec_batch, CONV_WIDTH - 1, inner)[None]
    v_sample = _rows_to_sample(vn[tp:], gb, dec_seq)[None]
    return (y_prompt, y_sample,
            c_p[None], n_p[:, :, 0, :][None], m_p[:, :, 0, 0][None], conv_prompt,
            c_s[None], n_s[:, :, 0, :][None], m_s[:, :, 0, 0][None], conv_sample,
            v_sample)
```

```python
import functools
import math

import jax
import jax.numpy as jnp
from jax import lax
from jax.experimental import pallas as pl
from jax.experimental.pallas import tpu as pltpu

F32 = jnp.float32
BF16 = jnp.bfloat16
I32 = jnp.int32

LANES = 128
SUBLANES = 8
ROW_TILE = 256
EXPERT_TILE = 512
MXU_WIDTH = 256
VMEM_LIMIT = 56 * 1024 * 1024
CONV_WIDTH = 4
QKV_BLOCK = 4
MLSTM_CHUNK_PROMPT = 256
MLSTM_CHUNK_SAMPLE = 128
CMLP_CHUNK = 128
STAB_INIT = -1e30
N_GATES = 8


def _cparams(*sem):
    return pltpu.CompilerParams(dimension_semantics=sem, vmem_limit_bytes=VMEM_LIMIT)


def _rms(x, g, eps=1e-6):
    return x * lax.rsqrt(jnp.mean(x * x, axis=-1, keepdims=True) + eps) * g


def _silu(x):
    return x * (1.0 / (1.0 + jnp.exp(-x)))


def _split3(x):
    p0 = x.astype(BF16)
    r = x - p0.astype(F32)
    p1 = r.astype(BF16)
    p2 = (r - p1.astype(F32)).astype(BF16)
    return p0, p1, p2


def _dot(a, b):
    return jnp.dot(a, b, preferred_element_type=F32)


def _dot_nt(a, b):
    return lax.dot_general(a, b, (((1,), (1,)), ((), ())), preferred_element_type=F32)


def _dot_tn(a, b):
    return lax.dot_general(a, b, (((0,), (0,)), ((), ())), preferred_element_type=F32)


def _dot_nt_f32(a, b):
    a0, a1, a2 = _split3(a)
    b0, b1, b2 = _split3(b)
    acc = _dot_nt(a0, b0)
    acc += _dot_nt(a0, b1) + _dot_nt(a1, b0)
    acc += _dot_nt(a1, b1) + _dot_nt(a0, b2) + _dot_nt(a2, b0)
    return acc


def _norm_up_body(xp_ref, xs_ref, g_ref, w_ref, xm_ref, z_ref, *, n_prompt_tiles, inner):
    i = pl.program_id(0)
    x = jnp.where(i < n_prompt_tiles, xp_ref[...], xs_ref[...])
    h = _rms(x, g_ref[...]).astype(BF16)
    u = _dot(h, w_ref[...])
    xm_ref[...] = u[:, :inner]
    z_ref[...] = u[:, inner:].astype(BF16)


def _two_src_specs(n_prompt_tiles, width):
    sp = pl.BlockSpec((ROW_TILE, width), lambda i: (jnp.minimum(i, n_prompt_tiles - 1), 0))
    ss = pl.BlockSpec((ROW_TILE, width), lambda i: (jnp.maximum(i - n_prompt_tiles, 0), 0))
    return sp, ss


def _norm_up(xp, xs, g, w):
    tp, d = xp.shape
    ts = xs.shape[0]
    tt = tp + ts
    n2 = w.shape[1]
    inner = n2 // 2
    npt = tp // ROW_TILE
    sp, ss = _two_src_specs(npt, d)
    return pl.pallas_call(
        functools.partial(_norm_up_body, n_prompt_tiles=npt, inner=inner),
        grid=(tt // ROW_TILE,),
        in_specs=[sp, ss,
                  pl.BlockSpec((1, d), lambda i: (0, 0)),
                  pl.BlockSpec((d, n2), lambda i: (0, 0))],
        out_specs=[pl.BlockSpec((ROW_TILE, inner), lambda i: (i, 0)),
                   pl.BlockSpec((ROW_TILE, inner), lambda i: (i, 0))],
        out_shape=[jax.ShapeDtypeStruct((tt, inner), F32),
                   jax.ShapeDtypeStruct((tt, inner), BF16)],
        compiler_params=_cparams("parallel"),
        name="norm_up",
    )(xp, xs, g, w)


def _conv_qkv_body(xm_ref, st_ref, cw_ref, cb_ref, wqk_ref, wv_ref, wg_ref, bg_ref,
                   q_ref, k_ref, v_ref, xc_ref, gc_ref, bc_ref, st_out_ref,
                   *, row_stride, k_scale):
    t = pl.program_id(1)
    halo = st_ref.shape[1]
    rows, inner = xm_ref.shape

    @pl.when(t == 0)
    def _():
        st_out_ref[0] = st_ref[0]

    x = xm_ref[...]
    xe = jnp.concatenate([st_out_ref[0], x], axis=0)
    acc = x * cw_ref[CONV_WIDTH - 1:CONV_WIDTH, :] + cb_ref[...]
    for j in range(CONV_WIDTH - 1):
        lo = halo - (CONV_WIDTH - 1 - j) * row_stride
        acc = acc + xe[lo:lo + rows, :] * cw_ref[j:j + 1, :]
    st_out_ref[0] = xe[rows:rows + halo, :]
    xc = _silu(acc)
    xc_b = xc.astype(BF16)
    xm_b = x.astype(BF16)
    xc_ref[...] = xc_b

    gacc = jnp.zeros((rows, LANES), F32)
    for c in range(inner // MXU_WIDTH):
        sl = slice(c * MXU_WIDTH, (c + 1) * MXU_WIDTH)
        qk = _dot(xc_b[:, sl], wqk_ref[c])
        vv = _dot(xm_b[:, sl], wv_ref[c])
        qb = qk[:, :MXU_WIDTH].astype(BF16)
        kb = qk[:, MXU_WIDTH:].astype(BF16)
        vb = vv.astype(BF16)
        q_ref[:, sl] = qb
        k_ref[:, sl] = (qk[:, MXU_WIDTH:] * k_scale).astype(BF16)
        v_ref[:, sl] = vb
        gacc += _dot(qb, wg_ref[0, sl, :]) + _dot(kb, wg_ref[1, sl, :]) + _dot(vb, wg_ref[2, sl, :])
    g = gacc + bg_ref[...]
    lane = lax.broadcasted_iota(I32, g.shape, 1)
    logsig = jnp.minimum(g, 0.0) - jnp.log1p(jnp.exp(-jnp.abs(g)))
    gates = jnp.where(lane < N_GATES // 2, g, logsig)
    gc_ref[...] = gates
    bc_ref[...] = jnp.where(lane < N_GATES // 2, gates, _cumsum_rows(gates))


def _conv_qkv(xm, row_block_offset, state, cw, cb, wqk, wv, wg, bg, *, n_groups, tiles_per_group,
              row_stride, k_scale):
    inner = xm.shape[1]
    rows = n_groups * tiles_per_group * ROW_TILE
    halo = state.shape[1]
    nt = tiles_per_group
    row_spec = pl.BlockSpec((ROW_TILE, inner), lambda b, t: (b * nt + t, 0))
    gate_spec = pl.BlockSpec((ROW_TILE, LANES), lambda b, t: (b * nt + t, 0))
    full = lambda a: pl.BlockSpec(a.shape, lambda b, t: (0,) * a.ndim)
    return pl.pallas_call(
        functools.partial(_conv_qkv_body, row_stride=row_stride, k_scale=k_scale),
        grid=(n_groups, nt),
        in_specs=[pl.BlockSpec((ROW_TILE, inner), lambda b, t: (row_block_offset + b * nt + t, 0)),
                  pl.BlockSpec((1, halo, inner), lambda b, t: (b, 0, 0)),
                  full(cw), full(cb), full(wqk), full(wv), full(wg), full(bg)],
        out_specs=[row_spec, row_spec, row_spec, row_spec, gate_spec, gate_spec,
                   pl.BlockSpec((1, halo, inner), lambda b, t: (b, 0, 0))],
        out_shape=[jax.ShapeDtypeStruct((rows, inner), BF16)] * 4
        + [jax.ShapeDtypeStruct((rows, LANES), F32)] * 2
        + [jax.ShapeDtypeStruct((n_groups, halo, inner), F32)],
        compiler_params=_cparams("parallel", "arbitrary"),
        name="conv_qkv",
    )(xm, state, cw, cb, wqk, wv, wg, bg)


def _cumsum_rows(x):
    n = x.shape[0]
    tri = (lax.broadcasted_iota(I32, (n, n), 1) <= lax.broadcasted_iota(I32, (n, n), 0)).astype(BF16)
    p0, p1, p2 = _split3(x)
    return _dot(tri, p0) + _dot(tri, p1) + _dot(tri, p2)


def _cumsum_lanes(x):
    n = x.shape[1]
    tri = (lax.broadcasted_iota(I32, (n, n), 0) <= lax.broadcasted_iota(I32, (n, n), 1)).astype(BF16)
    p0, p1, p2 = _split3(x)
    return _dot(p0, tri) + _dot(p1, tri) + _dot(p2, tri)


def _mlstm_chunk(q, k, v, ig_col, bcum_col, ig_row, bcum_row, c_in, c_out, n_prev, m_prev):
    lq, lk = q.shape[0], k.shape[0]
    ri = lax.broadcasted_iota(I32, (lq, lk), 0)
    ci = lax.broadcasted_iota(I32, (lq, lk), 1)
    bcum_q = bcum_col[:lq]
    log_d = jnp.where(ci <= ri, bcum_q - bcum_row + ig_row, -jnp.inf)
    m_inter = bcum_q + m_prev
    m_t = jnp.maximum(m_inter, jnp.max(log_d, axis=-1, keepdims=True))
    s = _dot_nt(q, k) * jnp.exp(log_d - m_t)
    inter = jnp.exp(m_inter - m_t)
    m_new = m_t[lq - 1:lq, :]
    b_last = bcum_col[lk - 1:lk, :]
    w_col = jnp.exp(b_last - bcum_col + ig_col - m_new)
    decay = jnp.exp(b_last + m_prev - m_new)
    wk = w_col * k.astype(F32)
    c_prev = c_in[...]
    num = _dot(s.astype(BF16), v) + inter * _dot(q, c_prev.astype(BF16))
    c_out[...] = decay * c_prev + _dot_tn(wk.astype(BF16), v)
    den = jnp.sum(s, axis=-1, keepdims=True) + inter * jnp.sum(q.astype(F32) * n_prev, axis=-1, keepdims=True)
    h = num * (1.0 / jnp.maximum(jnp.abs(den), jnp.exp(-m_t)))
    n_new = decay * n_prev + jnp.sum(wk, axis=0, keepdims=True)
    return h, n_new, m_new


def _head_norm(h, w, eps=1e-5):
    mu = jnp.mean(h, axis=-1, keepdims=True)
    var = jnp.mean(jnp.square(h - mu), axis=-1, keepdims=True)
    return (h - mu) * lax.rsqrt(var + eps) * w


def _cell_prompt_body(q_ref, k_ref, v_ref, gc_ref, gr_ref, hw_ref,
                      h_ref, c_ref, n_ref, m_ref, *, heads):
    c = pl.program_id(1)

    @pl.when(c == 0)
    def _():
        c_ref[...] = jnp.zeros(c_ref.shape, F32)
        n_ref[...] = jnp.zeros(n_ref.shape, F32)
        m_ref[...] = jnp.full(m_ref.shape, STAB_INIT, F32)

    gc = gc_ref[...]
    gr = gr_ref[...]
    half = N_GATES // 2
    dh = q_ref.shape[1] // heads
    for hd in range(heads):
        sl = slice(hd * dh, (hd + 1) * dh)
        h, n_new, m_new = _mlstm_chunk(
            q_ref[:, sl], k_ref[:, sl], v_ref[:, sl],
            gc[:, hd:hd + 1], gc[:, hd + half:hd + half + 1],
            gr[hd:hd + 1, :], gr[hd + half:hd + half + 1, :],
            c_ref.at[0, hd], c_ref.at[0, hd], n_ref[0, hd], m_ref[0, hd, :, 0:1])
        h_ref[:, sl] = _head_norm(h, hw_ref[hd]).astype(BF16)
        n_ref[0, hd] = n_new
        m_ref[0, hd] = jnp.broadcast_to(m_new, m_ref.shape[2:])


def _cell_prompt(q, k, v, gc, gr, hn_w, *, batch, heads):
    tp, inner = q.shape
    dh = inner // heads
    seq = tp // batch
    ln = MLSTM_CHUNK_PROMPT
    nc = seq // ln
    qspec = pl.BlockSpec((ln, inner), lambda b, c: (b * nc + c, 0))
    return pl.pallas_call(
        functools.partial(_cell_prompt_body, heads=heads),
        grid=(batch, nc),
        in_specs=[qspec, qspec, qspec,
                  pl.BlockSpec((ln, LANES), lambda b, c: (b * nc + c, 0)),
                  pl.BlockSpec((SUBLANES, ln), lambda b, c: (0, b * nc + c)),
                  pl.BlockSpec((heads, 1, dh), lambda b, c: (0, 0, 0))],
        out_specs=[qspec,
                   pl.BlockSpec((1, heads, dh, dh), lambda b, c: (b, 0, 0, 0)),
                   pl.BlockSpec((1, heads, 1, dh), lambda b, c: (b, 0, 0, 0)),
                   pl.BlockSpec((1, heads, 1, LANES), lambda b, c: (b, 0, 0, 0))],
        out_shape=[jax.ShapeDtypeStruct((tp, inner), BF16),
                   jax.ShapeDtypeStruct((batch, heads, dh, dh), F32),
                   jax.ShapeDtypeStruct((batch, heads, 1, dh), F32),
                   jax.ShapeDtypeStruct((batch, heads, 1, LANES), F32)],
        compiler_params=_cparams("parallel", "arbitrary"),
        name="mlstm_cell_prompt",
    )(q, k, v, gc, gr, hn_w)


STATE_RING = 3


def _cell_sample_body(q_ref, k_ref, v_ref, gc_ref, gr_ref, hw_ref, c0_hbm, n0_ref, m0_ref,
                      h_ref, c_ref, n_ref, m_ref, c0_buf, c0_sem, *, heads, valid):
    b = pl.program_id(0)
    nb = pl.num_programs(0)

    def fetch(step):
        slot = step % STATE_RING
        return pltpu.make_async_copy(c0_hbm.at[step], c0_buf.at[slot], c0_sem.at[slot])

    @pl.when(b == 0)
    def _():
        for step in range(STATE_RING - 1):
            @pl.when(step < nb)
            def _():
                fetch(step).start()

    @pl.when(b + STATE_RING - 1 < nb)
    def _():
        fetch(b + STATE_RING - 1).start()

    fetch(b).wait()
    c0_ref = c0_buf.at[b % STATE_RING]
    ln = MLSTM_CHUNK_SAMPLE
    rows = q_ref.shape[1]
    dh = q_ref.shape[2] // heads
    half = N_GATES // 2
    gc = gc_ref[0]
    gr = gr_ref[0]
    row_c = lax.broadcasted_iota(I32, gc.shape, 0)
    lane_c = lax.broadcasted_iota(I32, gc.shape, 1)
    gc = jnp.where(row_c < valid, gc, jnp.where(lane_c < half, -jnp.inf, 0.0))
    row_r = lax.broadcasted_iota(I32, gr.shape, 0)
    lane_r = lax.broadcasted_iota(I32, gr.shape, 1)
    gr = jnp.where(lane_r < valid, gr, jnp.where(row_r < half, -jnp.inf, 0.0))
    bc = _cumsum_rows(jnp.where(lane_c < half, 0.0, gc))
    br = _cumsum_lanes(jnp.where(row_r < half, 0.0, gr))
    pad = jnp.zeros((ln - rows, dh), F32)

    def padded(ref, sl):
        return jnp.concatenate([ref[0, :, sl].astype(F32), pad], axis=0).astype(BF16)

    for hd in range(heads):
        sl = slice(hd * dh, (hd + 1) * dh)
        h, n_new, m_new = _mlstm_chunk(
            q_ref[0, :, sl], padded(k_ref, sl), padded(v_ref, sl),
            gc[:, hd:hd + 1], bc[:, hd + half:hd + half + 1],
            gr[hd:hd + 1, :], br[hd + half:hd + half + 1, :],
            c0_ref.at[hd], c_ref.at[0, hd], n0_ref[0, hd], m0_ref[0, hd, :, 0:1])
        h_ref[0, :, sl] = _head_norm(h, hw_ref[hd]).astype(BF16)
        n_ref[0, hd] = n_new
        m_ref[0, hd] = jnp.broadcast_to(m_new, m_ref.shape[2:])


def _cell_sample(q, k, v, gc, gr, hn_w, c0, n0, m0, *, heads, valid, group_batch):
    n_groups, rows, wide = q.shape
    inner = wide // group_batch
    dh = inner // heads
    nb = n_groups * group_batch
    ln = MLSTM_CHUNK_SAMPLE
    qspec = pl.BlockSpec((1, rows, inner), lambda b: (b // group_batch, 0, b % group_batch))
    cspec = pl.BlockSpec((1, heads, dh, dh), lambda b: (b, 0, 0, 0))
    nspec = pl.BlockSpec((1, heads, 1, dh), lambda b: (b, 0, 0, 0))
    mspec = pl.BlockSpec((1, heads, 1, LANES), lambda b: (b, 0, 0, 0))
    return pl.pallas_call(
        functools.partial(_cell_sample_body, heads=heads, valid=valid),
        grid=(nb,),
        in_specs=[qspec, qspec, qspec,
                  pl.BlockSpec((1, ln, LANES), lambda b: (b, 0, 0)),
                  pl.BlockSpec((1, SUBLANES, ln), lambda b: (b, 0, 0)),
                  pl.BlockSpec((heads, 1, dh), lambda b: (0, 0, 0)),
                  pl.BlockSpec(memory_space=pl.ANY), nspec, mspec],
        out_specs=[qspec, cspec, nspec, mspec],
        out_shape=[jax.ShapeDtypeStruct(q.shape, BF16),
                   jax.ShapeDtypeStruct((nb, heads, dh, dh), F32),
                   jax.ShapeDtypeStruct((nb, heads, 1, dh), F32),
                   jax.ShapeDtypeStruct((nb, heads, 1, LANES), F32)],
        scratch_shapes=[pltpu.VMEM((STATE_RING, heads, dh, dh), F32),
                        pltpu.SemaphoreType.DMA((STATE_RING,))],
        compiler_params=_cparams("arbitrary"),
        name="mlstm_cell_sample",
    )(q, k, v, gc, gr, hn_w, c0, n0, m0)


def _gate_down_body(hp_ref, hs_ref, cp_ref, cs_ref, z_ref, skip_ref, w_ref, xp_ref, xs_ref, o_ref,
                    *, n_prompt_tiles):
    i = pl.program_id(0)
    is_p = i < n_prompt_tiles
    hn = jnp.where(is_p, hp_ref[...], hs_ref[...]).astype(F32)
    xc = jnp.where(is_p, cp_ref[...], cs_ref[...]).astype(F32)
    xres = jnp.where(is_p, xp_ref[...], xs_ref[...])
    a = (hn + skip_ref[...] * xc) * _silu(z_ref[...].astype(F32))
    o_ref[...] = xres + _dot(a.astype(BF16), w_ref[...])


def _gate_down(hn_p, hn_s, xc_p, xc_s, z, skip, w, xp, xs):
    tp, inner = hn_p.shape
    tt = z.shape[0]
    d = w.shape[1]
    npt = tp // ROW_TILE
    ip, is_ = _two_src_specs(npt, inner)
    dp, ds = _two_src_specs(npt, d)
    return pl.pallas_call(
        functools.partial(_gate_down_body, n_prompt_tiles=npt),
        grid=(tt // ROW_TILE,),
        in_specs=[ip, is_, ip, is_,
                  pl.BlockSpec((ROW_TILE, inner), lambda i: (i, 0)),
                  pl.BlockSpec((1, inner), lambda i: (0, 0)),
                  pl.BlockSpec((inner, d), lambda i: (0, 0)),
                  dp, ds],
        out_specs=pl.BlockSpec((ROW_TILE, d), lambda i: (i, 0)),
        out_shape=jax.ShapeDtypeStruct((tt, d), F32),
        compiler_params=_cparams("parallel"),
        name="gate_down",
    )(hn_p, hn_s, xc_p, xc_s, z, skip, w, xp, xs)


def _cmlp_in_body(x_ref, g_ref, w_ref, b_ref, lw_ref, lb_ref, u_ref, v_ref, *, half):
    h = _rms(x_ref[...], g_ref[...]).astype(BF16)
    y = _dot(h, w_ref[...]) + b_ref[...]
    zz = y * (0.5 * (1.0 + jnp.tanh(math.sqrt(2.0 / math.pi) * (y + 0.044715 * (y * y * y)))))
    u_ref[...] = zz[:, :half].astype(BF16)
    v = zz[:, half:]
    mu = jnp.mean(v, axis=-1, keepdims=True)
    var = jnp.mean(jnp.square(v - mu), axis=-1, keepdims=True)
    v_ref[...] = (v - mu) * lax.rsqrt(var + 1e-5) * lw_ref[...] + lb_ref[...]


def _cmlp_in(x, g, w, b, lw, lb):
    tt, d = x.shape
    n2 = w.shape[1]
    half = n2 // 2
    vec = lambda n: pl.BlockSpec((1, n), lambda i: (0, 0))
    return pl.pallas_call(
        functools.partial(_cmlp_in_body, half=half),
        grid=(tt // ROW_TILE,),
        in_specs=[pl.BlockSpec((ROW_TILE, d), lambda i: (i, 0)), vec(d),
                  pl.BlockSpec((d, n2), lambda i: (0, 0)), vec(n2), vec(half), vec(half)],
        out_specs=[pl.BlockSpec((ROW_TILE, half), lambda i: (i, 0))] * 2,
        out_shape=[jax.ShapeDtypeStruct((tt, half), BF16), jax.ShapeDtypeStruct((tt, half), F32)],
        compiler_params=_cparams("parallel"),
        name="cmlp_in",
    )(x, g, w, b, lw, lb)


def _cmlp_mix_out_body(u_ref, v_ref, mix_ref, bias_ref, w_ref, bo_ref, x_ref, o_ref, *, groups):
    half = u_ref.shape[1]
    gd = half // groups
    pieces = []
    for g in range(groups):
        sl = slice(g * gd, (g + 1) * gd)
        mixed = _dot(mix_ref[0, g], v_ref[:, sl].astype(BF16)) + bias_ref[0, :, g:g + 1]
        pieces.append((u_ref[:, sl].astype(F32) * mixed).astype(BF16))
    a = jnp.concatenate(pieces, axis=1)
    o_ref[...] = x_ref[...] + _dot(a, w_ref[...]) + bo_ref[...]


def _cmlp_mix_out(u, v, mix, bias, w, bo, x, *, n_prompt_tiles):
    tt, half = u.shape
    d = w.shape[1]
    groups = mix.shape[1]
    kind = lambda i: jnp.where(i < n_prompt_tiles, 0, 1)
    return pl.pallas_call(
        functools.partial(_cmlp_mix_out_body, groups=groups),
        grid=(tt // ROW_TILE,),
        in_specs=[pl.BlockSpec((ROW_TILE, half), lambda i: (i, 0)),
                  pl.BlockSpec((ROW_TILE, half), lambda i: (i, 0)),
                  pl.BlockSpec((1, groups, ROW_TILE, ROW_TILE), lambda i: (kind(i), 0, 0, 0)),
                  pl.BlockSpec((1, ROW_TILE, groups), lambda i: (kind(i), 0, 0)),
                  pl.BlockSpec((half, d), lambda i: (0, 0)),
                  pl.BlockSpec((1, d), lambda i: (0, 0)),
                  pl.BlockSpec((ROW_TILE, d), lambda i: (i, 0))],
        out_specs=pl.BlockSpec((ROW_TILE, d), lambda i: (i, 0)),
        out_shape=jax.ShapeDtypeStruct((tt, d), F32),
        compiler_params=_cparams("parallel"),
        name="cmlp_mix_out",
    )(u, v, mix, bias, w, bo, x)


GROUP_ROW_OFFSET = SUBLANES


def _first_argmax_rows(x, n):
    row = lax.broadcasted_iota(I32, x.shape, 0)
    mx = jnp.max(x, axis=0, keepdims=True)
    idx = jnp.min(jnp.where(x == mx, row, n), axis=0, keepdims=True)
    return idx, mx


def _router_body(x_ref, g_ref, wt_ref, bt_ref, route_ref, cnt_ref, *, n_groups, per_group):
    i = pl.program_id(0)
    n_exp = n_groups * per_group

    @pl.when(i == 0)
    def _():
        cnt_ref[...] = jnp.zeros(cnt_ref.shape, F32)

    h = _rms(x_ref[...], g_ref[...])
    logits = _dot_nt_f32(wt_ref[...], h) + bt_ref[:, 0:1]
    tokens = logits.shape[1]
    gl = logits[0:n_groups, :]
    gmax = jnp.max(gl, axis=0, keepdims=True)
    ge = jnp.exp(gl - gmax)
    gp = ge / jnp.sum(ge, axis=0, keepdims=True)
    g_idx, g_w = _first_argmax_rows(gp, n_groups)
    e_sel = jnp.zeros((per_group, tokens), F32)
    for g in range(n_groups):
        lo = GROUP_ROW_OFFSET + g * per_group
        e_sel = e_sel + jnp.where(g_idx == g, logits[lo:lo + per_group, :], 0.0)
    i1, m1 = _first_argmax_rows(e_sel, per_group)
    row = lax.broadcasted_iota(I32, e_sel.shape, 0)
    i2, m2 = _first_argmax_rows(jnp.where(row == i1, -jnp.inf, e_sel), per_group)
    t = jnp.exp(m2 - m1)
    w1 = (1.0 / (1.0 + t)) * g_w
    w2 = (t / (1.0 + t)) * g_w
    e1 = g_idx * per_group + i1
    e2 = g_idx * per_group + i2

    erow = lax.broadcasted_iota(I32, (n_exp, tokens), 0)
    oh1 = erow == e1
    oh2 = erow == e2
    onehot = jnp.where(oh1 | oh2, 1.0, 0.0)
    strict_upper = (lax.broadcasted_iota(I32, (tokens, tokens), 0)
                    < lax.broadcasted_iota(I32, (tokens, tokens), 1)).astype(BF16)
    before = _dot(onehot.astype(BF16), strict_upper) + cnt_ref[:, 0:1]
    r1 = jnp.sum(jnp.where(oh1, before, 0.0), axis=0, keepdims=True)
    r2 = jnp.sum(jnp.where(oh2, before, 0.0), axis=0, keepdims=True)
    cnt_ref[...] = cnt_ref[...] + jnp.sum(onehot, axis=1, keepdims=True)
    zero = jnp.zeros_like(w1)
    route_ref[...] = jnp.concatenate(
        [e1.astype(F32), e2.astype(F32), r1, r2, w1, w2, zero, zero], axis=0)


def _wide_tile(n_rows, max_tiles):
    n = n_rows // ROW_TILE
    return ROW_TILE * max(k for k in range(1, max_tiles + 1) if n % k == 0)


def _router(x, g, wt, bt, *, n_groups, per_group):
    tt, d = x.shape
    n_exp = n_groups * per_group
    rt = _wide_tile(tt, 2)
    return pl.pallas_call(
        functools.partial(_router_body, n_groups=n_groups, per_group=per_group),
        grid=(tt // rt,),
        in_specs=[pl.BlockSpec((rt, d), lambda i: (i, 0)),
                  pl.BlockSpec((1, d), lambda i: (0, 0)),
                  pl.BlockSpec(wt.shape, lambda i: (0, 0)),
                  pl.BlockSpec(bt.shape, lambda i: (0, 0))],
        out_specs=[pl.BlockSpec((SUBLANES, rt), lambda i: (0, i)),
                   pl.BlockSpec((n_exp, LANES), lambda i: (0, 0))],
        out_shape=[jax.ShapeDtypeStruct((SUBLANES, tt), F32),
                   jax.ShapeDtypeStruct((n_exp, LANES), F32)],
        compiler_params=_cparams("arbitrary"),
        name="moe_router",
    )(x, g, wt, bt)


def _positions_body(route_ref, off_ref, pos_ref):
    r = route_ref[...]
    n_exp = off_ref.shape[0]
    tokens = r.shape[1]
    erow = lax.broadcasted_iota(I32, (n_exp, tokens), 0)
    off = off_ref[:, 0:1]
    out = []
    for k in range(2):
        e = r[k:k + 1, :].astype(I32)
        base = jnp.sum(jnp.where(erow == e, off, 0.0), axis=0, keepdims=True)
        out.append((base + r[2 + k:3 + k, :]).astype(I32))
    pos_ref[...] = jnp.concatenate(out, axis=0)


def _positions(route, off):
    tt = route.shape[1]
    pt = _wide_tile(tt, 16)
    return pl.pallas_call(
        _positions_body,
        grid=(tt // pt,),
        in_specs=[pl.BlockSpec((SUBLANES, pt), lambda i: (0, i)),
                  pl.BlockSpec(off.shape, lambda i: (0, 0))],
        out_specs=pl.BlockSpec((2, pt), lambda i: (0, i)),
        out_shape=jax.ShapeDtypeStruct((2, tt), I32),
        compiler_params=_cparams("parallel"),
        name="moe_positions",
    )(route, off)


def _row_copy(src, src_row, dst, dst_row, sem):
    return pltpu.make_async_copy(src.at[pl.ds(src_row, 1)], dst.at[pl.ds(dst_row, 1)], sem)


def _dispatch_body(pos_ref, x_ref, xs_hbm, sem, *, n_tokens):
    i = pl.program_id(0)
    base = i * ROW_TILE

    def copies(r):
        return [_row_copy(x_ref, r, xs_hbm, pos_ref[k * n_tokens + base + r], sem) for k in range(2)]

    def start(r, carry):
        for cp in copies(r):
            cp.start()
        return carry

    def wait(r, carry):
        for cp in copies(r):
            cp.wait()
        return carry

    lax.fori_loop(0, ROW_TILE, start, 0, unroll=16)
    lax.fori_loop(0, ROW_TILE, wait, 0, unroll=16)


def _dispatch(pos_flat, x):
    tt, d = x.shape
    return pl.pallas_call(
        functools.partial(_dispatch_body, n_tokens=tt),
        grid_spec=pltpu.PrefetchScalarGridSpec(
            num_scalar_prefetch=1,
            grid=(tt // ROW_TILE,),
            in_specs=[pl.BlockSpec((ROW_TILE, d), lambda i, p: (i, 0))],
            out_specs=pl.BlockSpec(memory_space=pl.ANY),
            scratch_shapes=[pltpu.SemaphoreType.DMA]),
        out_shape=jax.ShapeDtypeStruct((2 * tt, d), F32),
        compiler_params=_cparams("arbitrary"),
        name="moe_dispatch",
    )(pos_flat, x)


def _experts_body(tile_ref, exp_ref, lo_ref, hi_ref, nwork_ref,
                  xs_ref, g_ref, wg_ref, wu_ref, wd_ref, o_ref, wg_b, wu_b, wd_b):
    w = pl.program_id(0)
    prev = jnp.maximum(w - 1, 0)
    first = jnp.logical_or(w == 0, tile_ref[w] != tile_ref[prev])
    new_expert = jnp.logical_or(w == 0, exp_ref[w] != exp_ref[prev])

    @pl.when(w < nwork_ref[0])
    def _():
        @pl.when(new_expert)
        def _():
            wg_b[...] = wg_ref[0].astype(BF16)
            wu_b[...] = wu_ref[0].astype(BF16)
            wd_b[...] = wd_ref[0].astype(BF16)

        xb = _rms(xs_ref[...], g_ref[...]).astype(BF16)
        a = _dot(xb, wg_b[...])
        b = _dot(xb, wu_b[...])
        hg = (_silu(a) * b).astype(BF16)
        y = _dot(hg, wd_b[...])
        row = lax.broadcasted_iota(I32, y.shape, 0)
        y = jnp.where((row >= lo_ref[w]) & (row < hi_ref[w]), y, 0.0)

        @pl.when(first)
        def _():
            o_ref[...] = y

        @pl.when(jnp.logical_not(first))
        def _():
            o_ref[...] += y


def _experts(meta, xs, g, wg, wu, wd):
    rows, d = xs.shape
    n_work = meta[0].shape[0]
    f = wg.shape[2]
    return pl.pallas_call(
        _experts_body,
        grid_spec=pltpu.PrefetchScalarGridSpec(
            num_scalar_prefetch=5,
            grid=(n_work,),
            in_specs=[pl.BlockSpec((EXPERT_TILE, d), lambda w, tl, ex, lo, hi, nw: (tl[w], 0)),
                      pl.BlockSpec((1, d), lambda w, tl, ex, lo, hi, nw: (0, 0)),
                      pl.BlockSpec((1, d, f), lambda w, tl, ex, lo, hi, nw: (ex[w], 0, 0)),
                      pl.BlockSpec((1, d, f), lambda w, tl, ex, lo, hi, nw: (ex[w], 0, 0)),
                      pl.BlockSpec((1, f, d), lambda w, tl, ex, lo, hi, nw: (ex[w], 0, 0))],
            out_specs=pl.BlockSpec((EXPERT_TILE, d), lambda w, tl, ex, lo, hi, nw: (tl[w], 0)),
            scratch_shapes=[pltpu.VMEM((d, f), BF16), pltpu.VMEM((d, f), BF16), pltpu.VMEM((f, d), BF16)]),
        out_shape=jax.ShapeDtypeStruct((rows, d), F32),
        compiler_params=_cparams("arbitrary"),
        name="moe_experts",
    )(*meta, xs, g, wg, wu, wd)


def _combine_rows(pos_ref, x_ref, route_ref, ys_hbm, buf, sem, n_tokens):
    base = pl.program_id(0) * ROW_TILE

    def copies(r):
        t = base + r
        return [_row_copy(ys_hbm, pos_ref[k * n_tokens + t], buf.at[k], r, sem) for k in range(2)]

    def start(r, carry):
        for cp in copies(r):
            cp.start()
        return carry

    def wait(r, carry):
        for cp in copies(r):
            cp.wait()
        return carry

    lax.fori_loop(0, ROW_TILE, start, 0, unroll=16)
    r = route_ref[...]
    rt = jnp.concatenate([r, jnp.zeros((LANES - SUBLANES, r.shape[1]), F32)], axis=0).T
    lax.fori_loop(0, ROW_TILE, wait, 0, unroll=16)
    return x_ref[...] + rt[:, 4:5] * buf[0] + rt[:, 5:6] * buf[1]


def _combine_body(pos_ref, x_ref, route_ref, ys_hbm, o_ref, buf, sem, *, n_tokens):
    o_ref[...] = _combine_rows(pos_ref, x_ref, route_ref, ys_hbm, buf, sem, n_tokens)


def _combine_norm_body(pos_ref, x_ref, route_ref, g_ref, ys_hbm, op_ref, os_ref, buf, sem,
                       *, n_tokens, n_prompt_tiles):
    y = _rms(_combine_rows(pos_ref, x_ref, route_ref, ys_hbm, buf, sem, n_tokens), g_ref[...])
    is_prompt = pl.program_id(0) < n_prompt_tiles

    @pl.when(is_prompt)
    def _():
        op_ref[...] = y

    @pl.when(jnp.logical_not(is_prompt))
    def _():
        os_ref[...] = y


def _combine(pos_flat, x, route, ys, final_g=None, n_prompt_tiles=None):
    tt, d = x.shape
    in_specs = [pl.BlockSpec((ROW_TILE, d), lambda i, p: (i, 0)),
                pl.BlockSpec((SUBLANES, ROW_TILE), lambda i, p: (0, i))]
    args = [pos_flat, x, route]
    if final_g is None:
        body = functools.partial(_combine_body, n_tokens=tt)
        out_specs = pl.BlockSpec((ROW_TILE, d), lambda i, p: (i, 0))
        out_shape = jax.ShapeDtypeStruct((tt, d), F32)
    else:
        npt = n_prompt_tiles
        body = functools.partial(_combine_norm_body, n_tokens=tt, n_prompt_tiles=npt)
        in_specs.append(pl.BlockSpec((1, d), lambda i, p: (0, 0)))
        args.append(final_g)
        out_specs = [pl.BlockSpec((ROW_TILE, d), lambda i, p: (jnp.minimum(i, npt - 1), 0)),
                     pl.BlockSpec((ROW_TILE, d), lambda i, p: (jnp.maximum(i - npt, 0), 0))]
        out_shape = [jax.ShapeDtypeStruct((npt * ROW_TILE, d), F32),
                     jax.ShapeDtypeStruct((tt - npt * ROW_TILE, d), F32)]
    in_specs.append(pl.BlockSpec(memory_space=pl.ANY))
    args.append(ys)
    return pl.pallas_call(
        body,
        grid_spec=pltpu.PrefetchScalarGridSpec(
            num_scalar_prefetch=1,
            grid=(tt // ROW_TILE,),
            in_specs=in_specs,
            out_specs=out_specs,
            scratch_shapes=[pltpu.VMEM((2, ROW_TILE, d), F32), pltpu.SemaphoreType.DMA]),
        out_shape=out_shape,
        compiler_params=_cparams("arbitrary"),
        name="moe_combine",
    )(*args)


def _work_items(counts, n_rows):
    n_exp = counts.shape[0]
    n_tiles = n_rows // EXPERT_TILE
    n_work = n_tiles + n_exp - 1
    ids = jnp.arange(n_exp, dtype=I32)
    lower = ids[None, :] <= ids[:, None]

    def cumsum(v):
        return jnp.sum(jnp.where(lower, v[None, :], 0), axis=1)

    end = cumsum(counts)
    off = end - counts
    first_tile = off // EXPERT_TILE
    last_tile = (end - 1) // EXPERT_TILE
    per_exp = jnp.where(counts > 0, last_tile - first_tile + 1, 0)
    wend = cumsum(per_exp)
    wstart = wend - per_exp
    total = jnp.sum(per_exp)
    w = jnp.arange(n_work, dtype=I32)
    e = jnp.minimum(jnp.sum((wend[None, :] <= w[:, None]).astype(I32), axis=1), n_exp - 1)
    sel = e[:, None] == ids[None, :]

    def pick(v):
        return jnp.sum(jnp.where(sel, v[None, :], 0), axis=1)

    tile = pick(first_tile) + (w - pick(wstart))
    lo = jnp.maximum(pick(off), tile * EXPERT_TILE) - tile * EXPERT_TILE
    hi = jnp.minimum(pick(end), (tile + 1) * EXPERT_TILE) - tile * EXPERT_TILE
    live = w < total
    tile = jnp.where(live, tile, n_tiles - 1)
    lo = jnp.where(live, lo, 0)
    hi = jnp.where(live, hi, 0)
    return off, (tile.astype(I32), e.astype(I32), lo.astype(I32), hi.astype(I32),
                 total.reshape(1).astype(I32))


def _moe(x, g, layer, w_rg, b_rg, w_re, b_re, w_gate, w_up, w_down, final_g=None, n_prompt_tiles=None):
    tt, d = x.shape
    n_groups, per_group = w_re.shape[0], w_re.shape[2]
    n_exp = n_groups * per_group
    f = w_gate.shape[-1]
    pad_rows = jnp.zeros((GROUP_ROW_OFFSET - n_groups, d), F32)
    wt = jnp.concatenate([w_rg.T, pad_rows, jnp.transpose(w_re, (0, 2, 1)).reshape(n_exp, d)], axis=0)
    bt = jnp.concatenate([b_rg, jnp.zeros((GROUP_ROW_OFFSET - n_groups,), F32), b_re.reshape(n_exp)])
    bt = jnp.broadcast_to(bt[:, None], (wt.shape[0], LANES))
    g2 = g.reshape(1, d)

    route, cnt = _router(x, g2, wt, bt, n_groups=n_groups, per_group=per_group)
    counts = cnt[:, 0].astype(I32)
    off, (tile, exp, lo, hi, total) = _work_items(counts, 2 * tt)
    off_b = jnp.broadcast_to(off.astype(F32)[:, None], (n_exp, LANES))
    pos = _positions(route, off_b).reshape(2 * tt)
    xs = _dispatch(pos, x)
    ys = _experts((tile, exp + layer * n_exp, lo, hi, total), xs, g2,
                  w_gate.reshape(-1, d, f), w_up.reshape(-1, d, f), w_down.reshape(-1, f, d))
    return _combine(pos, x, route, ys, final_g, n_prompt_tiles)


def _block_diag_tiles(w, tile):
    nb, c, _ = w.shape
    rows = w.reshape(nb * c // tile, tile, c)
    blk = jnp.arange(tile, dtype=I32) // c
    return jnp.where(blk[:, None] == blk[None, :], jnp.tile(rows, (1, 1, tile // c)), 0.0)


def _sample_to_rows(x, group_batch):
    db, s, w = x.shape
    return x.reshape(db // group_batch, group_batch, s, w).transpose(0, 2, 1, 3).reshape(db * s, w)


def _rows_to_sample(x, group_batch, s):
    rows, w = x.shape
    db = rows // s
    return x.reshape(db // group_batch, s, group_batch, w).transpose(0, 2, 1, 3).reshape(db, s, w)


def kernel(x_prompt, x_sample, state_mlstm_C, state_mlstm_n, state_mlstm_m, state_mlstm_conv,
           norm_mix, norm_ffn, norm_final,
           mlstm_w_up, mlstm_conv_w, mlstm_conv_b, mlstm_w_q, mlstm_w_k, mlstm_w_v,
           mlstm_w_ig, mlstm_b_ig, mlstm_w_fg, mlstm_b_fg, mlstm_skip, mlstm_hn_w, mlstm_w_down,
           cmlp_w_in, cmlp_b_in, cmlp_ln_w, cmlp_ln_b, cmlp_w_s, cmlp_b_s, cmlp_w_out, cmlp_b_out,
           moe_w_rg, moe_b_rg, moe_w_re, moe_b_re, moe_w_gate, moe_w_up, moe_w_down):
    batch, seq, d = x_prompt.shape
    dec_batch, dec_seq, _ = x_sample.shape
    heads = state_mlstm_C.shape[2]
    dh = state_mlstm_C.shape[3]
    inner = heads * dh
    tp, ts = batch * seq, dec_batch * dec_seq
    npt = tp // ROW_TILE
    gb = ROW_TILE // dec_seq
    n_sgroups = dec_batch // gb
    half_g = N_GATES // 2
    assert heads == half_g and tp % ROW_TILE == 0 and ts % ROW_TILE == 0
    assert ROW_TILE % dec_seq == 0 and dec_batch % gb == 0 and seq % MLSTM_CHUNK_PROMPT == 0
    assert seq % ROW_TILE == 0 and ROW_TILE % CMLP_CHUNK == 0 and dec_seq <= SUBLANES
    assert MLSTM_CHUNK_PROMPT == ROW_TILE

    xp = x_prompt.reshape(tp, d)
    xs = _sample_to_rows(x_sample, gb)

    xm, z = _norm_up(xp, xs, norm_mix[0].reshape(1, d), mlstm_w_up[0].astype(BF16))

    wq = _block_diag_tiles(mlstm_w_q[0], MXU_WIDTH)
    wk = _block_diag_tiles(mlstm_w_k[0], MXU_WIDTH)
    wqk = jnp.concatenate([wq, wk], axis=2).astype(BF16)
    wv = _block_diag_tiles(mlstm_w_v[0], MXU_WIDTH).astype(BF16)
    wg = jnp.concatenate([mlstm_w_ig[0], mlstm_w_fg[0]], axis=1)
    wg = jnp.pad(wg, ((0, 0), (0, LANES - N_GATES))).reshape(3, inner, LANES).astype(BF16)
    bg = jnp.pad(jnp.concatenate([mlstm_b_ig[0], mlstm_b_fg[0]]), (0, LANES - N_GATES)).reshape(1, LANES)
    cw, cb = mlstm_conv_w[0], mlstm_conv_b[0].reshape(1, inner)
    k_scale = float(dh) ** -0.5

    conv0_p = jnp.zeros((batch, SUBLANES, inner), F32)
    q_p, k_p, v_p, xc_p, _, bc_p, conv_p = _conv_qkv(
        xm, 0, conv0_p, cw, cb, wqk, wv, wg, bg,
        n_groups=batch, tiles_per_group=seq // ROW_TILE, row_stride=1, k_scale=k_scale)
    halo_s = (CONV_WIDTH - 1) * gb
    conv0_s = state_mlstm_conv[0].reshape(n_sgroups, gb, CONV_WIDTH - 1, inner)
    conv0_s = conv0_s.transpose(0, 2, 1, 3).reshape(n_sgroups, halo_s, inner)
    q_s, k_s, v_s, xc_s, gc_s, _, conv_s = _conv_qkv(
        xm, npt, conv0_s, cw, cb, wqk, wv, wg, bg,
        n_groups=n_sgroups, tiles_per_group=1, row_stride=gb, k_scale=k_scale)

    hn_w = mlstm_hn_w[0].reshape(heads, 1, dh)
    br_p = jnp.transpose(bc_p[:, :SUBLANES])
    hn_p, c_p, n_p, m_p = _cell_prompt(q_p, k_p, v_p, bc_p, br_p, hn_w, batch=batch, heads=heads)

    def pad_steps(a):
        a = a.reshape(n_sgroups, dec_seq, gb * inner)
        return jnp.pad(a, ((0, 0), (0, SUBLANES - dec_seq), (0, 0)))

    ln_s = MLSTM_CHUNK_SAMPLE
    gcs = gc_s.reshape(n_sgroups, dec_seq, gb, LANES).transpose(0, 2, 1, 3).reshape(dec_batch, dec_seq, LANES)
    gcs = jnp.pad(gcs, ((0, 0), (0, ln_s - dec_seq), (0, 0)))
    grs = jnp.transpose(gcs[:, :, :SUBLANES], (0, 2, 1))
    m0 = jnp.broadcast_to(state_mlstm_m[0][:, :, None, None], (dec_batch, heads, 1, LANES))
    hn_s, c_s, n_s, m_s = _cell_sample(
        pad_steps(q_s), pad_steps(k_s), pad_steps(v_s), gcs, grs, hn_w,
        state_mlstm_C[0], state_mlstm_n[0].reshape(dec_batch, heads, 1, dh), m0,
        heads=heads, valid=dec_seq, group_batch=gb)
    hn_s = hn_s[:, :dec_seq].reshape(ts, inner)

    x1 = _gate_down(hn_p, hn_s, xc_p, xc_s, z, mlstm_skip[0].reshape(1, inner),
                    mlstm_w_down[0].astype(BF16), xp, xs)
    x2 = _moe(x1, norm_ffn[0], 0, moe_w_rg[0], moe_b_rg[0], moe_w_re[0], moe_b_re[0],
              moe_w_gate, moe_w_up, moe_w_down)

    half = cmlp_w_in.shape[2] // 2
    groups = cmlp_w_s.shape[1]
    u, vn = _cmlp_in(x2, norm_mix[1].reshape(1, d), cmlp_w_in[0].astype(BF16),
                     cmlp_b_in[0].reshape(1, 2 * half), cmlp_ln_w[0].reshape(1, half),
                     cmlp_ln_b[0].reshape(1, half))
    causal = jnp.tril(jnp.ones((CMLP_CHUNK, CMLP_CHUNK), dtype=bool))
    ws = jnp.where(causal, cmlp_w_s[0], 0.0)
    reps = ROW_TILE // CMLP_CHUNK
    rid = jnp.arange(ROW_TILE, dtype=I32)
    same_chunk = (rid[:, None] // CMLP_CHUNK) == (rid[None, :] // CMLP_CHUNK)
    same_seq = (rid[:, None] % gb) == (rid[None, :] % gb)
    mix_p = jnp.where(same_chunk, jnp.tile(ws, (1, reps, reps)), 0.0)
    ws_s = ws[:, :dec_seq, :dec_seq]
    mix_s = jnp.where(same_seq, jnp.repeat(jnp.repeat(ws_s, gb, axis=1), gb, axis=2), 0.0)
    mix = jnp.stack([mix_p, mix_s]).astype(BF16)
    bias_t = jnp.transpose(cmlp_b_s[0])
    bias = jnp.stack([jnp.tile(bias_t, (reps, 1)), jnp.repeat(bias_t[:dec_seq], gb, axis=0)])
    x3 = _cmlp_mix_out(u, vn, mix, bias, cmlp_w_out[0].astype(BF16), cmlp_b_out[0].reshape(1, d), x2,
                       n_prompt_tiles=npt)
    y_p, y_s = _moe(x3, norm_ffn[1], 1, moe_w_rg[1], moe_b_rg[1], moe_w_re[1], moe_b_re[1],
                    moe_w_gate, moe_w_up, moe_w_down,
                    final_g=norm_final.reshape(1, d), n_prompt_tiles=npt)
    y_prompt = y_p.reshape(batch, seq, d)
    y_sample = _rows_to_sample(y_s, gb, dec_seq)

    conv_prompt = conv_p[:, SUBLANES - (CONV_WIDTH - 1):, :][None]
    conv_sample = conv_s.reshape(n_sgroups, CONV_WIDTH - 1, gb, inner).transpose(0, 2, 1, 3)
    conv_sample = conv_sample.reshape(dec_batch, CONV_WIDTH - 1, inner)[None]
    v_sample = _rows_to_sample(vn[tp:], gb, dec_seq)[None]
    return (y_prompt, y_sample,
            c_p[None], n_p[:, :, 0, :][None], m_p[:, :, 0, 0][None], conv_prompt,
            c_s[None], n_s[:, :, 0, :][None], m_s[:, :, 0, 0][None], conv_sample,
            v_sample)
```

```python
import functools
import math

import jax
import jax.numpy as jnp
from jax import lax
from jax.experimental import pallas as pl
from jax.experimental.pallas import tpu as pltpu

F32 = jnp.float32
BF16 = jnp.bfloat16
I32 = jnp.int32

LANES = 128
SUBLANES = 8
ROW_TILE = 256
EXPERT_TILE = 512
MXU_WIDTH = 256
VMEM_LIMIT = 56 * 1024 * 1024
CONV_WIDTH = 4
QKV_BLOCK = 4
MLSTM_CHUNK_PROMPT = 256
MLSTM_CHUNK_SAMPLE = 128
CMLP_CHUNK = 128
STAB_INIT = -1e30
N_GATES = 8


def _cparams(*sem):
    return pltpu.CompilerParams(dimension_semantics=sem, vmem_limit_bytes=VMEM_LIMIT)


def _rms(x, g, eps=1e-6):
    return x * lax.rsqrt(jnp.mean(x * x, axis=-1, keepdims=True) + eps) * g


def _silu(x):
    hx = 0.5 * x
    return hx * jnp.tanh(hx) + hx


def _split3(x):
    p0 = x.astype(BF16)
    r = x - p0.astype(F32)
    p1 = r.astype(BF16)
    p2 = (r - p1.astype(F32)).astype(BF16)
    return p0, p1, p2


def _dot(a, b):
    return jnp.dot(a, b, preferred_element_type=F32)


def _dot_nt(a, b):
    return lax.dot_general(a, b, (((1,), (1,)), ((), ())), preferred_element_type=F32)


def _dot_tn(a, b):
    return lax.dot_general(a, b, (((0,), (0,)), ((), ())), preferred_element_type=F32)


def _dot_nt_f32(a, b):
    a0, a1, a2 = _split3(a)
    b0, b1, b2 = _split3(b)
    acc = _dot_nt(a0, b0)
    acc += _dot_nt(a0, b1) + _dot_nt(a1, b0)
    acc += _dot_nt(a1, b1) + _dot_nt(a0, b2) + _dot_nt(a2, b0)
    return acc


def _norm_up_body(xp_ref, xs_ref, g_ref, w_ref, xm_ref, z_ref, *, n_prompt_tiles, inner):
    i = pl.program_id(0)
    x = jnp.where(i < n_prompt_tiles, xp_ref[...], xs_ref[...])
    h = _rms(x, g_ref[...]).astype(BF16)
    u = _dot(h, w_ref[...])
    xm_ref[...] = u[:, :inner]
    z_ref[...] = u[:, inner:].astype(BF16)


def _two_src_specs(n_prompt_tiles, width):
    sp = pl.BlockSpec((ROW_TILE, width), lambda i: (jnp.minimum(i, n_prompt_tiles - 1), 0))
    ss = pl.BlockSpec((ROW_TILE, width), lambda i: (jnp.maximum(i - n_prompt_tiles, 0), 0))
    return sp, ss


def _norm_up(xp, xs, g, w):
    tp, d = xp.shape
    ts = xs.shape[0]
    tt = tp + ts
    n2 = w.shape[1]
    inner = n2 // 2
    npt = tp // ROW_TILE
    sp, ss = _two_src_specs(npt, d)
    return pl.pallas_call(
        functools.partial(_norm_up_body, n_prompt_tiles=npt, inner=inner),
        grid=(tt // ROW_TILE,),
        in_specs=[sp, ss,
                  pl.BlockSpec((1, d), lambda i: (0, 0)),
                  pl.BlockSpec((d, n2), lambda i: (0, 0))],
        out_specs=[pl.BlockSpec((ROW_TILE, inner), lambda i: (i, 0)),
                   pl.BlockSpec((ROW_TILE, inner), lambda i: (i, 0))],
        out_shape=[jax.ShapeDtypeStruct((tt, inner), F32),
                   jax.ShapeDtypeStruct((tt, inner), BF16)],
        compiler_params=_cparams("parallel"),
        name="norm_up",
    )(xp, xs, g, w)


def _conv_qkv_body(xm_ref, st_ref, cw_ref, cb_ref, wqk_ref, wv_ref, wg_ref, bg_ref,
                   q_ref, k_ref, v_ref, xc_ref, gc_ref, bc_ref, st_out_ref,
                   *, row_stride, k_scale):
    t = pl.program_id(1)
    halo = st_ref.shape[1]
    rows, inner = xm_ref.shape

    @pl.when(t == 0)
    def _():
        st_out_ref[0] = st_ref[0]

    x = xm_ref[...]
    xe = jnp.concatenate([st_out_ref[0], x], axis=0)
    acc = x * cw_ref[CONV_WIDTH - 1:CONV_WIDTH, :] + cb_ref[...]
    for j in range(CONV_WIDTH - 1):
        lo = halo - (CONV_WIDTH - 1 - j) * row_stride
        acc = acc + xe[lo:lo + rows, :] * cw_ref[j:j + 1, :]
    st_out_ref[0] = xe[rows:rows + halo, :]
    xc = _silu(acc)
    xc_b = xc.astype(BF16)
    xm_b = x.astype(BF16)
    xc_ref[...] = xc_b

    gacc = jnp.zeros((rows, LANES), F32)
    for c in range(inner // MXU_WIDTH):
        sl = slice(c * MXU_WIDTH, (c + 1) * MXU_WIDTH)
        qk = _dot(xc_b[:, sl], wqk_ref[c])
        vv = _dot(xm_b[:, sl], wv_ref[c])
        qb = qk[:, :MXU_WIDTH].astype(BF16)
        kb = qk[:, MXU_WIDTH:].astype(BF16)
        vb = vv.astype(BF16)
        q_ref[:, sl] = qb
        k_ref[:, sl] = (qk[:, MXU_WIDTH:] * k_scale).astype(BF16)
        v_ref[:, sl] = vb
        gacc += _dot(qb, wg_ref[0, sl, :]) + _dot(kb, wg_ref[1, sl, :]) + _dot(vb, wg_ref[2, sl, :])
    g = gacc + bg_ref[...]
    lane = lax.broadcasted_iota(I32, g.shape, 1)
    logsig = jnp.minimum(g, 0.0) - jnp.log1p(jnp.exp(-jnp.abs(g)))
    gates = jnp.where(lane < N_GATES // 2, g, logsig)
    gc_ref[...] = gates
    bc_ref[...] = jnp.where(lane < N_GATES // 2, gates, _cumsum_rows(gates))


def _conv_qkv(xm, row_block_offset, state, cw, cb, wqk, wv, wg, bg, *, n_groups, tiles_per_group,
              row_stride, k_scale):
    inner = xm.shape[1]
    rows = n_groups * tiles_per_group * ROW_TILE
    halo = state.shape[1]
    nt = tiles_per_group
    row_spec = pl.BlockSpec((ROW_TILE, inner), lambda b, t: (b * nt + t, 0))
    gate_spec = pl.BlockSpec((ROW_TILE, LANES), lambda b, t: (b * nt + t, 0))
    full = lambda a: pl.BlockSpec(a.shape, lambda b, t: (0,) * a.ndim)
    return pl.pallas_call(
        functools.partial(_conv_qkv_body, row_stride=row_stride, k_scale=k_scale),
        grid=(n_groups, nt),
        in_specs=[pl.BlockSpec((ROW_TILE, inner), lambda b, t: (row_block_offset + b * nt + t, 0)),
                  pl.BlockSpec((1, halo, inner), lambda b, t: (b, 0, 0)),
                  full(cw), full(cb), full(wqk), full(wv), full(wg), full(bg)],
        out_specs=[row_spec, row_spec, row_spec, row_spec, gate_spec, gate_spec,
                   pl.BlockSpec((1, halo, inner), lambda b, t: (b, 0, 0))],
        out_shape=[jax.ShapeDtypeStruct((rows, inner), BF16)] * 4
        + [jax.ShapeDtypeStruct((rows, LANES), F32)] * 2
        + [jax.ShapeDtypeStruct((n_groups, halo, inner), F32)],
        compiler_params=_cparams("parallel", "arbitrary"),
        name="conv_qkv",
    )(xm, state, cw, cb, wqk, wv, wg, bg)


def _cumsum_rows(x):
    n = x.shape[0]
    tri = (lax.broadcasted_iota(I32, (n, n), 1) <= lax.broadcasted_iota(I32, (n, n), 0)).astype(BF16)
    p0, p1, p2 = _split3(x)
    return _dot(tri, p0) + _dot(tri, p1) + _dot(tri, p2)


def _cumsum_lanes(x):
    n = x.shape[1]
    tri = (lax.broadcasted_iota(I32, (n, n), 0) <= lax.broadcasted_iota(I32, (n, n), 1)).astype(BF16)
    p0, p1, p2 = _split3(x)
    return _dot(p0, tri) + _dot(p1, tri) + _dot(p2, tri)


def _mlstm_chunk(q, k, v, ig_col, bcum_col, ig_row, bcum_row, c_in, c_out, n_prev, m_prev):
    lq, lk = q.shape[0], k.shape[0]
    ri = lax.broadcasted_iota(I32, (lq, lk), 0)
    ci = lax.broadcasted_iota(I32, (lq, lk), 1)
    bcum_q = bcum_col[:lq]
    log_d = jnp.where(ci <= ri, bcum_q - bcum_row + ig_row, -jnp.inf)
    m_inter = bcum_q + m_prev
    m_t = jnp.maximum(m_inter, jnp.max(log_d, axis=-1, keepdims=True))
    s = _dot_nt(q, k) * jnp.exp(log_d - m_t)
    inter = jnp.exp(m_inter - m_t)
    m_new = m_t[lq - 1:lq, :]
    b_last = bcum_col[lk - 1:lk, :]
    w_col = jnp.exp(b_last - bcum_col + ig_col - m_new)
    decay = jnp.exp(b_last + m_prev - m_new)
    wk = w_col * k.astype(F32)
    c_prev = c_in[...]
    num = _dot(s.astype(BF16), v) + inter * _dot(q, c_prev.astype(BF16))
    c_out[...] = decay * c_prev + _dot_tn(wk.astype(BF16), v)
    den = jnp.sum(s, axis=-1, keepdims=True) + inter * jnp.sum(q.astype(F32) * n_prev, axis=-1, keepdims=True)
    h = num * (1.0 / jnp.maximum(jnp.abs(den), jnp.exp(-m_t)))
    n_new = decay * n_prev + jnp.sum(wk, axis=0, keepdims=True)
    return h, n_new, m_new


def _head_norm(h, w, eps=1e-5):
    mu = jnp.mean(h, axis=-1, keepdims=True)
    var = jnp.mean(jnp.square(h - mu), axis=-1, keepdims=True)
    return (h - mu) * lax.rsqrt(var + eps) * w


def _cell_prompt_body(q_ref, k_ref, v_ref, gc_ref, gr_ref, hw_ref,
                      h_ref, c_ref, n_ref, m_ref, *, heads):
    c = pl.program_id(1)

    @pl.when(c == 0)
    def _():
        c_ref[...] = jnp.zeros(c_ref.shape, F32)
        n_ref[...] = jnp.zeros(n_ref.shape, F32)
        m_ref[...] = jnp.full(m_ref.shape, STAB_INIT, F32)

    gc = gc_ref[...]
    gr = gr_ref[...]
    half = N_GATES // 2
    dh = q_ref.shape[1] // heads
    for hd in range(heads):
        sl = slice(hd * dh, (hd + 1) * dh)
        h, n_new, m_new = _mlstm_chunk(
            q_ref[:, sl], k_ref[:, sl], v_ref[:, sl],
            gc[:, hd:hd + 1], gc[:, hd + half:hd + half + 1],
            gr[hd:hd + 1, :], gr[hd + half:hd + half + 1, :],
            c_ref.at[0, hd], c_ref.at[0, hd], n_ref[0, hd], m_ref[0, hd, :, 0:1])
        h_ref[:, sl] = _head_norm(h, hw_ref[hd]).astype(BF16)
        n_ref[0, hd] = n_new
        m_ref[0, hd] = jnp.broadcast_to(m_new, m_ref.shape[2:])


def _cell_prompt(q, k, v, gc, gr, hn_w, *, batch, heads):
    tp, inner = q.shape
    dh = inner // heads
    seq = tp // batch
    ln = MLSTM_CHUNK_PROMPT
    nc = seq // ln
    qspec = pl.BlockSpec((ln, inner), lambda b, c: (b * nc + c, 0))
    return pl.pallas_call(
        functools.partial(_cell_prompt_body, heads=heads),
        grid=(batch, nc),
        in_specs=[qspec, qspec, qspec,
                  pl.BlockSpec((ln, LANES), lambda b, c: (b * nc + c, 0)),
                  pl.BlockSpec((SUBLANES, ln), lambda b, c: (0, b * nc + c)),
                  pl.BlockSpec((heads, 1, dh), lambda b, c: (0, 0, 0))],
        out_specs=[qspec,
                   pl.BlockSpec((1, heads, dh, dh), lambda b, c: (b, 0, 0, 0)),
                   pl.BlockSpec((1, heads, 1, dh), lambda b, c: (b, 0, 0, 0)),
                   pl.BlockSpec((1, heads, 1, LANES), lambda b, c: (b, 0, 0, 0))],
        out_shape=[jax.ShapeDtypeStruct((tp, inner), BF16),
                   jax.ShapeDtypeStruct((batch, heads, dh, dh), F32),
                   jax.ShapeDtypeStruct((batch, heads, 1, dh), F32),
                   jax.ShapeDtypeStruct((batch, heads, 1, LANES), F32)],
        compiler_params=_cparams("parallel", "arbitrary"),
        name="mlstm_cell_prompt",
    )(q, k, v, gc, gr, hn_w)


STATE_RING = 3


def _cell_sample_body(q_ref, k_ref, v_ref, gc_ref, gr_ref, hw_ref, c0_hbm, n0_ref, m0_ref,
                      h_ref, c_ref, n_ref, m_ref, c0_buf, c0_sem, *, heads, valid):
    b = pl.program_id(0)
    nb = pl.num_programs(0)

    def fetch(step):
        slot = step % STATE_RING
        return pltpu.make_async_copy(c0_hbm.at[step], c0_buf.at[slot], c0_sem.at[slot])

    @pl.when(b == 0)
    def _():
        for step in range(STATE_RING - 1):
            @pl.when(step < nb)
            def _():
                fetch(step).start()

    @pl.when(b + STATE_RING - 1 < nb)
    def _():
        fetch(b + STATE_RING - 1).start()

    fetch(b).wait()
    c0_ref = c0_buf.at[b % STATE_RING]
    ln = MLSTM_CHUNK_SAMPLE
    rows = q_ref.shape[1]
    dh = q_ref.shape[2] // heads
    half = N_GATES // 2
    gc = gc_ref[0]
    gr = gr_ref[0]
    row_c = lax.broadcasted_iota(I32, gc.shape, 0)
    lane_c = lax.broadcasted_iota(I32, gc.shape, 1)
    gc = jnp.where(row_c < valid, gc, jnp.where(lane_c < half, -jnp.inf, 0.0))
    row_r = lax.broadcasted_iota(I32, gr.shape, 0)
    lane_r = lax.broadcasted_iota(I32, gr.shape, 1)
    gr = jnp.where(lane_r < valid, gr, jnp.where(row_r < half, -jnp.inf, 0.0))
    bc = _cumsum_rows(jnp.where(lane_c < half, 0.0, gc))
    br = _cumsum_lanes(jnp.where(row_r < half, 0.0, gr))
    pad = jnp.zeros((ln - rows, dh), F32)

    def padded(ref, sl):
        return jnp.concatenate([ref[0, :, sl].astype(F32), pad], axis=0).astype(BF16)

    for hd in range(heads):
        sl = slice(hd * dh, (hd + 1) * dh)
        h, n_new, m_new = _mlstm_chunk(
            q_ref[0, :, sl], padded(k_ref, sl), padded(v_ref, sl),
            gc[:, hd:hd + 1], bc[:, hd + half:hd + half + 1],
            gr[hd:hd + 1, :], br[hd + half:hd + half + 1, :],
            c0_ref.at[hd], c_ref.at[0, hd], n0_ref[0, hd], m0_ref[0, hd, :, 0:1])
        h_ref[0, :, sl] = _head_norm(h, hw_ref[hd]).astype(BF16)
        n_ref[0, hd] = n_new
        m_ref[0, hd] = jnp.broadcast_to(m_new, m_ref.shape[2:])


def _cell_sample(q, k, v, gc, gr, hn_w, c0, n0, m0, *, heads, valid, group_batch):
    n_groups, rows, wide = q.shape
    inner = wide // group_batch
    dh = inner // heads
    nb = n_groups * group_batch
    ln = MLSTM_CHUNK_SAMPLE
    qspec = pl.BlockSpec((1, rows, inner), lambda b: (b // group_batch, 0, b % group_batch))
    cspec = pl.BlockSpec((1, heads, dh, dh), lambda b: (b, 0, 0, 0))
    nspec = pl.BlockSpec((1, heads, 1, dh), lambda b: (b, 0, 0, 0))
    mspec = pl.BlockSpec((1, heads, 1, LANES), lambda b: (b, 0, 0, 0))
    return pl.pallas_call(
        functools.partial(_cell_sample_body, heads=heads, valid=valid),
        grid=(nb,),
        in_specs=[qspec, qspec, qspec,
                  pl.BlockSpec((1, ln, LANES), lambda b: (b, 0, 0)),
                  pl.BlockSpec((1, SUBLANES, ln), lambda b: (b, 0, 0)),
                  pl.BlockSpec((heads, 1, dh), lambda b: (0, 0, 0)),
                  pl.BlockSpec(memory_space=pl.ANY), nspec, mspec],
        out_specs=[qspec, cspec, nspec, mspec],
        out_shape=[jax.ShapeDtypeStruct(q.shape, BF16),
                   jax.ShapeDtypeStruct((nb, heads, dh, dh), F32),
                   jax.ShapeDtypeStruct((nb, heads, 1, dh), F32),
                   jax.ShapeDtypeStruct((nb, heads, 1, LANES), F32)],
        scratch_shapes=[pltpu.VMEM((STATE_RING, heads, dh, dh), F32),
                        pltpu.SemaphoreType.DMA((STATE_RING,))],
        compiler_params=_cparams("arbitrary"),
        name="mlstm_cell_sample",
    )(q, k, v, gc, gr, hn_w, c0, n0, m0)


def _gate_down_body(hp_ref, hs_ref, cp_ref, cs_ref, z_ref, skip_ref, w_ref, xp_ref, xs_ref, o_ref,
                    *, n_prompt_tiles):
    i = pl.program_id(0)
    is_p = i < n_prompt_tiles
    hn = jnp.where(is_p, hp_ref[...], hs_ref[...]).astype(F32)
    xc = jnp.where(is_p, cp_ref[...], cs_ref[...]).astype(F32)
    xres = jnp.where(is_p, xp_ref[...], xs_ref[...])
    a = (hn + skip_ref[...] * xc) * _silu(z_ref[...].astype(F32))
    o_ref[...] = xres + _dot(a.astype(BF16), w_ref[...])


def _gate_down(hn_p, hn_s, xc_p, xc_s, z, skip, w, xp, xs):
    tp, inner = hn_p.shape
    tt = z.shape[0]
    d = w.shape[1]
    npt = tp // ROW_TILE
    ip, is_ = _two_src_specs(npt, inner)
    dp, ds = _two_src_specs(npt, d)
    return pl.pallas_call(
        functools.partial(_gate_down_body, n_prompt_tiles=npt),
        grid=(tt // ROW_TILE,),
        in_specs=[ip, is_, ip, is_,
                  pl.BlockSpec((ROW_TILE, inner), lambda i: (i, 0)),
                  pl.BlockSpec((1, inner), lambda i: (0, 0)),
                  pl.BlockSpec((inner, d), lambda i: (0, 0)),
                  dp, ds],
        out_specs=pl.BlockSpec((ROW_TILE, d), lambda i: (i, 0)),
        out_shape=jax.ShapeDtypeStruct((tt, d), F32),
        compiler_params=_cparams("parallel"),
        name="gate_down",
    )(hn_p, hn_s, xc_p, xc_s, z, skip, w, xp, xs)


def _cmlp_in_body(x_ref, g_ref, w_ref, b_ref, lw_ref, lb_ref, u_ref, v_ref, *, half):
    h = _rms(x_ref[...], g_ref[...]).astype(BF16)
    y = _dot(h, w_ref[...]) + b_ref[...]
    c = math.sqrt(2.0 / math.pi)
    hy = 0.5 * y
    zz = hy * jnp.tanh(y * (c + (0.044715 * c) * (y * y))) + hy
    u_ref[...] = zz[:, :half].astype(BF16)
    v = zz[:, half:]
    mu = jnp.mean(v, axis=-1, keepdims=True)
    var = jnp.mean(jnp.square(v - mu), axis=-1, keepdims=True)
    v_ref[...] = (v - mu) * lax.rsqrt(var + 1e-5) * lw_ref[...] + lb_ref[...]


def _cmlp_in(x, g, w, b, lw, lb):
    tt, d = x.shape
    n2 = w.shape[1]
    half = n2 // 2
    vec = lambda n: pl.BlockSpec((1, n), lambda i: (0, 0))
    return pl.pallas_call(
        functools.partial(_cmlp_in_body, half=half),
        grid=(tt // ROW_TILE,),
        in_specs=[pl.BlockSpec((ROW_TILE, d), lambda i: (i, 0)), vec(d),
                  pl.BlockSpec((d, n2), lambda i: (0, 0)), vec(n2), vec(half), vec(half)],
        out_specs=[pl.BlockSpec((ROW_TILE, half), lambda i: (i, 0))] * 2,
        out_shape=[jax.ShapeDtypeStruct((tt, half), BF16), jax.ShapeDtypeStruct((tt, half), F32)],
        compiler_params=_cparams("parallel"),
        name="cmlp_in",
    )(x, g, w, b, lw, lb)


def _cmlp_mix_out_body(u_ref, v_ref, mix_ref, bias_ref, w_ref, bo_ref, x_ref, o_ref, *, groups):
    half = u_ref.shape[1]
    gd = half // groups
    pieces = []
    for g in range(groups):
        sl = slice(g * gd, (g + 1) * gd)
        mixed = _dot(mix_ref[0, g], v_ref[:, sl].astype(BF16)) + bias_ref[0, :, g:g + 1]
        pieces.append((u_ref[:, sl].astype(F32) * mixed).astype(BF16))
    a = jnp.concatenate(pieces, axis=1)
    o_ref[...] = x_ref[...] + _dot(a, w_ref[...]) + bo_ref[...]


def _cmlp_mix_out(u, v, mix, bias, w, bo, x, *, n_prompt_tiles):
    tt, half = u.shape
    d = w.shape[1]
    groups = mix.shape[1]
    kind = lambda i: jnp.where(i < n_prompt_tiles, 0, 1)
    return pl.pallas_call(
        functools.partial(_cmlp_mix_out_body, groups=groups),
        grid=(tt // ROW_TILE,),
        in_specs=[pl.BlockSpec((ROW_TILE, half), lambda i: (i, 0)),
                  pl.BlockSpec((ROW_TILE, half), lambda i: (i, 0)),
                  pl.BlockSpec((1, groups, ROW_TILE, ROW_TILE), lambda i: (kind(i), 0, 0, 0)),
                  pl.BlockSpec((1, ROW_TILE, groups), lambda i: (kind(i), 0, 0)),
                  pl.BlockSpec((half, d), lambda i: (0, 0)),
                  pl.BlockSpec((1, d), lambda i: (0, 0)),
                  pl.BlockSpec((ROW_TILE, d), lambda i: (i, 0))],
        out_specs=pl.BlockSpec((ROW_TILE, d), lambda i: (i, 0)),
        out_shape=jax.ShapeDtypeStruct((tt, d), F32),
        compiler_params=_cparams("parallel"),
        name="cmlp_mix_out",
    )(u, v, mix, bias, w, bo, x)


GROUP_ROW_OFFSET = SUBLANES


def _first_argmax_rows(x, n):
    row = lax.broadcasted_iota(I32, x.shape, 0)
    mx = jnp.max(x, axis=0, keepdims=True)
    idx = jnp.min(jnp.where(x == mx, row, n), axis=0, keepdims=True)
    return idx, mx


def _router_body(x_ref, g_ref, wt_ref, bt_ref, route_ref, cnt_ref, *, n_groups, per_group):
    i = pl.program_id(0)
    n_exp = n_groups * per_group

    @pl.when(i == 0)
    def _():
        cnt_ref[...] = jnp.zeros(cnt_ref.shape, F32)

    h = _rms(x_ref[...], g_ref[...])
    logits = _dot_nt_f32(wt_ref[...], h) + bt_ref[:, 0:1]
    tokens = logits.shape[1]
    gl = logits[0:n_groups, :]
    gmax = jnp.max(gl, axis=0, keepdims=True)
    ge = jnp.exp(gl - gmax)
    gp = ge / jnp.sum(ge, axis=0, keepdims=True)
    g_idx, g_w = _first_argmax_rows(gp, n_groups)
    e_sel = jnp.zeros((per_group, tokens), F32)
    for g in range(n_groups):
        lo = GROUP_ROW_OFFSET + g * per_group
        e_sel = e_sel + jnp.where(g_idx == g, logits[lo:lo + per_group, :], 0.0)
    i1, m1 = _first_argmax_rows(e_sel, per_group)
    row = lax.broadcasted_iota(I32, e_sel.shape, 0)
    i2, m2 = _first_argmax_rows(jnp.where(row == i1, -jnp.inf, e_sel), per_group)
    t = jnp.exp(m2 - m1)
    w1 = (1.0 / (1.0 + t)) * g_w
    w2 = (t / (1.0 + t)) * g_w
    e1 = g_idx * per_group + i1
    e2 = g_idx * per_group + i2

    erow = lax.broadcasted_iota(I32, (n_exp, tokens), 0)
    oh1 = erow == e1
    oh2 = erow == e2
    onehot = jnp.where(oh1 | oh2, 1.0, 0.0)
    strict_upper = (lax.broadcasted_iota(I32, (tokens, tokens), 0)
                    < lax.broadcasted_iota(I32, (tokens, tokens), 1)).astype(BF16)
    before = _dot(onehot.astype(BF16), strict_upper) + cnt_ref[:, 0:1]
    r1 = jnp.sum(jnp.where(oh1, before, 0.0), axis=0, keepdims=True)
    r2 = jnp.sum(jnp.where(oh2, before, 0.0), axis=0, keepdims=True)
    cnt_ref[...] = cnt_ref[...] + jnp.sum(onehot, axis=1, keepdims=True)
    zero = jnp.zeros_like(w1)
    route_ref[...] = jnp.concatenate(
        [e1.astype(F32), e2.astype(F32), r1, r2, w1, w2, zero, zero], axis=0)


def _wide_tile(n_rows, max_tiles):
    n = n_rows // ROW_TILE
    return ROW_TILE * max(k for k in range(1, max_tiles + 1) if n % k == 0)


def _router(x, g, wt, bt, *, n_groups, per_group):
    tt, d = x.shape
    n_exp = n_groups * per_group
    rt = _wide_tile(tt, 2)
    return pl.pallas_call(
        functools.partial(_router_body, n_groups=n_groups, per_group=per_group),
        grid=(tt // rt,),
        in_specs=[pl.BlockSpec((rt, d), lambda i: (i, 0)),
                  pl.BlockSpec((1, d), lambda i: (0, 0)),
                  pl.BlockSpec(wt.shape, lambda i: (0, 0)),
                  pl.BlockSpec(bt.shape, lambda i: (0, 0))],
        out_specs=[pl.BlockSpec((SUBLANES, rt), lambda i: (0, i)),
                   pl.BlockSpec((n_exp, LANES), lambda i: (0, 0))],
        out_shape=[jax.ShapeDtypeStruct((SUBLANES, tt), F32),
                   jax.ShapeDtypeStruct((n_exp, LANES), F32)],
        compiler_params=_cparams("arbitrary"),
        name="moe_router",
    )(x, g, wt, bt)


def _positions_body(route_ref, off_ref, pos_ref):
    r = route_ref[...]
    n_exp = off_ref.shape[0]
    tokens = r.shape[1]
    erow = lax.broadcasted_iota(I32, (n_exp, tokens), 0)
    off = off_ref[:, 0:1]
    out = []
    for k in range(2):
        e = r[k:k + 1, :].astype(I32)
        base = jnp.sum(jnp.where(erow == e, off, 0.0), axis=0, keepdims=True)
        out.append((base + r[2 + k:3 + k, :]).astype(I32))
    pos_ref[...] = jnp.concatenate(out, axis=0)


def _positions(route, off):
    tt = route.shape[1]
    pt = _wide_tile(tt, 16)
    return pl.pallas_call(
        _positions_body,
        grid=(tt // pt,),
        in_specs=[pl.BlockSpec((SUBLANES, pt), lambda i: (0, i)),
                  pl.BlockSpec(off.shape, lambda i: (0, 0))],
        out_specs=pl.BlockSpec((2, pt), lambda i: (0, i)),
        out_shape=jax.ShapeDtypeStruct((2, tt), I32),
        compiler_params=_cparams("parallel"),
        name="moe_positions",
    )(route, off)


def _for_each_row(n_rows, fn):
    def group(g, carry):
        start = pl.multiple_of(g * SUBLANES, SUBLANES)
        for u in range(SUBLANES):
            fn(start, u)
        return carry

    lax.fori_loop(0, n_rows // SUBLANES, group, 0, unroll=2)


def _tile_row(ref, group_start, u):
    return ref.at[pl.ds(group_start, SUBLANES)].at[pl.ds(u, 1)]


def _dispatch_body(pos_ref, x_ref, xs_hbm, sem, *, n_tokens):
    base = pl.program_id(0) * ROW_TILE

    def copies(start, u):
        return [pltpu.make_async_copy(_tile_row(x_ref, start, u),
                                      xs_hbm.at[pl.ds(pos_ref[k * n_tokens + base + start + u], 1)], sem)
                for k in range(2)]

    _for_each_row(ROW_TILE, lambda start, u: [cp.start() for cp in copies(start, u)])
    _for_each_row(ROW_TILE, lambda start, u: [cp.wait() for cp in copies(start, u)])


def _dispatch(pos_flat, x):
    tt, d = x.shape
    return pl.pallas_call(
        functools.partial(_dispatch_body, n_tokens=tt),
        grid_spec=pltpu.PrefetchScalarGridSpec(
            num_scalar_prefetch=1,
            grid=(tt // ROW_TILE,),
            in_specs=[pl.BlockSpec((ROW_TILE, d), lambda i, p: (i, 0))],
            out_specs=pl.BlockSpec(memory_space=pl.ANY),
            scratch_shapes=[pltpu.SemaphoreType.DMA]),
        out_shape=jax.ShapeDtypeStruct((2 * tt, d), F32),
        compiler_params=_cparams("arbitrary"),
        name="moe_dispatch",
    )(pos_flat, x)


def _experts_body(tile_ref, exp_ref, lo_ref, hi_ref, nwork_ref,
                  xs_ref, g_ref, wg_ref, wu_ref, wd_ref, o_ref, wg_b, wu_b, wd_b):
    w = pl.program_id(0)
    prev = jnp.maximum(w - 1, 0)
    first = jnp.logical_or(w == 0, tile_ref[w] != tile_ref[prev])
    new_expert = jnp.logical_or(w == 0, exp_ref[w] != exp_ref[prev])

    @pl.when(w < nwork_ref[0])
    def _():
        @pl.when(new_expert)
        def _():
            wg_b[...] = wg_ref[0].astype(BF16)
            wu_b[...] = wu_ref[0].astype(BF16)
            wd_b[...] = wd_ref[0].astype(BF16)

        xb = _rms(xs_ref[...], g_ref[...]).astype(BF16)
        a = _dot(xb, wg_b[...])
        b = _dot(xb, wu_b[...])
        hg = (_silu(a) * b).astype(BF16)
        y = _dot(hg, wd_b[...])
        lo, hi = lo_ref[w], hi_ref[w]
        whole = jnp.logical_and(lo == 0, hi == EXPERT_TILE)
        row = lax.broadcasted_iota(I32, y.shape, 0)
        mine = (row >= lo) & (row < hi)

        @pl.when(whole)
        def _():
            o_ref[...] = y

        @pl.when(jnp.logical_and(first, jnp.logical_not(whole)))
        def _():
            o_ref[...] = jnp.where(mine, y, 0.0)

        @pl.when(jnp.logical_not(first))
        def _():
            o_ref[...] = jnp.where(mine, y, o_ref[...])


def _experts(meta, xs, g, wg, wu, wd):
    rows, d = xs.shape
    n_work = meta[0].shape[0]
    f = wg.shape[2]
    return pl.pallas_call(
        _experts_body,
        grid_spec=pltpu.PrefetchScalarGridSpec(
            num_scalar_prefetch=5,
            grid=(n_work,),
            in_specs=[pl.BlockSpec((EXPERT_TILE, d), lambda w, tl, ex, lo, hi, nw: (tl[w], 0)),
                      pl.BlockSpec((1, d), lambda w, tl, ex, lo, hi, nw: (0, 0)),
                      pl.BlockSpec((1, d, f), lambda w, tl, ex, lo, hi, nw: (ex[w], 0, 0)),
                      pl.BlockSpec((1, d, f), lambda w, tl, ex, lo, hi, nw: (ex[w], 0, 0)),
                      pl.BlockSpec((1, f, d), lambda w, tl, ex, lo, hi, nw: (ex[w], 0, 0))],
            out_specs=pl.BlockSpec((EXPERT_TILE, d), lambda w, tl, ex, lo, hi, nw: (tl[w], 0)),
            scratch_shapes=[pltpu.VMEM((d, f), BF16), pltpu.VMEM((d, f), BF16), pltpu.VMEM((f, d), BF16)]),
        out_shape=jax.ShapeDtypeStruct((rows, d), F32),
        compiler_params=_cparams("arbitrary"),
        name="moe_experts",
    )(*meta, xs, g, wg, wu, wd)


def _combine_rows(pos_ref, x_ref, route_ref, ys_hbm, buf, sem, n_tokens):
    base = pl.program_id(0) * ROW_TILE

    def copies(start, u):
        return [pltpu.make_async_copy(ys_hbm.at[pl.ds(pos_ref[k * n_tokens + base + start + u], 1)],
                                      _tile_row(buf.at[k], start, u), sem)
                for k in range(2)]

    _for_each_row(ROW_TILE, lambda start, u: [cp.start() for cp in copies(start, u)])
    r = route_ref[...]
    rt = jnp.concatenate([r, jnp.zeros((LANES - SUBLANES, r.shape[1]), F32)], axis=0).T
    _for_each_row(ROW_TILE, lambda start, u: [cp.wait() for cp in copies(start, u)])
    return x_ref[...] + rt[:, 4:5] * buf[0] + rt[:, 5:6] * buf[1]


def _combine_body(pos_ref, x_ref, route_ref, ys_hbm, o_ref, buf, sem, *, n_tokens):
    o_ref[...] = _combine_rows(pos_ref, x_ref, route_ref, ys_hbm, buf, sem, n_tokens)


def _combine_norm_body(pos_ref, x_ref, route_ref, g_ref, ys_hbm, op_ref, os_ref, buf, sem,
                       *, n_tokens, n_prompt_tiles):
    y = _rms(_combine_rows(pos_ref, x_ref, route_ref, ys_hbm, buf, sem, n_tokens), g_ref[...])
    is_prompt = pl.program_id(0) < n_prompt_tiles

    @pl.when(is_prompt)
    def _():
        op_ref[...] = y

    @pl.when(jnp.logical_not(is_prompt))
    def _():
        os_ref[...] = y


def _combine(pos_flat, x, route, ys, final_g=None, n_prompt_tiles=None):
    tt, d = x.shape
    in_specs = [pl.BlockSpec((ROW_TILE, d), lambda i, p: (i, 0)),
                pl.BlockSpec((SUBLANES, ROW_TILE), lambda i, p: (0, i))]
    args = [pos_flat, x, route]
    if final_g is None:
        body = functools.partial(_combine_body, n_tokens=tt)
        out_specs = pl.BlockSpec((ROW_TILE, d), lambda i, p: (i, 0))
        out_shape = jax.ShapeDtypeStruct((tt, d), F32)
    else:
        npt = n_prompt_tiles
        body = functools.partial(_combine_norm_body, n_tokens=tt, n_prompt_tiles=npt)
        in_specs.append(pl.BlockSpec((1, d), lambda i, p: (0, 0)))
        args.append(final_g)
        out_specs = [pl.BlockSpec((ROW_TILE, d), lambda i, p: (jnp.minimum(i, npt - 1), 0)),
                     pl.BlockSpec((ROW_TILE, d), lambda i, p: (jnp.maximum(i - npt, 0), 0))]
        out_shape = [jax.ShapeDtypeStruct((npt * ROW_TILE, d), F32),
                     jax.ShapeDtypeStruct((tt - npt * ROW_TILE, d), F32)]
    in_specs.append(pl.BlockSpec(memory_space=pl.ANY))
    args.append(ys)
    return pl.pallas_call(
        body,
        grid_spec=pltpu.PrefetchScalarGridSpec(
            num_scalar_prefetch=1,
            grid=(tt // ROW_TILE,),
            in_specs=in_specs,
            out_specs=out_specs,
            scratch_shapes=[pltpu.VMEM((2, ROW_TILE, d), F32), pltpu.SemaphoreType.DMA]),
        out_shape=out_shape,
        compiler_params=_cparams("arbitrary"),
        name="moe_combine",
    )(*args)


def _work_items(counts, n_rows):
    n_exp = counts.shape[0]
    n_tiles = n_rows // EXPERT_TILE
    n_work = n_tiles + n_exp - 1
    ids = jnp.arange(n_exp, dtype=I32)
    lower = ids[None, :] <= ids[:, None]

    def cumsum(v):
        return jnp.sum(jnp.where(lower, v[None, :], 0), axis=1)

    end = cumsum(counts)
    off = end - counts
    first_tile = off // EXPERT_TILE
    last_tile = (end - 1) // EXPERT_TILE
    per_exp = jnp.where(counts > 0, last_tile - first_tile + 1, 0)
    wend = cumsum(per_exp)
    wstart = wend - per_exp
    total = jnp.sum(per_exp)
    w = jnp.arange(n_work, dtype=I32)
    e = jnp.minimum(jnp.sum((wend[None, :] <= w[:, None]).astype(I32), axis=1), n_exp - 1)
    sel = e[:, None] == ids[None, :]

    def pick(v):
        return jnp.sum(jnp.where(sel, v[None, :], 0), axis=1)

    tile = pick(first_tile) + (w - pick(wstart))
    lo = jnp.maximum(pick(off), tile * EXPERT_TILE) - tile * EXPERT_TILE
    hi = jnp.minimum(pick(end), (tile + 1) * EXPERT_TILE) - tile * EXPERT_TILE
    live = w < total
    tile = jnp.where(live, tile, n_tiles - 1)
    lo = jnp.where(live, lo, 0)
    hi = jnp.where(live, hi, 0)
    return off, (tile.astype(I32), e.astype(I32), lo.astype(I32), hi.astype(I32),
                 total.reshape(1).astype(I32))


def _moe(x, g, layer, w_rg, b_rg, w_re, b_re, w_gate, w_up, w_down, final_g=None, n_prompt_tiles=None):
    tt, d = x.shape
    n_groups, per_group = w_re.shape[0], w_re.shape[2]
    n_exp = n_groups * per_group
    f = w_gate.shape[-1]
    pad_rows = jnp.zeros((GROUP_ROW_OFFSET - n_groups, d), F32)
    wt = jnp.concatenate([w_rg.T, pad_rows, jnp.transpose(w_re, (0, 2, 1)).reshape(n_exp, d)], axis=0)
    bt = jnp.concatenate([b_rg, jnp.zeros((GROUP_ROW_OFFSET - n_groups,), F32), b_re.reshape(n_exp)])
    bt = jnp.broadcast_to(bt[:, None], (wt.shape[0], LANES))
    g2 = g.reshape(1, d)

    route, cnt = _router(x, g2, wt, bt, n_groups=n_groups, per_group=per_group)
    counts = cnt[:, 0].astype(I32)
    off, (tile, exp, lo, hi, total) = _work_items(counts, 2 * tt)
    off_b = jnp.broadcast_to(off.astype(F32)[:, None], (n_exp, LANES))
    pos = _positions(route, off_b).reshape(2 * tt)
    xs = _dispatch(pos, x)
    ys = _experts((tile, exp + layer * n_exp, lo, hi, total), xs, g2,
                  w_gate.reshape(-1, d, f), w_up.reshape(-1, d, f), w_down.reshape(-1, f, d))
    return _combine(pos, x, route, ys, final_g, n_prompt_tiles)


def _block_diag_tiles(w, tile):
    nb, c, _ = w.shape
    rows = w.reshape(nb * c // tile, tile, c)
    blk = jnp.arange(tile, dtype=I32) // c
    return jnp.where(blk[:, None] == blk[None, :], jnp.tile(rows, (1, 1, tile // c)), 0.0)


def _sample_to_rows(x, group_batch):
    db, s, w = x.shape
    return x.reshape(db // group_batch, group_batch, s, w).transpose(0, 2, 1, 3).reshape(db * s, w)


def _rows_to_sample(x, group_batch, s):
    rows, w = x.shape
    db = rows // s
    return x.reshape(db // group_batch, s, group_batch, w).transpose(0, 2, 1, 3).reshape(db, s, w)


def kernel(x_prompt, x_sample, state_mlstm_C, state_mlstm_n, state_mlstm_m, state_mlstm_conv,
           norm_mix, norm_ffn, norm_final,
           mlstm_w_up, mlstm_conv_w, mlstm_conv_b, mlstm_w_q, mlstm_w_k, mlstm_w_v,
           mlstm_w_ig, mlstm_b_ig, mlstm_w_fg, mlstm_b_fg, mlstm_skip, mlstm_hn_w, mlstm_w_down,
           cmlp_w_in, cmlp_b_in, cmlp_ln_w, cmlp_ln_b, cmlp_w_s, cmlp_b_s, cmlp_w_out, cmlp_b_out,
           moe_w_rg, moe_b_rg, moe_w_re, moe_b_re, moe_w_gate, moe_w_up, moe_w_down):
    batch, seq, d = x_prompt.shape
    dec_batch, dec_seq, _ = x_sample.shape
    heads = state_mlstm_C.shape[2]
    dh = state_mlstm_C.shape[3]
    inner = heads * dh
    tp, ts = batch * seq, dec_batch * dec_seq
    npt = tp // ROW_TILE
    gb = ROW_TILE // dec_seq
    n_sgroups = dec_batch // gb
    half_g = N_GATES // 2
    assert heads == half_g and tp % ROW_TILE == 0 and ts % ROW_TILE == 0
    assert ROW_TILE % dec_seq == 0 and dec_batch % gb == 0 and seq % MLSTM_CHUNK_PROMPT == 0
    assert seq % ROW_TILE == 0 and ROW_TILE % CMLP_CHUNK == 0 and dec_seq <= SUBLANES
    assert MLSTM_CHUNK_PROMPT == ROW_TILE

    xp = x_prompt.reshape(tp, d)
    xs = _sample_to_rows(x_sample, gb)

    xm, z = _norm_up(xp, xs, norm_mix[0].reshape(1, d), mlstm_w_up[0].astype(BF16))

    wq = _block_diag_tiles(mlstm_w_q[0], MXU_WIDTH)
    wk = _block_diag_tiles(mlstm_w_k[0], MXU_WIDTH)
    wqk = jnp.concatenate([wq, wk], axis=2).astype(BF16)
    wv = _block_diag_tiles(mlstm_w_v[0], MXU_WIDTH).astype(BF16)
    wg = jnp.concatenate([mlstm_w_ig[0], mlstm_w_fg[0]], axis=1)
    wg = jnp.pad(wg, ((0, 0), (0, LANES - N_GATES))).reshape(3, inner, LANES).astype(BF16)
    bg = jnp.pad(jnp.concatenate([mlstm_b_ig[0], mlstm_b_fg[0]]), (0, LANES - N_GATES)).reshape(1, LANES)
    cw, cb = mlstm_conv_w[0], mlstm_conv_b[0].reshape(1, inner)
    k_scale = float(dh) ** -0.5

    conv0_p = jnp.zeros((batch, SUBLANES, inner), F32)
    q_p, k_p, v_p, xc_p, _, bc_p, conv_p = _conv_qkv(
        xm, 0, conv0_p, cw, cb, wqk, wv, wg, bg,
        n_groups=batch, tiles_per_group=seq // ROW_TILE, row_stride=1, k_scale=k_scale)
    halo_s = (CONV_WIDTH - 1) * gb
    conv0_s = state_mlstm_conv[0].reshape(n_sgroups, gb, CONV_WIDTH - 1, inner)
    conv0_s = conv0_s.transpose(0, 2, 1, 3).reshape(n_sgroups, halo_s, inner)
    q_s, k_s, v_s, xc_s, gc_s, _, conv_s = _conv_qkv(
        xm, npt, conv0_s, cw, cb, wqk, wv, wg, bg,
        n_groups=n_sgroups, tiles_per_group=1, row_stride=gb, k_scale=k_scale)

    hn_w = mlstm_hn_w[0].reshape(heads, 1, dh)
    br_p = jnp.transpose(bc_p[:, :SUBLANES])
    hn_p, c_p, n_p, m_p = _cell_prompt(q_p, k_p, v_p, bc_p, br_p, hn_w, batch=batch, heads=heads)

    def pad_steps(a):
        a = a.reshape(n_sgroups, dec_seq, gb * inner)
        return jnp.pad(a, ((0, 0), (0, SUBLANES - dec_seq), (0, 0)))

    ln_s = MLSTM_CHUNK_SAMPLE
    gcs = gc_s.reshape(n_sgroups, dec_seq, gb, LANES).transpose(0, 2, 1, 3).reshape(dec_batch, dec_seq, LANES)
    gcs = jnp.pad(gcs, ((0, 0), (0, ln_s - dec_seq), (0, 0)))
    grs = jnp.transpose(gcs[:, :, :SUBLANES], (0, 2, 1))
    m0 = jnp.broadcast_to(state_mlstm_m[0][:, :, None, None], (dec_batch, heads, 1, LANES))
    hn_s, c_s, n_s, m_s = _cell_sample(
        pad_steps(q_s), pad_steps(k_s), pad_steps(v_s), gcs, grs, hn_w,
        state_mlstm_C[0], state_mlstm_n[0].reshape(dec_batch, heads, 1, dh), m0,
        heads=heads, valid=dec_seq, group_batch=gb)
    hn_s = hn_s[:, :dec_seq].reshape(ts, inner)

    x1 = _gate_down(hn_p, hn_s, xc_p, xc_s, z, mlstm_skip[0].reshape(1, inner),
                    mlstm_w_down[0].astype(BF16), xp, xs)
    x2 = _moe(x1, norm_ffn[0], 0, moe_w_rg[0], moe_b_rg[0], moe_w_re[0], moe_b_re[0],
              moe_w_gate, moe_w_up, moe_w_down)

    half = cmlp_w_in.shape[2] // 2
    groups = cmlp_w_s.shape[1]
    u, vn = _cmlp_in(x2, norm_mix[1].reshape(1, d), cmlp_w_in[0].astype(BF16),
                     cmlp_b_in[0].reshape(1, 2 * half), cmlp_ln_w[0].reshape(1, half),
                     cmlp_ln_b[0].reshape(1, half))
    causal = jnp.tril(jnp.ones((CMLP_CHUNK, CMLP_CHUNK), dtype=bool))
    ws = jnp.where(causal, cmlp_w_s[0], 0.0)
    reps = ROW_TILE // CMLP_CHUNK
    rid = jnp.arange(ROW_TILE, dtype=I32)
    same_chunk = (rid[:, None] // CMLP_CHUNK) == (rid[None, :] // CMLP_CHUNK)
    same_seq = (rid[:, None] % gb) == (rid[None, :] % gb)
    mix_p = jnp.where(same_chunk, jnp.tile(ws, (1, reps, reps)), 0.0)
    ws_s = ws[:, :dec_seq, :dec_seq]
    mix_s = jnp.where(same_seq, jnp.repeat(jnp.repeat(ws_s, gb, axis=1), gb, axis=2), 0.0)
    mix = jnp.stack([mix_p, mix_s]).astype(BF16)
    bias_t = jnp.transpose(cmlp_b_s[0])
    bias = jnp.stack([jnp.tile(bias_t, (reps, 1)), jnp.repeat(bias_t[:dec_seq], gb, axis=0)])
    x3 = _cmlp_mix_out(u, vn, mix, bias, cmlp_w_out[0].astype(BF16), cmlp_b_out[0].reshape(1, d), x2,
                       n_prompt_tiles=npt)
    y_p, y_s = _moe(x3, norm_ffn[1], 1, moe_w_rg[1], moe_b_rg[1], moe_w_re[1], moe_b_re[1],
                    moe_w_gate, moe_w_up, moe_w_down,
                    final_g=norm_final.reshape(1, d), n_prompt_tiles=npt)
    y_prompt = y_p.reshape(batch, seq, d)
    y_sample = _rows_to_sample(y_s, gb, dec_seq)

    conv_prompt = conv_p[:, SUBLANES - (CONV_WIDTH - 1):, :][None]
    conv_sample = conv_s.reshape(n_sgroups, CONV_WIDTH - 1, gb, inner).transpose(0, 2, 1, 3)
    conv_sample = conv_sample.reshape(dec_batch, CONV_WIDTH - 1, inner)[None]
    v_sample = _rows_to_sample(vn[tp:], gb, dec_seq)[None]
    return (y_prompt, y_sample,
            c_p[None], n_p[:, :, 0, :][None], m_p[:, :, 0, 0][None], conv_prompt,
            c_s[None], n_s[:, :, 0, :][None], m_s[:, :, 0, 0][None], conv_sample,
            v_sample)
```

```python
import functools
import math

import jax
import jax.numpy as jnp
from jax import lax
from jax.experimental import pallas as pl
from jax.experimental.pallas import tpu as pltpu

F32 = jnp.float32
BF16 = jnp.bfloat16
I32 = jnp.int32

LANES = 128
SUBLANES = 8
ROW_TILE = 256
EXPERT_TILE = 512
MXU_WIDTH = 256
VMEM_LIMIT = 56 * 1024 * 1024
CONV_WIDTH = 4
QKV_BLOCK = 4
MLSTM_CHUNK_PROMPT = 256
MLSTM_CHUNK_SAMPLE = 128
CMLP_CHUNK = 128
STAB_INIT = -1e30
N_GATES = 8


def _cparams(*sem):
    return pltpu.CompilerParams(dimension_semantics=sem, vmem_limit_bytes=VMEM_LIMIT)


def _rms(x, g, eps=1e-6):
    return x * lax.rsqrt(jnp.mean(x * x, axis=-1, keepdims=True) + eps) * g


def _silu(x):
    hx = 0.5 * x
    return hx * jnp.tanh(hx) + hx


def _split3(x):
    p0 = x.astype(BF16)
    r = x - p0.astype(F32)
    p1 = r.astype(BF16)
    p2 = (r - p1.astype(F32)).astype(BF16)
    return p0, p1, p2


def _dot(a, b):
    return jnp.dot(a, b, preferred_element_type=F32)


def _dot_nt(a, b):
    return lax.dot_general(a, b, (((1,), (1,)), ((), ())), preferred_element_type=F32)


def _dot_tn(a, b):
    return lax.dot_general(a, b, (((0,), (0,)), ((), ())), preferred_element_type=F32)


def _dot_nt_f32(a, b):
    a0, a1, a2 = _split3(a)
    b0, b1, b2 = _split3(b)
    acc = _dot_nt(a0, b0)
    acc += _dot_nt(a0, b1) + _dot_nt(a1, b0)
    acc += _dot_nt(a1, b1) + _dot_nt(a0, b2) + _dot_nt(a2, b0)
    return acc


def _norm_up_body(xp_ref, xs_ref, g_ref, w_ref, xm_ref, z_ref, *, n_prompt_tiles, inner):
    i = pl.program_id(0)
    x = jnp.where(i < n_prompt_tiles, xp_ref[...], xs_ref[...])
    h = _rms(x, g_ref[...]).astype(BF16)
    u = _dot(h, w_ref[...])
    xm_ref[...] = u[:, :inner]
    z_ref[...] = u[:, inner:].astype(BF16)


def _two_src_specs(n_prompt_tiles, width):
    sp = pl.BlockSpec((ROW_TILE, width), lambda i: (jnp.minimum(i, n_prompt_tiles - 1), 0))
    ss = pl.BlockSpec((ROW_TILE, width), lambda i: (jnp.maximum(i - n_prompt_tiles, 0), 0))
    return sp, ss


def _norm_up(xp, xs, g, w):
    tp, d = xp.shape
    ts = xs.shape[0]
    tt = tp + ts
    n2 = w.shape[1]
    inner = n2 // 2
    npt = tp // ROW_TILE
    sp, ss = _two_src_specs(npt, d)
    return pl.pallas_call(
        functools.partial(_norm_up_body, n_prompt_tiles=npt, inner=inner),
        grid=(tt // ROW_TILE,),
        in_specs=[sp, ss,
                  pl.BlockSpec((1, d), lambda i: (0, 0)),
                  pl.BlockSpec((d, n2), lambda i: (0, 0))],
        out_specs=[pl.BlockSpec((ROW_TILE, inner), lambda i: (i, 0)),
                   pl.BlockSpec((ROW_TILE, inner), lambda i: (i, 0))],
        out_shape=[jax.ShapeDtypeStruct((tt, inner), F32),
                   jax.ShapeDtypeStruct((tt, inner), BF16)],
        compiler_params=_cparams("parallel"),
        name="norm_up",
    )(xp, xs, g, w)


def _conv_qkv_body(xm_ref, st_ref, cw_ref, cb_ref, wqk_ref, wv_ref, wg_ref, bg_ref,
                   q_ref, k_ref, v_ref, xc_ref, gc_ref, bc_ref, st_out_ref,
                   *, row_stride, k_scale):
    t = pl.program_id(1)
    halo = st_ref.shape[1]
    rows, inner = xm_ref.shape

    @pl.when(t == 0)
    def _():
        st_out_ref[0] = st_ref[0]

    x = xm_ref[...]
    xe = jnp.concatenate([st_out_ref[0], x], axis=0)
    acc = x * cw_ref[CONV_WIDTH - 1:CONV_WIDTH, :] + cb_ref[...]
    for j in range(CONV_WIDTH - 1):
        lo = halo - (CONV_WIDTH - 1 - j) * row_stride
        acc = acc + xe[lo:lo + rows, :] * cw_ref[j:j + 1, :]
    st_out_ref[0] = xe[rows:rows + halo, :]
    xc = _silu(acc)
    xc_b = xc.astype(BF16)
    xm_b = x.astype(BF16)
    xc_ref[...] = xc_b

    gacc = jnp.zeros((rows, LANES), F32)
    for c in range(inner // MXU_WIDTH):
        sl = slice(c * MXU_WIDTH, (c + 1) * MXU_WIDTH)
        qk = _dot(xc_b[:, sl], wqk_ref[c])
        vv = _dot(xm_b[:, sl], wv_ref[c])
        qb = qk[:, :MXU_WIDTH].astype(BF16)
        kb = qk[:, MXU_WIDTH:].astype(BF16)
        vb = vv.astype(BF16)
        q_ref[:, sl] = qb
        k_ref[:, sl] = (qk[:, MXU_WIDTH:] * k_scale).astype(BF16)
        v_ref[:, sl] = vb
        gacc += _dot(qb, wg_ref[0, sl, :]) + _dot(kb, wg_ref[1, sl, :]) + _dot(vb, wg_ref[2, sl, :])
    g = gacc + bg_ref[...]
    lane = lax.broadcasted_iota(I32, g.shape, 1)
    logsig = jnp.minimum(g, 0.0) - jnp.log1p(jnp.exp(-jnp.abs(g)))
    gates = jnp.where(lane < N_GATES // 2, g, logsig)
    gc_ref[...] = gates
    bc_ref[...] = jnp.where(lane < N_GATES // 2, gates, _cumsum_rows(gates))


def _conv_qkv(xm, row_block_offset, state, cw, cb, wqk, wv, wg, bg, *, n_groups, tiles_per_group,
              row_stride, k_scale):
    inner = xm.shape[1]
    rows = n_groups * tiles_per_group * ROW_TILE
    halo = state.shape[1]
    nt = tiles_per_group
    row_spec = pl.BlockSpec((ROW_TILE, inner), lambda b, t: (b * nt + t, 0))
    gate_spec = pl.BlockSpec((ROW_TILE, LANES), lambda b, t: (b * nt + t, 0))
    full = lambda a: pl.BlockSpec(a.shape, lambda b, t: (0,) * a.ndim)
    return pl.pallas_call(
        functools.partial(_conv_qkv_body, row_stride=row_stride, k_scale=k_scale),
        grid=(n_groups, nt),
        in_specs=[pl.BlockSpec((ROW_TILE, inner), lambda b, t: (row_block_offset + b * nt + t, 0)),
                  pl.BlockSpec((1, halo, inner), lambda b, t: (b, 0, 0)),
                  full(cw), full(cb), full(wqk), full(wv), full(wg), full(bg)],
        out_specs=[row_spec, row_spec, row_spec, row_spec, gate_spec, gate_spec,
                   pl.BlockSpec((1, halo, inner), lambda b, t: (b, 0, 0))],
        out_shape=[jax.ShapeDtypeStruct((rows, inner), BF16)] * 4
        + [jax.ShapeDtypeStruct((rows, LANES), F32)] * 2
        + [jax.ShapeDtypeStruct((n_groups, halo, inner), F32)],
        compiler_params=_cparams("parallel", "arbitrary"),
        name="conv_qkv",
    )(xm, state, cw, cb, wqk, wv, wg, bg)


def _cumsum_rows(x):
    n = x.shape[0]
    tri = (lax.broadcasted_iota(I32, (n, n), 1) <= lax.broadcasted_iota(I32, (n, n), 0)).astype(BF16)
    p0, p1, p2 = _split3(x)
    return _dot(tri, p0) + _dot(tri, p1) + _dot(tri, p2)


def _cumsum_lanes(x):
    n = x.shape[1]
    tri = (lax.broadcasted_iota(I32, (n, n), 0) <= lax.broadcasted_iota(I32, (n, n), 1)).astype(BF16)
    p0, p1, p2 = _split3(x)
    return _dot(p0, tri) + _dot(p1, tri) + _dot(p2, tri)


def _mlstm_chunk(q, k, v, ig_col, bcum_col, ig_row, bcum_row, c_in, c_out, n_prev, m_prev):
    lq, lk = q.shape[0], k.shape[0]
    ri = lax.broadcasted_iota(I32, (lq, lk), 0)
    ci = lax.broadcasted_iota(I32, (lq, lk), 1)
    bcum_q = bcum_col[:lq]
    log_d = jnp.where(ci <= ri, bcum_q - bcum_row + ig_row, -jnp.inf)
    m_inter = bcum_q + m_prev
    m_t = jnp.maximum(m_inter, jnp.max(log_d, axis=-1, keepdims=True))
    s = _dot_nt(q, k) * jnp.exp(log_d - m_t)
    inter = jnp.exp(m_inter - m_t)
    m_new = m_t[lq - 1:lq, :]
    b_last = bcum_col[lk - 1:lk, :]
    w_col = jnp.exp(b_last - bcum_col + ig_col - m_new)
    decay = jnp.exp(b_last + m_prev - m_new)
    wk = w_col * k.astype(F32)
    c_prev = c_in[...]
    num = _dot(s.astype(BF16), v) + inter * _dot(q, c_prev.astype(BF16))
    c_out[...] = decay * c_prev + _dot_tn(wk.astype(BF16), v)
    den = jnp.sum(s, axis=-1, keepdims=True) + inter * jnp.sum(q.astype(F32) * n_prev, axis=-1, keepdims=True)
    h = num * (1.0 / jnp.maximum(jnp.abs(den), jnp.exp(-m_t)))
    n_new = decay * n_prev + jnp.sum(wk, axis=0, keepdims=True)
    return h, n_new, m_new


def _head_norm(h, w, eps=1e-5):
    mu = jnp.mean(h, axis=-1, keepdims=True)
    var = jnp.mean(jnp.square(h - mu), axis=-1, keepdims=True)
    return (h - mu) * lax.rsqrt(var + eps) * w


def _cell_prompt_body(q_ref, k_ref, v_ref, gc_ref, gr_ref, hw_ref,
                      h_ref, c_ref, n_ref, m_ref, *, heads):
    c = pl.program_id(1)

    @pl.when(c == 0)
    def _():
        c_ref[...] = jnp.zeros(c_ref.shape, F32)
        n_ref[...] = jnp.zeros(n_ref.shape, F32)
        m_ref[...] = jnp.full(m_ref.shape, STAB_INIT, F32)

    gc = gc_ref[...]
    gr = gr_ref[...]
    half = N_GATES // 2
    dh = q_ref.shape[1] // heads
    for hd in range(heads):
        sl = slice(hd * dh, (hd + 1) * dh)
        h, n_new, m_new = _mlstm_chunk(
            q_ref[:, sl], k_ref[:, sl], v_ref[:, sl],
            gc[:, hd:hd + 1], gc[:, hd + half:hd + half + 1],
            gr[hd:hd + 1, :], gr[hd + half:hd + half + 1, :],
            c_ref.at[0, hd], c_ref.at[0, hd], n_ref[0, hd], m_ref[0, hd, :, 0:1])
        h_ref[:, sl] = _head_norm(h, hw_ref[hd]).astype(BF16)
        n_ref[0, hd] = n_new
        m_ref[0, hd] = jnp.broadcast_to(m_new, m_ref.shape[2:])


def _cell_prompt(q, k, v, gc, gr, hn_w, *, batch, heads):
    tp, inner = q.shape
    dh = inner // heads
    seq = tp // batch
    ln = MLSTM_CHUNK_PROMPT
    nc = seq // ln
    qspec = pl.BlockSpec((ln, inner), lambda b, c: (b * nc + c, 0))
    return pl.pallas_call(
        functools.partial(_cell_prompt_body, heads=heads),
        grid=(batch, nc),
        in_specs=[qspec, qspec, qspec,
                  pl.BlockSpec((ln, LANES), lambda b, c: (b * nc + c, 0)),
                  pl.BlockSpec((SUBLANES, ln), lambda b, c: (0, b * nc + c)),
                  pl.BlockSpec((heads, 1, dh), lambda b, c: (0, 0, 0))],
        out_specs=[qspec,
                   pl.BlockSpec((1, heads, dh, dh), lambda b, c: (b, 0, 0, 0)),
                   pl.BlockSpec((1, heads, 1, dh), lambda b, c: (b, 0, 0, 0)),
                   pl.BlockSpec((1, heads, 1, LANES), lambda b, c: (b, 0, 0, 0))],
        out_shape=[jax.ShapeDtypeStruct((tp, inner), BF16),
                   jax.ShapeDtypeStruct((batch, heads, dh, dh), F32),
                   jax.ShapeDtypeStruct((batch, heads, 1, dh), F32),
                   jax.ShapeDtypeStruct((batch, heads, 1, LANES), F32)],
        compiler_params=_cparams("parallel", "arbitrary"),
        name="mlstm_cell_prompt",
    )(q, k, v, gc, gr, hn_w)


STATE_RING = 3


def _cell_sample_body(q_ref, k_ref, v_ref, gc_ref, gr_ref, hw_ref, c0_hbm, n0_ref, m0_ref,
                      h_ref, c_ref, n_ref, m_ref, c0_buf, c0_sem, *, heads, valid):
    b = pl.program_id(0)
    nb = pl.num_programs(0)

    def fetch(step):
        slot = step % STATE_RING
        return pltpu.make_async_copy(c0_hbm.at[step], c0_buf.at[slot], c0_sem.at[slot])

    @pl.when(b == 0)
    def _():
        for step in range(STATE_RING - 1):
            @pl.when(step < nb)
            def _():
                fetch(step).start()

    @pl.when(b + STATE_RING - 1 < nb)
    def _():
        fetch(b + STATE_RING - 1).start()

    fetch(b).wait()
    c0_ref = c0_buf.at[b % STATE_RING]
    ln = MLSTM_CHUNK_SAMPLE
    rows = q_ref.shape[1]
    dh = q_ref.shape[2] // heads
    half = N_GATES // 2
    gc = gc_ref[0]
    gr = gr_ref[0]
    row_c = lax.broadcasted_iota(I32, gc.shape, 0)
    lane_c = lax.broadcasted_iota(I32, gc.shape, 1)
    gc = jnp.where(row_c < valid, gc, jnp.where(lane_c < half, -jnp.inf, 0.0))
    row_r = lax.broadcasted_iota(I32, gr.shape, 0)
    lane_r = lax.broadcasted_iota(I32, gr.shape, 1)
    gr = jnp.where(lane_r < valid, gr, jnp.where(row_r < half, -jnp.inf, 0.0))
    bc = _cumsum_rows(jnp.where(lane_c < half, 0.0, gc))
    br = _cumsum_lanes(jnp.where(row_r < half, 0.0, gr))
    pad = jnp.zeros((ln - rows, dh), F32)

    def padded(ref, sl):
        return jnp.concatenate([ref[0, :, sl].astype(F32), pad], axis=0).astype(BF16)

    for hd in range(heads):
        sl = slice(hd * dh, (hd + 1) * dh)
        h, n_new, m_new = _mlstm_chunk(
            q_ref[0, :, sl], padded(k_ref, sl), padded(v_ref, sl),
            gc[:, hd:hd + 1], bc[:, hd + half:hd + half + 1],
            gr[hd:hd + 1, :], br[hd + half:hd + half + 1, :],
            c0_ref.at[hd], c_ref.at[0, hd], n0_ref[0, hd], m0_ref[0, hd, :, 0:1])
        h_ref[0, :, sl] = _head_norm(h, hw_ref[hd]).astype(BF16)
        n_ref[0, hd] = n_new
        m_ref[0, hd] = jnp.broadcast_to(m_new, m_ref.shape[2:])


def _cell_sample(q, k, v, gc, gr, hn_w, c0, n0, m0, *, heads, valid, group_batch):
    n_groups, rows, wide = q.shape
    inner = wide // group_batch
    dh = inner // heads
    nb = n_groups * group_batch
    ln = MLSTM_CHUNK_SAMPLE
    qspec = pl.BlockSpec((1, rows, inner), lambda b: (b // group_batch, 0, b % group_batch))
    cspec = pl.BlockSpec((1, heads, dh, dh), lambda b: (b, 0, 0, 0))
    nspec = pl.BlockSpec((1, heads, 1, dh), lambda b: (b, 0, 0, 0))
    mspec = pl.BlockSpec((1, heads, 1, LANES), lambda b: (b, 0, 0, 0))
    return pl.pallas_call(
        functools.partial(_cell_sample_body, heads=heads, valid=valid),
        grid=(nb,),
        in_specs=[qspec, qspec, qspec,
                  pl.BlockSpec((1, ln, LANES), lambda b: (b, 0, 0)),
                  pl.BlockSpec((1, SUBLANES, ln), lambda b: (b, 0, 0)),
                  pl.BlockSpec((heads, 1, dh), lambda b: (0, 0, 0)),
                  pl.BlockSpec(memory_space=pl.ANY), nspec, mspec],
        out_specs=[qspec, cspec, nspec, mspec],
        out_shape=[jax.ShapeDtypeStruct(q.shape, BF16),
                   jax.ShapeDtypeStruct((nb, heads, dh, dh), F32),
                   jax.ShapeDtypeStruct((nb, heads, 1, dh), F32),
                   jax.ShapeDtypeStruct((nb, heads, 1, LANES), F32)],
        scratch_shapes=[pltpu.VMEM((STATE_RING, heads, dh, dh), F32),
                        pltpu.SemaphoreType.DMA((STATE_RING,))],
        compiler_params=_cparams("arbitrary"),
        name="mlstm_cell_sample",
    )(q, k, v, gc, gr, hn_w, c0, n0, m0)


def _gate_down_body(hp_ref, hs_ref, cp_ref, cs_ref, z_ref, skip_ref, w_ref, xp_ref, xs_ref, o_ref,
                    *, n_prompt_tiles):
    i = pl.program_id(0)
    is_p = i < n_prompt_tiles
    hn = jnp.where(is_p, hp_ref[...], hs_ref[...]).astype(F32)
    xc = jnp.where(is_p, cp_ref[...], cs_ref[...]).astype(F32)
    xres = jnp.where(is_p, xp_ref[...], xs_ref[...])
    a = (hn + skip_ref[...] * xc) * _silu(z_ref[...].astype(F32))
    o_ref[...] = xres + _dot(a.astype(BF16), w_ref[...])


def _gate_down(hn_p, hn_s, xc_p, xc_s, z, skip, w, xp, xs):
    tp, inner = hn_p.shape
    tt = z.shape[0]
    d = w.shape[1]
    npt = tp // ROW_TILE
    ip, is_ = _two_src_specs(npt, inner)
    dp, ds = _two_src_specs(npt, d)
    return pl.pallas_call(
        functools.partial(_gate_down_body, n_prompt_tiles=npt),
        grid=(tt // ROW_TILE,),
        in_specs=[ip, is_, ip, is_,
                  pl.BlockSpec((ROW_TILE, inner), lambda i: (i, 0)),
                  pl.BlockSpec((1, inner), lambda i: (0, 0)),
                  pl.BlockSpec((inner, d), lambda i: (0, 0)),
                  dp, ds],
        out_specs=pl.BlockSpec((ROW_TILE, d), lambda i: (i, 0)),
        out_shape=jax.ShapeDtypeStruct((tt, d), F32),
        compiler_params=_cparams("parallel"),
        name="gate_down",
    )(hn_p, hn_s, xc_p, xc_s, z, skip, w, xp, xs)


def _cmlp_in_body(x_ref, g_ref, w_ref, b_ref, lw_ref, lb_ref, u_ref, v_ref, *, half):
    h = _rms(x_ref[...], g_ref[...]).astype(BF16)
    y = _dot(h, w_ref[...]) + b_ref[...]
    c = math.sqrt(2.0 / math.pi)
    hy = 0.5 * y
    zz = hy * jnp.tanh(y * (c + (0.044715 * c) * (y * y))) + hy
    u_ref[...] = zz[:, :half].astype(BF16)
    v = zz[:, half:]
    mu = jnp.mean(v, axis=-1, keepdims=True)
    var = jnp.mean(jnp.square(v - mu), axis=-1, keepdims=True)
    v_ref[...] = (v - mu) * lax.rsqrt(var + 1e-5) * lw_ref[...] + lb_ref[...]


def _cmlp_in(x, g, w, b, lw, lb):
    tt, d = x.shape
    n2 = w.shape[1]
    half = n2 // 2
    vec = lambda n: pl.BlockSpec((1, n), lambda i: (0, 0))
    return pl.pallas_call(
        functools.partial(_cmlp_in_body, half=half),
        grid=(tt // ROW_TILE,),
        in_specs=[pl.BlockSpec((ROW_TILE, d), lambda i: (i, 0)), vec(d),
                  pl.BlockSpec((d, n2), lambda i: (0, 0)), vec(n2), vec(half), vec(half)],
        out_specs=[pl.BlockSpec((ROW_TILE, half), lambda i: (i, 0))] * 2,
        out_shape=[jax.ShapeDtypeStruct((tt, half), BF16), jax.ShapeDtypeStruct((tt, half), F32)],
        compiler_params=_cparams("parallel"),
        name="cmlp_in",
    )(x, g, w, b, lw, lb)


def _cmlp_mix_out_body(u_ref, v_ref, mix_ref, bias_ref, w_ref, bo_ref, x_ref, o_ref, *, groups):
    half = u_ref.shape[1]
    gd = half // groups
    pieces = []
    for g in range(groups):
        sl = slice(g * gd, (g + 1) * gd)
        mixed = _dot(mix_ref[0, g], v_ref[:, sl].astype(BF16)) + bias_ref[0, :, g:g + 1]
        pieces.append((u_ref[:, sl].astype(F32) * mixed).astype(BF16))
    a = jnp.concatenate(pieces, axis=1)
    o_ref[...] = x_ref[...] + _dot(a, w_ref[...]) + bo_ref[...]


def _cmlp_mix_out(u, v, mix, bias, w, bo, x, *, n_prompt_tiles):
    tt, half = u.shape
    d = w.shape[1]
    groups = mix.shape[1]
    kind = lambda i: jnp.where(i < n_prompt_tiles, 0, 1)
    return pl.pallas_call(
        functools.partial(_cmlp_mix_out_body, groups=groups),
        grid=(tt // ROW_TILE,),
        in_specs=[pl.BlockSpec((ROW_TILE, half), lambda i: (i, 0)),
                  pl.BlockSpec((ROW_TILE, half), lambda i: (i, 0)),
                  pl.BlockSpec((1, groups, ROW_TILE, ROW_TILE), lambda i: (kind(i), 0, 0, 0)),
                  pl.BlockSpec((1, ROW_TILE, groups), lambda i: (kind(i), 0, 0)),
                  pl.BlockSpec((half, d), lambda i: (0, 0)),
                  pl.BlockSpec((1, d), lambda i: (0, 0)),
                  pl.BlockSpec((ROW_TILE, d), lambda i: (i, 0))],
        out_specs=pl.BlockSpec((ROW_TILE, d), lambda i: (i, 0)),
        out_shape=jax.ShapeDtypeStruct((tt, d), F32),
        compiler_params=_cparams("parallel"),
        name="cmlp_mix_out",
    )(u, v, mix, bias, w, bo, x)


GROUP_ROW_OFFSET = SUBLANES


def _first_argmax_rows(x, n):
    row = lax.broadcasted_iota(I32, x.shape, 0)
    mx = jnp.max(x, axis=0, keepdims=True)
    idx = jnp.min(jnp.where(x == mx, row, n), axis=0, keepdims=True)
    return idx, mx


def _router_body(x_ref, g_ref, wt_ref, bt_ref, route_ref, cnt_ref, *, n_groups, per_group):
    i = pl.program_id(0)
    n_exp = n_groups * per_group

    @pl.when(i == 0)
    def _():
        cnt_ref[...] = jnp.zeros(cnt_ref.shape, F32)

    h = _rms(x_ref[...], g_ref[...])
    logits = _dot_nt_f32(wt_ref[...], h) + bt_ref[:, 0:1]
    tokens = logits.shape[1]
    gl = logits[0:n_groups, :]
    gmax = jnp.max(gl, axis=0, keepdims=True)
    ge = jnp.exp(gl - gmax)
    gp = ge / jnp.sum(ge, axis=0, keepdims=True)
    g_idx, g_w = _first_argmax_rows(gp, n_groups)
    e_sel = jnp.zeros((per_group, tokens), F32)
    for g in range(n_groups):
        lo = GROUP_ROW_OFFSET + g * per_group
        e_sel = e_sel + jnp.where(g_idx == g, logits[lo:lo + per_group, :], 0.0)
    i1, m1 = _first_argmax_rows(e_sel, per_group)
    row = lax.broadcasted_iota(I32, e_sel.shape, 0)
    i2, m2 = _first_argmax_rows(jnp.where(row == i1, -jnp.inf, e_sel), per_group)
    t = jnp.exp(m2 - m1)
    w1 = (1.0 / (1.0 + t)) * g_w
    w2 = (t / (1.0 + t)) * g_w
    e1 = g_idx * per_group + i1
    e2 = g_idx * per_group + i2

    erow = lax.broadcasted_iota(I32, (n_exp, tokens), 0)
    oh1 = erow == e1
    oh2 = erow == e2
    onehot = jnp.where(oh1 | oh2, 1.0, 0.0)
    strict_upper = (lax.broadcasted_iota(I32, (tokens, tokens), 0)
                    < lax.broadcasted_iota(I32, (tokens, tokens), 1)).astype(BF16)
    before = _dot(onehot.astype(BF16), strict_upper) + cnt_ref[:, 0:1]
    r1 = jnp.sum(jnp.where(oh1, before, 0.0), axis=0, keepdims=True)
    r2 = jnp.sum(jnp.where(oh2, before, 0.0), axis=0, keepdims=True)
    cnt_ref[...] = cnt_ref[...] + jnp.sum(onehot, axis=1, keepdims=True)
    zero = jnp.zeros_like(w1)
    route_ref[...] = jnp.concatenate(
        [e1.astype(F32), e2.astype(F32), r1, r2, w1, w2, zero, zero], axis=0)


def _wide_tile(n_rows, max_tiles):
    n = n_rows // ROW_TILE
    return ROW_TILE * max(k for k in range(1, max_tiles + 1) if n % k == 0)


def _router(x, g, wt, bt, *, n_groups, per_group):
    tt, d = x.shape
    n_exp = n_groups * per_group
    rt = _wide_tile(tt, 2)
    return pl.pallas_call(
        functools.partial(_router_body, n_groups=n_groups, per_group=per_group),
        grid=(tt // rt,),
        in_specs=[pl.BlockSpec((rt, d), lambda i: (i, 0)),
                  pl.BlockSpec((1, d), lambda i: (0, 0)),
                  pl.BlockSpec(wt.shape, lambda i: (0, 0)),
                  pl.BlockSpec(bt.shape, lambda i: (0, 0))],
        out_specs=[pl.BlockSpec((SUBLANES, rt), lambda i: (0, i)),
                   pl.BlockSpec((n_exp, LANES), lambda i: (0, 0))],
        out_shape=[jax.ShapeDtypeStruct((SUBLANES, tt), F32),
                   jax.ShapeDtypeStruct((n_exp, LANES), F32)],
        compiler_params=_cparams("arbitrary"),
        name="moe_router",
    )(x, g, wt, bt)


def _positions_body(route_ref, off_ref, pos_ref):
    r = route_ref[...]
    n_exp = off_ref.shape[0]
    tokens = r.shape[1]
    erow = lax.broadcasted_iota(I32, (n_exp, tokens), 0)
    off = off_ref[:, 0:1]
    out = []
    for k in range(2):
        e = r[k:k + 1, :].astype(I32)
        base = jnp.sum(jnp.where(erow == e, off, 0.0), axis=0, keepdims=True)
        out.append((base + r[2 + k:3 + k, :]).astype(I32))
    pos_ref[...] = jnp.concatenate(out, axis=0)


def _positions(route, off):
    tt = route.shape[1]
    pt = _wide_tile(tt, 16)
    return pl.pallas_call(
        _positions_body,
        grid=(tt // pt,),
        in_specs=[pl.BlockSpec((SUBLANES, pt), lambda i: (0, i)),
                  pl.BlockSpec(off.shape, lambda i: (0, 0))],
        out_specs=pl.BlockSpec((2, pt), lambda i: (0, i)),
        out_shape=jax.ShapeDtypeStruct((2, tt), I32),
        compiler_params=_cparams("parallel"),
        name="moe_positions",
    )(route, off)


def _for_each_row(n_rows, fn):
    def group(g, carry):
        start = pl.multiple_of(g * SUBLANES, SUBLANES)
        for u in range(SUBLANES):
            fn(start, u)
        return carry

    lax.fori_loop(0, n_rows // SUBLANES, group, 0, unroll=2)


N_DMA_PRIORITIES = 2


def _start_alternating(copies):
    for i, cp in enumerate(copies):
        cp.start(priority=i % N_DMA_PRIORITIES)


def _tile_row(ref, group_start, u):
    return ref.at[pl.ds(group_start, SUBLANES)].at[pl.ds(u, 1)]


def _dispatch_body(pos_ref, x_ref, xs_hbm, sem, *, n_tokens):
    base = pl.program_id(0) * ROW_TILE

    def copies(start, u):
        return [pltpu.make_async_copy(_tile_row(x_ref, start, u),
                                      xs_hbm.at[pl.ds(pos_ref[k * n_tokens + base + start + u], 1)], sem)
                for k in range(2)]

    _for_each_row(ROW_TILE, lambda start, u: _start_alternating(copies(start, u)))
    _for_each_row(ROW_TILE, lambda start, u: [cp.wait() for cp in copies(start, u)])


def _dispatch(pos_flat, x):
    tt, d = x.shape
    return pl.pallas_call(
        functools.partial(_dispatch_body, n_tokens=tt),
        grid_spec=pltpu.PrefetchScalarGridSpec(
            num_scalar_prefetch=1,
            grid=(tt // ROW_TILE,),
            in_specs=[pl.BlockSpec((ROW_TILE, d), lambda i, p: (i, 0))],
            out_specs=pl.BlockSpec(memory_space=pl.ANY),
            scratch_shapes=[pltpu.SemaphoreType.DMA]),
        out_shape=jax.ShapeDtypeStruct((2 * tt, d), F32),
        compiler_params=_cparams("arbitrary"),
        name="moe_dispatch",
    )(pos_flat, x)


def _experts_body(tile_ref, exp_ref, lo_ref, hi_ref, nwork_ref,
                  xs_ref, g_ref, wg_ref, wu_ref, wd_ref, o_ref, wg_b, wu_b, wd_b):
    w = pl.program_id(0)
    prev = jnp.maximum(w - 1, 0)
    first = jnp.logical_or(w == 0, tile_ref[w] != tile_ref[prev])
    new_expert = jnp.logical_or(w == 0, exp_ref[w] != exp_ref[prev])

    @pl.when(w < nwork_ref[0])
    def _():
        @pl.when(new_expert)
        def _():
            wg_b[...] = wg_ref[0].astype(BF16)
            wu_b[...] = wu_ref[0].astype(BF16)
            wd_b[...] = wd_ref[0].astype(BF16)

        xb = _rms(xs_ref[...], g_ref[...]).astype(BF16)
        a = _dot(xb, wg_b[...])
        b = _dot(xb, wu_b[...])
        hg = (_silu(a) * b).astype(BF16)
        y = _dot(hg, wd_b[...])
        lo, hi = lo_ref[w], hi_ref[w]
        whole = jnp.logical_and(lo == 0, hi == EXPERT_TILE)
        row = lax.broadcasted_iota(I32, y.shape, 0)
        mine = (row >= lo) & (row < hi)

        @pl.when(whole)
        def _():
            o_ref[...] = y

        @pl.when(jnp.logical_and(first, jnp.logical_not(whole)))
        def _():
            o_ref[...] = jnp.where(mine, y, 0.0)

        @pl.when(jnp.logical_not(first))
        def _():
            o_ref[...] = jnp.where(mine, y, o_ref[...])


def _experts(meta, xs, g, wg, wu, wd):
    rows, d = xs.shape
    n_work = meta[0].shape[0]
    f = wg.shape[2]
    return pl.pallas_call(
        _experts_body,
        grid_spec=pltpu.PrefetchScalarGridSpec(
            num_scalar_prefetch=5,
            grid=(n_work,),
            in_specs=[pl.BlockSpec((EXPERT_TILE, d), lambda w, tl, ex, lo, hi, nw: (tl[w], 0)),
                      pl.BlockSpec((1, d), lambda w, tl, ex, lo, hi, nw: (0, 0)),
                      pl.BlockSpec((1, d, f), lambda w, tl, ex, lo, hi, nw: (ex[w], 0, 0)),
                      pl.BlockSpec((1, d, f), lambda w, tl, ex, lo, hi, nw: (ex[w], 0, 0)),
                      pl.BlockSpec((1, f, d), lambda w, tl, ex, lo, hi, nw: (ex[w], 0, 0))],
            out_specs=pl.BlockSpec((EXPERT_TILE, d), lambda w, tl, ex, lo, hi, nw: (tl[w], 0)),
            scratch_shapes=[pltpu.VMEM((d, f), BF16), pltpu.VMEM((d, f), BF16), pltpu.VMEM((f, d), BF16)]),
        out_shape=jax.ShapeDtypeStruct((rows, d), F32),
        compiler_params=_cparams("arbitrary"),
        name="moe_experts",
    )(*meta, xs, g, wg, wu, wd)


def _combine_rows(pos_ref, x_ref, route_ref, ys_hbm, buf, sem, n_tokens):
    base = pl.program_id(0) * ROW_TILE

    def copies(start, u):
        return [pltpu.make_async_copy(ys_hbm.at[pl.ds(pos_ref[k * n_tokens + base + start + u], 1)],
                                      _tile_row(buf.at[k], start, u), sem)
                for k in range(2)]

    _for_each_row(ROW_TILE, lambda start, u: _start_alternating(copies(start, u)))
    r = route_ref[...]
    rt = jnp.concatenate([r, jnp.zeros((LANES - SUBLANES, r.shape[1]), F32)], axis=0).T
    _for_each_row(ROW_TILE, lambda start, u: [cp.wait() for cp in copies(start, u)])
    return x_ref[...] + rt[:, 4:5] * buf[0] + rt[:, 5:6] * buf[1]


def _combine_body(pos_ref, x_ref, route_ref, ys_hbm, o_ref, buf, sem, *, n_tokens):
    o_ref[...] = _combine_rows(pos_ref, x_ref, route_ref, ys_hbm, buf, sem, n_tokens)


def _combine_norm_body(pos_ref, x_ref, route_ref, g_ref, ys_hbm, op_ref, os_ref, buf, sem,
                       *, n_tokens, n_prompt_tiles):
    y = _rms(_combine_rows(pos_ref, x_ref, route_ref, ys_hbm, buf, sem, n_tokens), g_ref[...])
    is_prompt = pl.program_id(0) < n_prompt_tiles

    @pl.when(is_prompt)
    def _():
        op_ref[...] = y

    @pl.when(jnp.logical_not(is_prompt))
    def _():
        os_ref[...] = y


def _combine(pos_flat, x, route, ys, final_g=None, n_prompt_tiles=None):
    tt, d = x.shape
    in_specs = [pl.BlockSpec((ROW_TILE, d), lambda i, p: (i, 0)),
                pl.BlockSpec((SUBLANES, ROW_TILE), lambda i, p: (0, i))]
    args = [pos_flat, x, route]
    if final_g is None:
        body = functools.partial(_combine_body, n_tokens=tt)
        out_specs = pl.BlockSpec((ROW_TILE, d), lambda i, p: (i, 0))
        out_shape = jax.ShapeDtypeStruct((tt, d), F32)
    else:
        npt = n_prompt_tiles
        body = functools.partial(_combine_norm_body, n_tokens=tt, n_prompt_tiles=npt)
        in_specs.append(pl.BlockSpec((1, d), lambda i, p: (0, 0)))
        args.append(final_g)
        out_specs = [pl.BlockSpec((ROW_TILE, d), lambda i, p: (jnp.minimum(i, npt - 1), 0)),
                     pl.BlockSpec((ROW_TILE, d), lambda i, p: (jnp.maximum(i - npt, 0), 0))]
        out_shape = [jax.ShapeDtypeStruct((npt * ROW_TILE, d), F32),
                     jax.ShapeDtypeStruct((tt - npt * ROW_TILE, d), F32)]
    in_specs.append(pl.BlockSpec(memory_space=pl.ANY))
    args.append(ys)
    return pl.pallas_call(
        body,
        grid_spec=pltpu.PrefetchScalarGridSpec(
            num_scalar_prefetch=1,
            grid=(tt // ROW_TILE,),
            in_specs=in_specs,
            out_specs=out_specs,
            scratch_shapes=[pltpu.VMEM((2, ROW_TILE, d), F32), pltpu.SemaphoreType.DMA]),
        out_shape=out_shape,
        compiler_params=_cparams("arbitrary"),
        name="moe_combine",
    )(*args)


def _work_items(counts, n_rows):
    n_exp = counts.shape[0]
    n_tiles = n_rows // EXPERT_TILE
    n_work = n_tiles + n_exp - 1
    ids = jnp.arange(n_exp, dtype=I32)
    lower = ids[None, :] <= ids[:, None]

    def cumsum(v):
        return jnp.sum(jnp.where(lower, v[None, :], 0), axis=1)

    end = cumsum(counts)
    off = end - counts
    first_tile = off // EXPERT_TILE
    last_tile = (end - 1) // EXPERT_TILE
    per_exp = jnp.where(counts > 0, last_tile - first_tile + 1, 0)
    wend = cumsum(per_exp)
    wstart = wend - per_exp
    total = jnp.sum(per_exp)
    w = jnp.arange(n_work, dtype=I32)
    e = jnp.minimum(jnp.sum((wend[None, :] <= w[:, None]).astype(I32), axis=1), n_exp - 1)
    sel = e[:, None] == ids[None, :]

    def pick(v):
        return jnp.sum(jnp.where(sel, v[None, :], 0), axis=1)

    tile = pick(first_tile) + (w - pick(wstart))
    lo = jnp.maximum(pick(off), tile * EXPERT_TILE) - tile * EXPERT_TILE
    hi = jnp.minimum(pick(end), (tile + 1) * EXPERT_TILE) - tile * EXPERT_TILE
    live = w < total
    tile = jnp.where(live, tile, n_tiles - 1)
    lo = jnp.where(live, lo, 0)
    hi = jnp.where(live, hi, 0)
    return off, (tile.astype(I32), e.astype(I32), lo.astype(I32), hi.astype(I32),
                 total.reshape(1).astype(I32))


def _moe(x, g, layer, w_rg, b_rg, w_re, b_re, w_gate, w_up, w_down, final_g=None, n_prompt_tiles=None):
    tt, d = x.shape
    n_groups, per_group = w_re.shape[0], w_re.shape[2]
    n_exp = n_groups * per_group
    f = w_gate.shape[-1]
    pad_rows = jnp.zeros((GROUP_ROW_OFFSET - n_groups, d), F32)
    wt = jnp.concatenate([w_rg.T, pad_rows, jnp.transpose(w_re, (0, 2, 1)).reshape(n_exp, d)], axis=0)
    bt = jnp.concatenate([b_rg, jnp.zeros((GROUP_ROW_OFFSET - n_groups,), F32), b_re.reshape(n_exp)])
    bt = jnp.broadcast_to(bt[:, None], (wt.shape[0], LANES))
    g2 = g.reshape(1, d)

    route, cnt = _router(x, g2, wt, bt, n_groups=n_groups, per_group=per_group)
    counts = cnt[:, 0].astype(I32)
    off, (tile, exp, lo, hi, total) = _work_items(counts, 2 * tt)
    off_b = jnp.broadcast_to(off.astype(F32)[:, None], (n_exp, LANES))
    pos = _positions(route, off_b).reshape(2 * tt)
    xs = _dispatch(pos, x)
    ys = _experts((tile, exp + layer * n_exp, lo, hi, total), xs, g2,
                  w_gate.reshape(-1, d, f), w_up.reshape(-1, d, f), w_down.reshape(-1, f, d))
    return _combine(pos, x, route, ys, final_g, n_prompt_tiles)


def _block_diag_tiles(w, tile):
    nb, c, _ = w.shape
    rows = w.reshape(nb * c // tile, tile, c)
    blk = jnp.arange(tile, dtype=I32) // c
    return jnp.where(blk[:, None] == blk[None, :], jnp.tile(rows, (1, 1, tile // c)), 0.0)


def _sample_to_rows(x, group_batch):
    db, s, w = x.shape
    return x.reshape(db // group_batch, group_batch, s, w).transpose(0, 2, 1, 3).reshape(db * s, w)


def _rows_to_sample(x, group_batch, s):
    rows, w = x.shape
    db = rows // s
    return x.reshape(db // group_batch, s, group_batch, w).transpose(0, 2, 1, 3).reshape(db, s, w)


def kernel(x_prompt, x_sample, state_mlstm_C, state_mlstm_n, state_mlstm_m, state_mlstm_conv,
           norm_mix, norm_ffn, norm_final,
           mlstm_w_up, mlstm_conv_w, mlstm_conv_b, mlstm_w_q, mlstm_w_k, mlstm_w_v,
           mlstm_w_ig, mlstm_b_ig, mlstm_w_fg, mlstm_b_fg, mlstm_skip, mlstm_hn_w, mlstm_w_down,
           cmlp_w_in, cmlp_b_in, cmlp_ln_w, cmlp_ln_b, cmlp_w_s, cmlp_b_s, cmlp_w_out, cmlp_b_out,
           moe_w_rg, moe_b_rg, moe_w_re, moe_b_re, moe_w_gate, moe_w_up, moe_w_down):
    batch, seq, d = x_prompt.shape
    dec_batch, dec_seq, _ = x_sample.shape
    heads = state_mlstm_C.shape[2]
    dh = state_mlstm_C.shape[3]
    inner = heads * dh
    tp, ts = batch * seq, dec_batch * dec_seq
    npt = tp // ROW_TILE
    gb = ROW_TILE // dec_seq
    n_sgroups = dec_batch // gb
    half_g = N_GATES // 2
    assert heads == half_g and tp % ROW_TILE == 0 and ts % ROW_TILE == 0
    assert ROW_TILE % dec_seq == 0 and dec_batch % gb == 0 and seq % MLSTM_CHUNK_PROMPT == 0
    assert seq % ROW_TILE == 0 and ROW_TILE % CMLP_CHUNK == 0 and dec_seq <= SUBLANES
    assert MLSTM_CHUNK_PROMPT == ROW_TILE

    xp = x_prompt.reshape(tp, d)
    xs = _sample_to_rows(x_sample, gb)

    xm, z = _norm_up(xp, xs, norm_mix[0].reshape(1, d), mlstm_w_up[0].astype(BF16))

    wq = _block_diag_tiles(mlstm_w_q[0], MXU_WIDTH)
    wk = _block_diag_tiles(mlstm_w_k[0], MXU_WIDTH)
    wqk = jnp.concatenate([wq, wk], axis=2).astype(BF16)
    wv = _block_diag_tiles(mlstm_w_v[0], MXU_WIDTH).astype(BF16)
    wg = jnp.concatenate([mlstm_w_ig[0], mlstm_w_fg[0]], axis=1)
    wg = jnp.pad(wg, ((0, 0), (0, LANES - N_GATES))).reshape(3, inner, LANES).astype(BF16)
    bg = jnp.pad(jnp.concatenate([mlstm_b_ig[0], mlstm_b_fg[0]]), (0, LANES - N_GATES)).reshape(1, LANES)
    cw, cb = mlstm_conv_w[0], mlstm_conv_b[0].reshape(1, inner)
    k_scale = float(dh) ** -0.5

    conv0_p = jnp.zeros((batch, SUBLANES, inner), F32)
    q_p, k_p, v_p, xc_p, _, bc_p, conv_p = _conv_qkv(
        xm, 0, conv0_p, cw, cb, wqk, wv, wg, bg,
        n_groups=batch, tiles_per_group=seq // ROW_TILE, row_stride=1, k_scale=k_scale)
    halo_s = (CONV_WIDTH - 1) * gb
    conv0_s = state_mlstm_conv[0].reshape(n_sgroups, gb, CONV_WIDTH - 1, inner)
    conv0_s = conv0_s.transpose(0, 2, 1, 3).reshape(n_sgroups, halo_s, inner)
    q_s, k_s, v_s, xc_s, gc_s, _, conv_s = _conv_qkv(
        xm, npt, conv0_s, cw, cb, wqk, wv, wg, bg,
        n_groups=n_sgroups, tiles_per_group=1, row_stride=gb, k_scale=k_scale)

    hn_w = mlstm_hn_w[0].reshape(heads, 1, dh)
    br_p = jnp.transpose(bc_p[:, :SUBLANES])
    hn_p, c_p, n_p, m_p = _cell_prompt(q_p, k_p, v_p, bc_p, br_p, hn_w, batch=batch, heads=heads)

    def pad_steps(a):
        a = a.reshape(n_sgroups, dec_seq, gb * inner)
        return jnp.pad(a, ((0, 0), (0, SUBLANES - dec_seq), (0, 0)))

    ln_s = MLSTM_CHUNK_SAMPLE
    gcs = gc_s.reshape(n_sgroups, dec_seq, gb, LANES).transpose(0, 2, 1, 3).reshape(dec_batch, dec_seq, LANES)
    gcs = jnp.pad(gcs, ((0, 0), (0, ln_s - dec_seq), (0, 0)))
    grs = jnp.transpose(gcs[:, :, :SUBLANES], (0, 2, 1))
    m0 = jnp.broadcast_to(state_mlstm_m[0][:, :, None, None], (dec_batch, heads, 1, LANES))
    hn_s, c_s, n_s, m_s = _cell_sample(
        pad_steps(q_s), pad_steps(k_s), pad_steps(v_s), gcs, grs, hn_w,
        state_mlstm_C[0], state_mlstm_n[0].reshape(dec_batch, heads, 1, dh), m0,
        heads=heads, valid=dec_seq, group_batch=gb)
    hn_s = hn_s[:, :dec_seq].reshape(ts, inner)

    x1 = _gate_down(hn_p, hn_s, xc_p, xc_s, z, mlstm_skip[0].reshape(1, inner),
                    mlstm_w_down[0].astype(BF16), xp, xs)
    x2 = _moe(x1, norm_ffn[0], 0, moe_w_rg[0], moe_b_rg[0], moe_w_re[0], moe_b_re[0],
              moe_w_gate, moe_w_up, moe_w_down)

    half = cmlp_w_in.shape[2] // 2
    groups = cmlp_w_s.shape[1]
    u, vn = _cmlp_in(x2, norm_mix[1].reshape(1, d), cmlp_w_in[0].astype(BF16),
                     cmlp_b_in[0].reshape(1, 2 * half), cmlp_ln_w[0].reshape(1, half),
                     cmlp_ln_b[0].reshape(1, half))
    causal = jnp.tril(jnp.ones((CMLP_CHUNK, CMLP_CHUNK), dtype=bool))
    ws = jnp.where(causal, cmlp_w_s[0], 0.0)
    reps = ROW_TILE // CMLP_CHUNK
    rid = jnp.arange(ROW_TILE, dtype=I32)
    same_chunk = (rid[:, None] // CMLP_CHUNK) == (rid[None, :] // CMLP_CHUNK)
    same_seq = (rid[:, None] % gb) == (rid[None, :] % gb)
    mix_p = jnp.where(same_chunk, jnp.tile(ws, (1, reps, reps)), 0.0)
    ws_s = ws[:, :dec_seq, :dec_seq]
    mix_s = jnp.where(same_seq, jnp.repeat(jnp.repeat(ws_s, gb, axis=1), gb, axis=2), 0.0)
    mix = jnp.stack([mix_p, mix_s]).astype(BF16)
    bias_t = jnp.transpose(cmlp_b_s[0])
    bias = jnp.stack([jnp.tile(bias_t, (reps, 1)), jnp.repeat(bias_t[:dec_seq], gb, axis=0)])
    x3 = _cmlp_mix_out(u, vn, mix, bias, cmlp_w_out[0].astype(BF16), cmlp_b_out[0].reshape(1, d), x2,
                       n_prompt_tiles=npt)
    y_p, y_s = _moe(x3, norm_ffn[1], 1, moe_w_rg[1], moe_b_rg[1], moe_w_re[1], moe_b_re[1],
                    moe_w_gate, moe_w_up, moe_w_down,
                    final_g=norm_final.reshape(1, d), n_prompt_tiles=npt)
    y_prompt = y_p.reshape(batch, seq, d)
    y_sample = _rows_to_sample(y_s, gb, dec_seq)

    conv_prompt = conv_p[:, SUBLANES - (CONV_WIDTH - 1):, :][None]
    conv_sample = conv_s.reshape(n_sgroups, CONV_WIDTH - 1, gb, inner).transpose(0, 2, 1, 3)
    conv_sample = conv_sample.reshape(dec_batch, CONV_WIDTH - 1, inner)[None]
    v_sample = _rows_to_sample(vn[tp:], gb, dec_seq)[None]
    return (y_prompt, y_sample,
            c_p[None], n_p[:, :, 0, :][None], m_p[:, :, 0, 0][None], conv_prompt,
            c_s[None], n_s[:, :, 0, :][None], m_s[:, :, 0, 0][None], conv_sample,
            v_sample)
```

```python
import functools
import math

import jax
import jax.numpy as jnp
from jax import lax
from jax.experimental import pallas as pl
from jax.experimental.pallas import tpu as pltpu

F32 = jnp.float32
BF16 = jnp.bfloat16
I32 = jnp.int32

LANES = 128
SUBLANES = 8
ROW_TILE = 256
EXPERT_TILE = 512
MXU_WIDTH = 256
VMEM_LIMIT = 56 * 1024 * 1024
CONV_WIDTH = 4
QKV_BLOCK = 4
MLSTM_CHUNK_PROMPT = 256
MLSTM_CHUNK_SAMPLE = 128
CMLP_CHUNK = 128
STAB_INIT = -1e30
N_GATES = 8


def _cparams(*sem):
    return pltpu.CompilerParams(dimension_semantics=sem, vmem_limit_bytes=VMEM_LIMIT)


def _rms(x, g, eps=1e-6):
    return x * lax.rsqrt(jnp.mean(x * x, axis=-1, keepdims=True) + eps) * g


def _silu(x):
    hx = 0.5 * x
    return hx * jnp.tanh(hx) + hx


def _split3(x):
    p0 = x.astype(BF16)
    r = x - p0.astype(F32)
    p1 = r.astype(BF16)
    p2 = (r - p1.astype(F32)).astype(BF16)
    return p0, p1, p2


def _dot(a, b):
    return jnp.dot(a, b, preferred_element_type=F32)


def _dot_nt(a, b):
    return lax.dot_general(a, b, (((1,), (1,)), ((), ())), preferred_element_type=F32)


def _dot_tn(a, b):
    return lax.dot_general(a, b, (((0,), (0,)), ((), ())), preferred_element_type=F32)


def _dot_nt_split(a, b):
    a0, a1, _ = _split3(a)
    b0, b1, _ = _split3(b)
    return _dot_nt(a0, b0) + (_dot_nt(a0, b1) + _dot_nt(a1, b0))


def _norm_up_body(xp_ref, xs_ref, g_ref, w_ref, xm_ref, z_ref, *, n_prompt_tiles, inner):
    i = pl.program_id(0)
    x = jnp.where(i < n_prompt_tiles, xp_ref[...], xs_ref[...])
    h = _rms(x, g_ref[...]).astype(BF16)
    u = _dot(h, w_ref[...])
    xm_ref[...] = u[:, :inner]
    z_ref[...] = u[:, inner:].astype(BF16)


def _two_src_specs(n_prompt_tiles, width):
    sp = pl.BlockSpec((ROW_TILE, width), lambda i: (jnp.minimum(i, n_prompt_tiles - 1), 0))
    ss = pl.BlockSpec((ROW_TILE, width), lambda i: (jnp.maximum(i - n_prompt_tiles, 0), 0))
    return sp, ss


def _norm_up(xp, xs, g, w):
    tp, d = xp.shape
    ts = xs.shape[0]
    tt = tp + ts
    n2 = w.shape[1]
    inner = n2 // 2
    npt = tp // ROW_TILE
    sp, ss = _two_src_specs(npt, d)
    return pl.pallas_call(
        functools.partial(_norm_up_body, n_prompt_tiles=npt, inner=inner),
        grid=(tt // ROW_TILE,),
        in_specs=[sp, ss,
                  pl.BlockSpec((1, d), lambda i: (0, 0)),
                  pl.BlockSpec((d, n2), lambda i: (0, 0))],
        out_specs=[pl.BlockSpec((ROW_TILE, inner), lambda i: (i, 0)),
                   pl.BlockSpec((ROW_TILE, inner), lambda i: (i, 0))],
        out_shape=[jax.ShapeDtypeStruct((tt, inner), F32),
                   jax.ShapeDtypeStruct((tt, inner), BF16)],
        compiler_params=_cparams("parallel"),
        name="norm_up",
    )(xp, xs, g, w)


def _conv_qkv_body(xm_ref, st_ref, cw_ref, cb_ref, wqk_ref, wv_ref, wg_ref, bg_ref,
                   q_ref, k_ref, v_ref, xc_ref, gc_ref, bc_ref, st_out_ref,
                   *, row_stride, k_scale):
    t = pl.program_id(1)
    halo = st_ref.shape[1]
    rows, inner = xm_ref.shape

    @pl.when(t == 0)
    def _():
        st_out_ref[0] = st_ref[0]

    x = xm_ref[...]
    xe = jnp.concatenate([st_out_ref[0], x], axis=0)
    acc = x * cw_ref[CONV_WIDTH - 1:CONV_WIDTH, :] + cb_ref[...]
    for j in range(CONV_WIDTH - 1):
        lo = halo - (CONV_WIDTH - 1 - j) * row_stride
        acc = acc + xe[lo:lo + rows, :] * cw_ref[j:j + 1, :]
    st_out_ref[0] = xe[rows:rows + halo, :]
    xc = _silu(acc)
    xc_b = xc.astype(BF16)
    xm_b = x.astype(BF16)
    xc_ref[...] = xc_b

    gacc = jnp.zeros((rows, LANES), F32)
    for c in range(inner // MXU_WIDTH):
        sl = slice(c * MXU_WIDTH, (c + 1) * MXU_WIDTH)
        qk = _dot(xc_b[:, sl], wqk_ref[c])
        vv = _dot(xm_b[:, sl], wv_ref[c])
        qb = qk[:, :MXU_WIDTH].astype(BF16)
        kb = qk[:, MXU_WIDTH:].astype(BF16)
        vb = vv.astype(BF16)
        q_ref[:, sl] = qb
        k_ref[:, sl] = (qk[:, MXU_WIDTH:] * k_scale).astype(BF16)
        v_ref[:, sl] = vb
        gacc += _dot(qb, wg_ref[0, sl, :]) + _dot(kb, wg_ref[1, sl, :]) + _dot(vb, wg_ref[2, sl, :])
    g = gacc + bg_ref[...]
    lane = lax.broadcasted_iota(I32, g.shape, 1)
    logsig = jnp.minimum(g, 0.0) - jnp.log1p(jnp.exp(-jnp.abs(g)))
    gates = jnp.where(lane < N_GATES // 2, g, logsig)
    gc_ref[...] = gates
    bc_ref[...] = jnp.where(lane < N_GATES // 2, gates, _cumsum_rows(gates))


def _conv_qkv(xm, row_block_offset, state, cw, cb, wqk, wv, wg, bg, *, n_groups, tiles_per_group,
              row_stride, k_scale):
    inner = xm.shape[1]
    rows = n_groups * tiles_per_group * ROW_TILE
    halo = state.shape[1]
    nt = tiles_per_group
    row_spec = pl.BlockSpec((ROW_TILE, inner), lambda b, t: (b * nt + t, 0))
    gate_spec = pl.BlockSpec((ROW_TILE, LANES), lambda b, t: (b * nt + t, 0))
    full = lambda a: pl.BlockSpec(a.shape, lambda b, t: (0,) * a.ndim)
    return pl.pallas_call(
        functools.partial(_conv_qkv_body, row_stride=row_stride, k_scale=k_scale),
        grid=(n_groups, nt),
        in_specs=[pl.BlockSpec((ROW_TILE, inner), lambda b, t: (row_block_offset + b * nt + t, 0)),
                  pl.BlockSpec((1, halo, inner), lambda b, t: (b, 0, 0)),
                  full(cw), full(cb), full(wqk), full(wv), full(wg), full(bg)],
        out_specs=[row_spec, row_spec, row_spec, row_spec, gate_spec, gate_spec,
                   pl.BlockSpec((1, halo, inner), lambda b, t: (b, 0, 0))],
        out_shape=[jax.ShapeDtypeStruct((rows, inner), BF16)] * 4
        + [jax.ShapeDtypeStruct((rows, LANES), F32)] * 2
        + [jax.ShapeDtypeStruct((n_groups, halo, inner), F32)],
        compiler_params=_cparams("parallel", "arbitrary"),
        name="conv_qkv",
    )(xm, state, cw, cb, wqk, wv, wg, bg)


def _cumsum_rows(x):
    n = x.shape[0]
    tri = (lax.broadcasted_iota(I32, (n, n), 1) <= lax.broadcasted_iota(I32, (n, n), 0)).astype(BF16)
    p0, p1, p2 = _split3(x)
    return _dot(tri, p0) + _dot(tri, p1) + _dot(tri, p2)


def _cumsum_lanes(x):
    n = x.shape[1]
    tri = (lax.broadcasted_iota(I32, (n, n), 0) <= lax.broadcasted_iota(I32, (n, n), 1)).astype(BF16)
    p0, p1, p2 = _split3(x)
    return _dot(p0, tri) + _dot(p1, tri) + _dot(p2, tri)


def _mlstm_chunk(q, k, v, ig_col, bcum_col, ig_row, bcum_row, c_in, c_out, n_prev, m_prev):
    lq, lk = q.shape[0], k.shape[0]
    ri = lax.broadcasted_iota(I32, (lq, lk), 0)
    ci = lax.broadcasted_iota(I32, (lq, lk), 1)
    bcum_q = bcum_col[:lq]
    log_d = jnp.where(ci <= ri, bcum_q - bcum_row + ig_row, -jnp.inf)
    m_inter = bcum_q + m_prev
    m_t = jnp.maximum(m_inter, jnp.max(log_d, axis=-1, keepdims=True))
    s = _dot_nt(q, k) * jnp.exp(log_d - m_t)
    inter = jnp.exp(m_inter - m_t)
    m_new = m_t[lq - 1:lq, :]
    b_last = bcum_col[lk - 1:lk, :]
    w_col = jnp.exp(b_last - bcum_col + ig_col - m_new)
    decay = jnp.exp(b_last + m_prev - m_new)
    wk = w_col * k.astype(F32)
    c_prev = c_in[...]
    num = _dot(s.astype(BF16), v) + inter * _dot(q, c_prev.astype(BF16))
    c_out[...] = decay * c_prev + _dot_tn(wk.astype(BF16), v)
    den = jnp.sum(s, axis=-1, keepdims=True) + inter * jnp.sum(q.astype(F32) * n_prev, axis=-1, keepdims=True)
    h = num * (1.0 / jnp.maximum(jnp.abs(den), jnp.exp(-m_t)))
    n_new = decay * n_prev + jnp.sum(wk, axis=0, keepdims=True)
    return h, n_new, m_new


def _head_norm(h, w, eps=1e-5):
    mu = jnp.mean(h, axis=-1, keepdims=True)
    var = jnp.mean(jnp.square(h - mu), axis=-1, keepdims=True)
    return (h - mu) * lax.rsqrt(var + eps) * w


def _cell_prompt_body(q_ref, k_ref, v_ref, gc_ref, gr_ref, hw_ref,
                      h_ref, c_ref, n_ref, m_ref, *, heads):
    c = pl.program_id(1)

    @pl.when(c == 0)
    def _():
        c_ref[...] = jnp.zeros(c_ref.shape, F32)
        n_ref[...] = jnp.zeros(n_ref.shape, F32)
        m_ref[...] = jnp.full(m_ref.shape, STAB_INIT, F32)

    gc = gc_ref[...]
    gr = gr_ref[...]
    half = N_GATES // 2
    dh = q_ref.shape[1] // heads
    for hd in range(heads):
        sl = slice(hd * dh, (hd + 1) * dh)
        h, n_new, m_new = _mlstm_chunk(
            q_ref[:, sl], k_ref[:, sl], v_ref[:, sl],
            gc[:, hd:hd + 1], gc[:, hd + half:hd + half + 1],
            gr[hd:hd + 1, :], gr[hd + half:hd + half + 1, :],
            c_ref.at[0, hd], c_ref.at[0, hd], n_ref[0, hd], m_ref[0, hd, :, 0:1])
        h_ref[:, sl] = _head_norm(h, hw_ref[hd]).astype(BF16)
        n_ref[0, hd] = n_new
        m_ref[0, hd] = jnp.broadcast_to(m_new, m_ref.shape[2:])


def _cell_prompt(q, k, v, gc, gr, hn_w, *, batch, heads):
    tp, inner = q.shape
    dh = inner // heads
    seq = tp // batch
    ln = MLSTM_CHUNK_PROMPT
    nc = seq // ln
    qspec = pl.BlockSpec((ln, inner), lambda b, c: (b * nc + c, 0))
    return pl.pallas_call(
        functools.partial(_cell_prompt_body, heads=heads),
        grid=(batch, nc),
        in_specs=[qspec, qspec, qspec,
                  pl.BlockSpec((ln, LANES), lambda b, c: (b * nc + c, 0)),
                  pl.BlockSpec((SUBLANES, ln), lambda b, c: (0, b * nc + c)),
                  pl.BlockSpec((heads, 1, dh), lambda b, c: (0, 0, 0))],
        out_specs=[qspec,
                   pl.BlockSpec((1, heads, dh, dh), lambda b, c: (b, 0, 0, 0)),
                   pl.BlockSpec((1, heads, 1, dh), lambda b, c: (b, 0, 0, 0)),
                   pl.BlockSpec((1, heads, 1, LANES), lambda b, c: (b, 0, 0, 0))],
        out_shape=[jax.ShapeDtypeStruct((tp, inner), BF16),
                   jax.ShapeDtypeStruct((batch, heads, dh, dh), F32),
                   jax.ShapeDtypeStruct((batch, heads, 1, dh), F32),
                   jax.ShapeDtypeStruct((batch, heads, 1, LANES), F32)],
        compiler_params=_cparams("parallel", "arbitrary"),
        name="mlstm_cell_prompt",
    )(q, k, v, gc, gr, hn_w)


STATE_RING = 3


def _cell_sample_body(q_ref, k_ref, v_ref, gc_ref, gr_ref, hw_ref, c0_hbm, n0_ref, m0_ref,
                      h_ref, c_ref, n_ref, m_ref, c0_buf, c0_sem, *, heads, valid):
    b = pl.program_id(0)
    nb = pl.num_programs(0)

    def fetch(step):
        slot = step % STATE_RING
        return pltpu.make_async_copy(c0_hbm.at[step], c0_buf.at[slot], c0_sem.at[slot])

    @pl.when(b == 0)
    def _():
        for step in range(STATE_RING - 1):
            @pl.when(step < nb)
            def _():
                fetch(step).start()

    @pl.when(b + STATE_RING - 1 < nb)
    def _():
        fetch(b + STATE_RING - 1).start()

    fetch(b).wait()
    c0_ref = c0_buf.at[b % STATE_RING]
    ln = MLSTM_CHUNK_SAMPLE
    rows = q_ref.shape[1]
    dh = q_ref.shape[2] // heads
    half = N_GATES // 2
    gc = gc_ref[0]
    gr = gr_ref[0]
    row_c = lax.broadcasted_iota(I32, gc.shape, 0)
    lane_c = lax.broadcasted_iota(I32, gc.shape, 1)
    gc = jnp.where(row_c < valid, gc, jnp.where(lane_c < half, -jnp.inf, 0.0))
    row_r = lax.broadcasted_iota(I32, gr.shape, 0)
    lane_r = lax.broadcasted_iota(I32, gr.shape, 1)
    gr = jnp.where(lane_r < valid, gr, jnp.where(row_r < half, -jnp.inf, 0.0))
    bc = _cumsum_rows(jnp.where(lane_c < half, 0.0, gc))
    br = _cumsum_lanes(jnp.where(row_r < half, 0.0, gr))
    pad = jnp.zeros((ln - rows, dh), F32)

    def padded(ref, sl):
        return jnp.concatenate([ref[0, :, sl].astype(F32), pad], axis=0).astype(BF16)

    for hd in range(heads):
        sl = slice(hd * dh, (hd + 1) * dh)
        h, n_new, m_new = _mlstm_chunk(
            q_ref[0, :, sl], padded(k_ref, sl), padded(v_ref, sl),
            gc[:, hd:hd + 1], bc[:, hd + half:hd + half + 1],
            gr[hd:hd + 1, :], br[hd + half:hd + half + 1, :],
            c0_ref.at[hd], c_ref.at[0, hd], n0_ref[0, hd], m0_ref[0, hd, :, 0:1])
        h_ref[0, :, sl] = _head_norm(h, hw_ref[hd]).astype(BF16)
        n_ref[0, hd] = n_new
        m_ref[0, hd] = jnp.broadcast_to(m_new, m_ref.shape[2:])


def _cell_sample(q, k, v, gc, gr, hn_w, c0, n0, m0, *, heads, valid, group_batch):
    n_groups, rows, wide = q.shape
    inner = wide // group_batch
    dh = inner // heads
    nb = n_groups * group_batch
    ln = MLSTM_CHUNK_SAMPLE
    qspec = pl.BlockSpec((1, rows, inner), lambda b: (b // group_batch, 0, b % group_batch))
    cspec = pl.BlockSpec((1, heads, dh, dh), lambda b: (b, 0, 0, 0))
    nspec = pl.BlockSpec((1, heads, 1, dh), lambda b: (b, 0, 0, 0))
    mspec = pl.BlockSpec((1, heads, 1, LANES), lambda b: (b, 0, 0, 0))
    return pl.pallas_call(
        functools.partial(_cell_sample_body, heads=heads, valid=valid),
        grid=(nb,),
        in_specs=[qspec, qspec, qspec,
                  pl.BlockSpec((1, ln, LANES), lambda b: (b, 0, 0)),
                  pl.BlockSpec((1, SUBLANES, ln), lambda b: (b, 0, 0)),
                  pl.BlockSpec((heads, 1, dh), lambda b: (0, 0, 0)),
                  pl.BlockSpec(memory_space=pl.ANY), nspec, mspec],
        out_specs=[qspec, cspec, nspec, mspec],
        out_shape=[jax.ShapeDtypeStruct(q.shape, BF16),
                   jax.ShapeDtypeStruct((nb, heads, dh, dh), F32),
                   jax.ShapeDtypeStruct((nb, heads, 1, dh), F32),
                   jax.ShapeDtypeStruct((nb, heads, 1, LANES), F32)],
        scratch_shapes=[pltpu.VMEM((STATE_RING, heads, dh, dh), F32),
                        pltpu.SemaphoreType.DMA((STATE_RING,))],
        compiler_params=_cparams("arbitrary"),
        name="mlstm_cell_sample",
    )(q, k, v, gc, gr, hn_w, c0, n0, m0)


def _gate_down_body(hp_ref, hs_ref, cp_ref, cs_ref, z_ref, skip_ref, w_ref, xp_ref, xs_ref, o_ref,
                    *, n_prompt_tiles):
    i = pl.program_id(0)
    is_p = i < n_prompt_tiles
    hn = jnp.where(is_p, hp_ref[...], hs_ref[...]).astype(F32)
    xc = jnp.where(is_p, cp_ref[...], cs_ref[...]).astype(F32)
    xres = jnp.where(is_p, xp_ref[...], xs_ref[...])
    a = (hn + skip_ref[...] * xc) * _silu(z_ref[...].astype(F32))
    o_ref[...] = xres + _dot(a.astype(BF16), w_ref[...])


def _gate_down(hn_p, hn_s, xc_p, xc_s, z, skip, w, xp, xs):
    tp, inner = hn_p.shape
    tt = z.shape[0]
    d = w.shape[1]
    npt = tp // ROW_TILE
    ip, is_ = _two_src_specs(npt, inner)
    dp, ds = _two_src_specs(npt, d)
    return pl.pallas_call(
        functools.partial(_gate_down_body, n_prompt_tiles=npt),
        grid=(tt // ROW_TILE,),
        in_specs=[ip, is_, ip, is_,
                  pl.BlockSpec((ROW_TILE, inner), lambda i: (i, 0)),
                  pl.BlockSpec((1, inner), lambda i: (0, 0)),
                  pl.BlockSpec((inner, d), lambda i: (0, 0)),
                  dp, ds],
        out_specs=pl.BlockSpec((ROW_TILE, d), lambda i: (i, 0)),
        out_shape=jax.ShapeDtypeStruct((tt, d), F32),
        compiler_params=_cparams("parallel"),
        name="gate_down",
    )(hn_p, hn_s, xc_p, xc_s, z, skip, w, xp, xs)


def _cmlp_in_body(x_ref, g_ref, w_ref, b_ref, lw_ref, lb_ref, u_ref, v_ref, *, half):
    h = _rms(x_ref[...], g_ref[...]).astype(BF16)
    y = _dot(h, w_ref[...]) + b_ref[...]
    c = math.sqrt(2.0 / math.pi)
    hy = 0.5 * y
    zz = hy * jnp.tanh(y * (c + (0.044715 * c) * (y * y))) + hy
    u_ref[...] = zz[:, :half].astype(BF16)
    v = zz[:, half:]
    mu = jnp.mean(v, axis=-1, keepdims=True)
    var = jnp.mean(jnp.square(v - mu), axis=-1, keepdims=True)
    v_ref[...] = (v - mu) * lax.rsqrt(var + 1e-5) * lw_ref[...] + lb_ref[...]


def _cmlp_in(x, g, w, b, lw, lb):
    tt, d = x.shape
    n2 = w.shape[1]
    half = n2 // 2
    vec = lambda n: pl.BlockSpec((1, n), lambda i: (0, 0))
    return pl.pallas_call(
        functools.partial(_cmlp_in_body, half=half),
        grid=(tt // ROW_TILE,),
        in_specs=[pl.BlockSpec((ROW_TILE, d), lambda i: (i, 0)), vec(d),
                  pl.BlockSpec((d, n2), lambda i: (0, 0)), vec(n2), vec(half), vec(half)],
        out_specs=[pl.BlockSpec((ROW_TILE, half), lambda i: (i, 0))] * 2,
        out_shape=[jax.ShapeDtypeStruct((tt, half), BF16), jax.ShapeDtypeStruct((tt, half), F32)],
        compiler_params=_cparams("parallel"),
        name="cmlp_in",
    )(x, g, w, b, lw, lb)


def _cmlp_mix_out_body(u_ref, v_ref, mix_ref, bias_ref, w_ref, bo_ref, x_ref, o_ref, *, groups):
    half = u_ref.shape[1]
    gd = half // groups
    pieces = []
    for g in range(groups):
        sl = slice(g * gd, (g + 1) * gd)
        mixed = _dot(mix_ref[0, g], v_ref[:, sl].astype(BF16)) + bias_ref[0, :, g:g + 1]
        pieces.append((u_ref[:, sl].astype(F32) * mixed).astype(BF16))
    a = jnp.concatenate(pieces, axis=1)
    o_ref[...] = x_ref[...] + _dot(a, w_ref[...]) + bo_ref[...]


def _cmlp_mix_out(u, v, mix, bias, w, bo, x, *, n_prompt_tiles):
    tt, half = u.shape
    d = w.shape[1]
    groups = mix.shape[1]
    kind = lambda i: jnp.where(i < n_prompt_tiles, 0, 1)
    return pl.pallas_call(
        functools.partial(_cmlp_mix_out_body, groups=groups),
        grid=(tt // ROW_TILE,),
        in_specs=[pl.BlockSpec((ROW_TILE, half), lambda i: (i, 0)),
                  pl.BlockSpec((ROW_TILE, half), lambda i: (i, 0)),
                  pl.BlockSpec((1, groups, ROW_TILE, ROW_TILE), lambda i: (kind(i), 0, 0, 0)),
                  pl.BlockSpec((1, ROW_TILE, groups), lambda i: (kind(i), 0, 0)),
                  pl.BlockSpec((half, d), lambda i: (0, 0)),
                  pl.BlockSpec((1, d), lambda i: (0, 0)),
                  pl.BlockSpec((ROW_TILE, d), lambda i: (i, 0))],
        out_specs=pl.BlockSpec((ROW_TILE, d), lambda i: (i, 0)),
        out_shape=jax.ShapeDtypeStruct((tt, d), F32),
        compiler_params=_cparams("parallel"),
        name="cmlp_mix_out",
    )(u, v, mix, bias, w, bo, x)


GROUP_ROW_OFFSET = SUBLANES


def _first_argmax_rows(x, n):
    row = lax.broadcasted_iota(I32, x.shape, 0)
    mx = jnp.max(x, axis=0, keepdims=True)
    idx = jnp.min(jnp.where(x == mx, row, n), axis=0, keepdims=True)
    return idx, mx


def _router_body(x_ref, g_ref, wt_ref, bt_ref, route_ref, cnt_ref, *, n_groups, per_group):
    i = pl.program_id(0)
    n_exp = n_groups * per_group

    @pl.when(i == 0)
    def _():
        cnt_ref[...] = jnp.zeros(cnt_ref.shape, F32)

    h = _rms(x_ref[...], g_ref[...])
    logits = _dot_nt_split(wt_ref[...], h) + bt_ref[:, 0:1]
    tokens = logits.shape[1]
    gl = logits[0:n_groups, :]
    gmax = jnp.max(gl, axis=0, keepdims=True)
    ge = jnp.exp(gl - gmax)
    gp = ge / jnp.sum(ge, axis=0, keepdims=True)
    g_idx, g_w = _first_argmax_rows(gp, n_groups)
    e_sel = jnp.zeros((per_group, tokens), F32)
    for g in range(n_groups):
        lo = GROUP_ROW_OFFSET + g * per_group
        e_sel = e_sel + jnp.where(g_idx == g, logits[lo:lo + per_group, :], 0.0)
    i1, m1 = _first_argmax_rows(e_sel, per_group)
    row = lax.broadcasted_iota(I32, e_sel.shape, 0)
    i2, m2 = _first_argmax_rows(jnp.where(row == i1, -jnp.inf, e_sel), per_group)
    t = jnp.exp(m2 - m1)
    w1 = (1.0 / (1.0 + t)) * g_w
    w2 = (t / (1.0 + t)) * g_w
    e1 = g_idx * per_group + i1
    e2 = g_idx * per_group + i2

    erow = lax.broadcasted_iota(I32, (n_exp, tokens), 0)
    oh1 = erow == e1
    oh2 = erow == e2
    onehot = jnp.where(oh1 | oh2, 1.0, 0.0)
    strict_upper = (lax.broadcasted_iota(I32, (tokens, tokens), 0)
                    < lax.broadcasted_iota(I32, (tokens, tokens), 1)).astype(BF16)
    before = _dot(onehot.astype(BF16), strict_upper) + cnt_ref[:, 0:1]
    r1 = jnp.sum(jnp.where(oh1, before, 0.0), axis=0, keepdims=True)
    r2 = jnp.sum(jnp.where(oh2, before, 0.0), axis=0, keepdims=True)
    cnt_ref[...] = cnt_ref[...] + jnp.sum(onehot, axis=1, keepdims=True)
    zero = jnp.zeros_like(w1)
    route_ref[...] = jnp.concatenate(
        [e1.astype(F32), e2.astype(F32), r1, r2, w1, w2, zero, zero], axis=0)


def _wide_tile(n_rows, max_tiles):
    n = n_rows // ROW_TILE
    return ROW_TILE * max(k for k in range(1, max_tiles + 1) if n % k == 0)


def _router(x, g, wt, bt, *, n_groups, per_group):
    tt, d = x.shape
    n_exp = n_groups * per_group
    rt = _wide_tile(tt, 2)
    return pl.pallas_call(
        functools.partial(_router_body, n_groups=n_groups, per_group=per_group),
        grid=(tt // rt,),
        in_specs=[pl.BlockSpec((rt, d), lambda i: (i, 0)),
                  pl.BlockSpec((1, d), lambda i: (0, 0)),
                  pl.BlockSpec(wt.shape, lambda i: (0, 0)),
                  pl.BlockSpec(bt.shape, lambda i: (0, 0))],
        out_specs=[pl.BlockSpec((SUBLANES, rt), lambda i: (0, i)),
                   pl.BlockSpec((n_exp, LANES), lambda i: (0, 0))],
        out_shape=[jax.ShapeDtypeStruct((SUBLANES, tt), F32),
                   jax.ShapeDtypeStruct((n_exp, LANES), F32)],
        compiler_params=_cparams("arbitrary"),
        name="moe_router",
    )(x, g, wt, bt)


def _positions_body(route_ref, off_ref, pos_ref):
    r = route_ref[...]
    n_exp = off_ref.shape[0]
    tokens = r.shape[1]
    erow = lax.broadcasted_iota(I32, (n_exp, tokens), 0)
    off = off_ref[:, 0:1]
    out = []
    for k in range(2):
        e = r[k:k + 1, :].astype(I32)
        base = jnp.sum(jnp.where(erow == e, off, 0.0), axis=0, keepdims=True)
        out.append((base + r[2 + k:3 + k, :]).astype(I32))
    pos_ref[...] = jnp.concatenate(out, axis=0)


def _positions(route, off):
    tt = route.shape[1]
    pt = _wide_tile(tt, 16)
    return pl.pallas_call(
        _positions_body,
        grid=(tt // pt,),
        in_specs=[pl.BlockSpec((SUBLANES, pt), lambda i: (0, i)),
                  pl.BlockSpec(off.shape, lambda i: (0, 0))],
        out_specs=pl.BlockSpec((2, pt), lambda i: (0, i)),
        out_shape=jax.ShapeDtypeStruct((2, tt), I32),
        compiler_params=_cparams("parallel"),
        name="moe_positions",
    )(route, off)


def _for_each_row(n_rows, fn):
    def group(g, carry):
        start = pl.multiple_of(g * SUBLANES, SUBLANES)
        for u in range(SUBLANES):
            fn(start, u)
        return carry

    lax.fori_loop(0, n_rows // SUBLANES, group, 0, unroll=2)


N_DMA_PRIORITIES = 2


def _start_alternating(copies):
    for i, cp in enumerate(copies):
        cp.start(priority=i % N_DMA_PRIORITIES)


def _tile_row(ref, group_start, u):
    return ref.at[pl.ds(group_start, SUBLANES)].at[pl.ds(u, 1)]


def _dispatch_body(pos_ref, x_ref, xs_hbm, sem, *, n_tokens):
    base = pl.program_id(0) * ROW_TILE

    def copies(start, u):
        return [pltpu.make_async_copy(_tile_row(x_ref, start, u),
                                      xs_hbm.at[pl.ds(pos_ref[k * n_tokens + base + start + u], 1)], sem)
                for k in range(2)]

    _for_each_row(ROW_TILE, lambda start, u: _start_alternating(copies(start, u)))
    _for_each_row(ROW_TILE, lambda start, u: [cp.wait() for cp in copies(start, u)])


def _dispatch(pos_flat, x):
    tt, d = x.shape
    return pl.pallas_call(
        functools.partial(_dispatch_body, n_tokens=tt),
        grid_spec=pltpu.PrefetchScalarGridSpec(
            num_scalar_prefetch=1,
            grid=(tt // ROW_TILE,),
            in_specs=[pl.BlockSpec((ROW_TILE, d), lambda i, p: (i, 0))],
            out_specs=pl.BlockSpec(memory_space=pl.ANY),
            scratch_shapes=[pltpu.SemaphoreType.DMA]),
        out_shape=jax.ShapeDtypeStruct((2 * tt, d), F32),
        compiler_params=_cparams("arbitrary"),
        name="moe_dispatch",
    )(pos_flat, x)


def _experts_body(tile_ref, exp_ref, lo_ref, hi_ref, nwork_ref,
                  xs_ref, g_ref, wg_ref, wu_ref, wd_ref, o_ref, wg_b, wu_b, wd_b):
    w = pl.program_id(0)
    prev = jnp.maximum(w - 1, 0)
    first = jnp.logical_or(w == 0, tile_ref[w] != tile_ref[prev])
    new_expert = jnp.logical_or(w == 0, exp_ref[w] != exp_ref[prev])

    @pl.when(w < nwork_ref[0])
    def _():
        @pl.when(new_expert)
        def _():
            wg_b[...] = wg_ref[0].astype(BF16)
            wu_b[...] = wu_ref[0].astype(BF16)
            wd_b[...] = wd_ref[0].astype(BF16)

        xb = _rms(xs_ref[...], g_ref[...]).astype(BF16)
        a = _dot(xb, wg_b[...])
        b = _dot(xb, wu_b[...])
        hg = (_silu(a) * b).astype(BF16)
        y = _dot(hg, wd_b[...])
        lo, hi = lo_ref[w], hi_ref[w]
        whole = jnp.logical_and(lo == 0, hi == EXPERT_TILE)
        row = lax.broadcasted_iota(I32, y.shape, 0)
        mine = (row >= lo) & (row < hi)

        @pl.when(whole)
        def _():
            o_ref[...] = y

        @pl.when(jnp.logical_and(first, jnp.logical_not(whole)))
        def _():
            o_ref[...] = jnp.where(mine, y, 0.0)

        @pl.when(jnp.logical_not(first))
        def _():
            o_ref[...] = jnp.where(mine, y, o_ref[...])


def _experts(meta, xs, g, wg, wu, wd):
    rows, d = xs.shape
    n_work = meta[0].shape[0]
    f = wg.shape[2]
    return pl.pallas_call(
        _experts_body,
        grid_spec=pltpu.PrefetchScalarGridSpec(
            num_scalar_prefetch=5,
            grid=(n_work,),
            in_specs=[pl.BlockSpec((EXPERT_TILE, d), lambda w, tl, ex, lo, hi, nw: (tl[w], 0)),
                      pl.BlockSpec((1, d), lambda w, tl, ex, lo, hi, nw: (0, 0)),
                      pl.BlockSpec((1, d, f), lambda w, tl, ex, lo, hi, nw: (ex[w], 0, 0)),
                      pl.BlockSpec((1, d, f), lambda w, tl, ex, lo, hi, nw: (ex[w], 0, 0)),
                      pl.BlockSpec((1, f, d), lambda w, tl, ex, lo, hi, nw: (ex[w], 0, 0))],
            out_specs=pl.BlockSpec((EXPERT_TILE, d), lambda w, tl, ex, lo, hi, nw: (tl[w], 0)),
            scratch_shapes=[pltpu.VMEM((d, f), BF16), pltpu.VMEM((d, f), BF16), pltpu.VMEM((f, d), BF16)]),
        out_shape=jax.ShapeDtypeStruct((rows, d), F32),
        compiler_params=_cparams("arbitrary"),
        name="moe_experts",
    )(*meta, xs, g, wg, wu, wd)


COMBINE_SLOTS = 2


def _combine_rows(pos_ref, x_ref, route_ref, ys_hbm, buf, sem, n_tokens):
    i = pl.program_id(0)

    def copies(tile, start, u):
        slot = tile % COMBINE_SLOTS
        t = tile * ROW_TILE + start + u
        return [pltpu.make_async_copy(ys_hbm.at[pl.ds(pos_ref[k * n_tokens + t], 1)],
                                      _tile_row(buf.at[slot, k], start, u), sem.at[slot])
                for k in range(2)]

    def request(tile):
        _for_each_row(ROW_TILE, lambda start, u: _start_alternating(copies(tile, start, u)))

    @pl.when(i == 0)
    def _():
        request(0)

    @pl.when(i + 1 < pl.num_programs(0))
    def _():
        request(i + 1)

    r = route_ref[...]
    rt = jnp.concatenate([r, jnp.zeros((LANES - SUBLANES, r.shape[1]), F32)], axis=0).T
    _for_each_row(ROW_TILE, lambda start, u: [cp.wait() for cp in copies(i, start, u)])
    slot = i % COMBINE_SLOTS
    return x_ref[...] + rt[:, 4:5] * buf[slot, 0] + rt[:, 5:6] * buf[slot, 1]


def _combine_body(pos_ref, x_ref, route_ref, ys_hbm, o_ref, buf, sem, *, n_tokens):
    o_ref[...] = _combine_rows(pos_ref, x_ref, route_ref, ys_hbm, buf, sem, n_tokens)


def _combine_norm_body(pos_ref, x_ref, route_ref, g_ref, ys_hbm, op_ref, os_ref, buf, sem,
                       *, n_tokens, n_prompt_tiles):
    y = _rms(_combine_rows(pos_ref, x_ref, route_ref, ys_hbm, buf, sem, n_tokens), g_ref[...])
    is_prompt = pl.program_id(0) < n_prompt_tiles

    @pl.when(is_prompt)
    def _():
        op_ref[...] = y

    @pl.when(jnp.logical_not(is_prompt))
    def _():
        os_ref[...] = y


def _combine(pos_flat, x, route, ys, final_g=None, n_prompt_tiles=None):
    tt, d = x.shape
    in_specs = [pl.BlockSpec((ROW_TILE, d), lambda i, p: (i, 0)),
                pl.BlockSpec((SUBLANES, ROW_TILE), lambda i, p: (0, i))]
    args = [pos_flat, x, route]
    if final_g is None:
        body = functools.partial(_combine_body, n_tokens=tt)
        out_specs = pl.BlockSpec((ROW_TILE, d), lambda i, p: (i, 0))
        out_shape = jax.ShapeDtypeStruct((tt, d), F32)
    else:
        npt = n_prompt_tiles
        body = functools.partial(_combine_norm_body, n_tokens=tt, n_prompt_tiles=npt)
        in_specs.append(pl.BlockSpec((1, d), lambda i, p: (0, 0)))
        args.append(final_g)
        out_specs = [pl.BlockSpec((ROW_TILE, d), lambda i, p: (jnp.minimum(i, npt - 1), 0)),
                     pl.BlockSpec((ROW_TILE, d), lambda i, p: (jnp.maximum(i - npt, 0), 0))]
        out_shape = [jax.ShapeDtypeStruct((npt * ROW_TILE, d), F32),
                     jax.ShapeDtypeStruct((tt - npt * ROW_TILE, d), F32)]
    in_specs.append(pl.BlockSpec(memory_space=pl.ANY))
    args.append(ys)
    return pl.pallas_call(
        body,
        grid_spec=pltpu.PrefetchScalarGridSpec(
            num_scalar_prefetch=1,
            grid=(tt // ROW_TILE,),
            in_specs=in_specs,
            out_specs=out_specs,
            scratch_shapes=[pltpu.VMEM((COMBINE_SLOTS, 2, ROW_TILE, d), F32),
                            pltpu.SemaphoreType.DMA((COMBINE_SLOTS,))]),
        out_shape=out_shape,
        compiler_params=_cparams("arbitrary"),
        name="moe_combine",
    )(*args)


def _work_items(counts, n_rows):
    n_exp = counts.shape[0]
    n_tiles = n_rows // EXPERT_TILE
    n_work = n_tiles + n_exp - 1
    ids = jnp.arange(n_exp, dtype=I32)
    lower = ids[None, :] <= ids[:, None]

    def cumsum(v):
        return jnp.sum(jnp.where(lower, v[None, :], 0), axis=1)

    end = cumsum(counts)
    off = end - counts
    first_tile = off // EXPERT_TILE
    last_tile = (end - 1) // EXPERT_TILE
    per_exp = jnp.where(counts > 0, last_tile - first_tile + 1, 0)
    wend = cumsum(per_exp)
    wstart = wend - per_exp
    total = jnp.sum(per_exp)
    w = jnp.arange(n_work, dtype=I32)
    e = jnp.minimum(jnp.sum((wend[None, :] <= w[:, None]).astype(I32), axis=1), n_exp - 1)
    sel = e[:, None] == ids[None, :]

    def pick(v):
        return jnp.sum(jnp.where(sel, v[None, :], 0), axis=1)

    tile = pick(first_tile) + (w - pick(wstart))
    lo = jnp.maximum(pick(off), tile * EXPERT_TILE) - tile * EXPERT_TILE
    hi = jnp.minimum(pick(end), (tile + 1) * EXPERT_TILE) - tile * EXPERT_TILE
    live = w < total
    tile = jnp.where(live, tile, n_tiles - 1)
    lo = jnp.where(live, lo, 0)
    hi = jnp.where(live, hi, 0)
    return off, (tile.astype(I32), e.astype(I32), lo.astype(I32), hi.astype(I32),
                 total.reshape(1).astype(I32))


def _moe(x, g, layer, w_rg, b_rg, w_re, b_re, w_gate, w_up, w_down, final_g=None, n_prompt_tiles=None):
    tt, d = x.shape
    n_groups, per_group = w_re.shape[0], w_re.shape[2]
    n_exp = n_groups * per_group
    f = w_gate.shape[-1]
    pad_rows = jnp.zeros((GROUP_ROW_OFFSET - n_groups, d), F32)
    wt = jnp.concatenate([w_rg.T, pad_rows, jnp.transpose(w_re, (0, 2, 1)).reshape(n_exp, d)], axis=0)
    bt = jnp.concatenate([b_rg, jnp.zeros((GROUP_ROW_OFFSET - n_groups,), F32), b_re.reshape(n_exp)])
    bt = jnp.broadcast_to(bt[:, None], (wt.shape[0], LANES))
    g2 = g.reshape(1, d)

    route, cnt = _router(x, g2, wt, bt, n_groups=n_groups, per_group=per_group)
    counts = cnt[:, 0].astype(I32)
    off, (tile, exp, lo, hi, total) = _work_items(counts, 2 * tt)
    off_b = jnp.broadcast_to(off.astype(F32)[:, None], (n_exp, LANES))
    pos = _positions(route, off_b).reshape(2 * tt)
    xs = _dispatch(pos, x)
    ys = _experts((tile, exp + layer * n_exp, lo, hi, total), xs, g2,
                  w_gate.reshape(-1, d, f), w_up.reshape(-1, d, f), w_down.reshape(-1, f, d))
    return _combine(pos, x, route, ys, final_g, n_prompt_tiles)


def _block_diag_tiles(w, tile):
    nb, c, _ = w.shape
    rows = w.reshape(nb * c // tile, tile, c)
    blk = jnp.arange(tile, dtype=I32) // c
    return jnp.where(blk[:, None] == blk[None, :], jnp.tile(rows, (1, 1, tile // c)), 0.0)


def _sample_to_rows(x, group_batch):
    db, s, w = x.shape
    return x.reshape(db // group_batch, group_batch, s, w).transpose(0, 2, 1, 3).reshape(db * s, w)


def _rows_to_sample(x, group_batch, s):
    rows, w = x.shape
    db = rows // s
    return x.reshape(db // group_batch, s, group_batch, w).transpose(0, 2, 1, 3).reshape(db, s, w)


def kernel(x_prompt, x_sample, state_mlstm_C, state_mlstm_n, state_mlstm_m, state_mlstm_conv,
           norm_mix, norm_ffn, norm_final,
           mlstm_w_up, mlstm_conv_w, mlstm_conv_b, mlstm_w_q, mlstm_w_k, mlstm_w_v,
           mlstm_w_ig, mlstm_b_ig, mlstm_w_fg, mlstm_b_fg, mlstm_skip, mlstm_hn_w, mlstm_w_down,
           cmlp_w_in, cmlp_b_in, cmlp_ln_w, cmlp_ln_b, cmlp_w_s, cmlp_b_s, cmlp_w_out, cmlp_b_out,
           moe_w_rg, moe_b_rg, moe_w_re, moe_b_re, moe_w_gate, moe_w_up, moe_w_down):
    batch, seq, d = x_prompt.shape
    dec_batch, dec_seq, _ = x_sample.shape
    heads = state_mlstm_C.shape[2]
    dh = state_mlstm_C.shape[3]
    inner = heads * dh
    tp, ts = batch * seq, dec_batch * dec_seq
    npt = tp // ROW_TILE
    gb = ROW_TILE // dec_seq
    n_sgroups = dec_batch // gb
    half_g = N_GATES // 2
    assert heads == half_g and tp % ROW_TILE == 0 and ts % ROW_TILE == 0
    assert ROW_TILE % dec_seq == 0 and dec_batch % gb == 0 and seq % MLSTM_CHUNK_PROMPT == 0
    assert seq % ROW_TILE == 0 and ROW_TILE % CMLP_CHUNK == 0 and dec_seq <= SUBLANES
    assert MLSTM_CHUNK_PROMPT == ROW_TILE

    xp = x_prompt.reshape(tp, d)
    xs = _sample_to_rows(x_sample, gb)

    xm, z = _norm_up(xp, xs, norm_mix[0].reshape(1, d), mlstm_w_up[0].astype(BF16))

    wq = _block_diag_tiles(mlstm_w_q[0], MXU_WIDTH)
    wk = _block_diag_tiles(mlstm_w_k[0], MXU_WIDTH)
    wqk = jnp.concatenate([wq, wk], axis=2).astype(BF16)
    wv = _block_diag_tiles(mlstm_w_v[0], MXU_WIDTH).astype(BF16)
    wg = jnp.concatenate([mlstm_w_ig[0], mlstm_w_fg[0]], axis=1)
    wg = jnp.pad(wg, ((0, 0), (0, LANES - N_GATES))).reshape(3, inner, LANES).astype(BF16)
    bg = jnp.pad(jnp.concatenate([mlstm_b_ig[0], mlstm_b_fg[0]]), (0, LANES - N_GATES)).reshape(1, LANES)
    cw, cb = mlstm_conv_w[0], mlstm_conv_b[0].reshape(1, inner)
    k_scale = float(dh) ** -0.5

    conv0_p = jnp.zeros((batch, SUBLANES, inner), F32)
    q_p, k_p, v_p, xc_p, _, bc_p, conv_p = _conv_qkv(
        xm, 0, conv0_p, cw, cb, wqk, wv, wg, bg,
        n_groups=batch, tiles_per_group=seq // ROW_TILE, row_stride=1, k_scale=k_scale)
    halo_s = (CONV_WIDTH - 1) * gb
    conv0_s = state_mlstm_conv[0].reshape(n_sgroups, gb, CONV_WIDTH - 1, inner)
    conv0_s = conv0_s.transpose(0, 2, 1, 3).reshape(n_sgroups, halo_s, inner)
    q_s, k_s, v_s, xc_s, gc_s, _, conv_s = _conv_qkv(
        xm, npt, conv0_s, cw, cb, wqk, wv, wg, bg,
        n_groups=n_sgroups, tiles_per_group=1, row_stride=gb, k_scale=k_scale)

    hn_w = mlstm_hn_w[0].reshape(heads, 1, dh)
    br_p = jnp.transpose(bc_p[:, :SUBLANES])
    hn_p, c_p, n_p, m_p = _cell_prompt(q_p, k_p, v_p, bc_p, br_p, hn_w, batch=batch, heads=heads)

    def pad_steps(a):
        a = a.reshape(n_sgroups, dec_seq, gb * inner)
        return jnp.pad(a, ((0, 0), (0, SUBLANES - dec_seq), (0, 0)))

    ln_s = MLSTM_CHUNK_SAMPLE
    gcs = gc_s.reshape(n_sgroups, dec_seq, gb, LANES).transpose(0, 2, 1, 3).reshape(dec_batch, dec_seq, LANES)
    gcs = jnp.pad(gcs, ((0, 0), (0, ln_s - dec_seq), (0, 0)))
    grs = jnp.transpose(gcs[:, :, :SUBLANES], (0, 2, 1))
    m0 = jnp.broadcast_to(state_mlstm_m[0][:, :, None, None], (dec_batch, heads, 1, LANES))
    hn_s, c_s, n_s, m_s = _cell_sample(
        pad_steps(q_s), pad_steps(k_s), pad_steps(v_s), gcs, grs, hn_w,
        state_mlstm_C[0], state_mlstm_n[0].reshape(dec_batch, heads, 1, dh), m0,
        heads=heads, valid=dec_seq, group_batch=gb)
    hn_s = hn_s[:, :dec_seq].reshape(ts, inner)

    x1 = _gate_down(hn_p, hn_s, xc_p, xc_s, z, mlstm_skip[0].reshape(1, inner),
                    mlstm_w_down[0].astype(BF16), xp, xs)
    x2 = _moe(x1, norm_ffn[0], 0, moe_w_rg[0], moe_b_rg[0], moe_w_re[0], moe_b_re[0],
              moe_w_gate, moe_w_up, moe_w_down)

    half = cmlp_w_in.shape[2] // 2
    groups = cmlp_w_s.shape[1]
    u, vn = _cmlp_in(x2, norm_mix[1].reshape(1, d), cmlp_w_in[0].astype(BF16),
                     cmlp_b_in[0].reshape(1, 2 * half), cmlp_ln_w[0].reshape(1, half),
                     cmlp_ln_b[0].reshape(1, half))
    causal = jnp.tril(jnp.ones((CMLP_CHUNK, CMLP_CHUNK), dtype=bool))
    ws = jnp.where(causal, cmlp_w_s[0], 0.0)
    reps = ROW_TILE // CMLP_CHUNK
    rid = jnp.arange(ROW_TILE, dtype=I32)
    same_chunk = (rid[:, None] // CMLP_CHUNK) == (rid[None, :] // CMLP_CHUNK)
    same_seq = (rid[:, None] % gb) == (rid[None, :] % gb)
    mix_p = jnp.where(same_chunk, jnp.tile(ws, (1, reps, reps)), 0.0)
    ws_s = ws[:, :dec_seq, :dec_seq]
    mix_s = jnp.where(same_seq, jnp.repeat(jnp.repeat(ws_s, gb, axis=1), gb, axis=2), 0.0)
    mix = jnp.stack([mix_p, mix_s]).astype(BF16)
    bias_t = jnp.transpose(cmlp_b_s[0])
    bias = jnp.stack([jnp.tile(bias_t, (reps, 1)), jnp.repeat(bias_t[:dec_seq], gb, axis=0)])
    x3 = _cmlp_mix_out(u, vn, mix, bias, cmlp_w_out[0].astype(BF16), cmlp_b_out[0].reshape(1, d), x2,
                       n_prompt_tiles=npt)
    y_p, y_s = _moe(x3, norm_ffn[1], 1, moe_w_rg[1], moe_b_rg[1], moe_w_re[1], moe_b_re[1],
                    moe_w_gate, moe_w_up, moe_w_down,
                    final_g=norm_final.reshape(1, d), n_prompt_tiles=npt)
    y_prompt = y_p.reshape(batch, seq, d)
    y_sample = _rows_to_sample(y_s, gb, dec_seq)

    conv_prompt = conv_p[:, SUBLANES - (CONV_WIDTH - 1):, :][None]
    conv_sample = conv_s.reshape(n_sgroups, CONV_WIDTH - 1, gb, inner).transpose(0, 2, 1, 3)
    conv_sample = conv_sample.reshape(dec_batch, CONV_WIDTH - 1, inner)[None]
    v_sample = _rows_to_sample(vn[tp:], gb, dec_seq)[None]
    return (y_prompt, y_sample,
            c_p[None], n_p[:, :, 0, :][None], m_p[:, :, 0, 0][None], conv_prompt,
            c_s[None], n_s[:, :, 0, :][None], m_s[:, :, 0, 0][None], conv_sample,
            v_sample)
```

```python
import functools
import math

import jax
import jax.numpy as jnp
from jax import lax
from jax.experimental import pallas as pl
from jax.experimental.pallas import tpu as pltpu

F32 = jnp.float32
BF16 = jnp.bfloat16
I32 = jnp.int32

LANES = 128
SUBLANES = 8
ROW_TILE = 256
EXPERT_TILE = 512
GATE_DOWN_TILE = 512
MXU_WIDTH = 256
VMEM_LIMIT = 56 * 1024 * 1024
CONV_WIDTH = 4
QKV_BLOCK = 4
MLSTM_CHUNK_PROMPT = 256
MLSTM_CHUNK_SAMPLE = 8
CMLP_CHUNK = 128
STAB_INIT = -1e30
N_GATES = 8


def _cparams(*sem):
    return pltpu.CompilerParams(dimension_semantics=sem, vmem_limit_bytes=VMEM_LIMIT)


def _rms(x, g, eps=1e-6):
    return x * lax.rsqrt(jnp.mean(x * x, axis=-1, keepdims=True) + eps) * g


def _silu(x):
    hx = 0.5 * x
    return hx * jnp.tanh(hx) + hx


def _split3(x):
    p0 = x.astype(BF16)
    r = x - p0.astype(F32)
    p1 = r.astype(BF16)
    p2 = (r - p1.astype(F32)).astype(BF16)
    return p0, p1, p2


def _dot(a, b):
    return jnp.dot(a, b, preferred_element_type=F32)


def _dot_nt(a, b):
    return lax.dot_general(a, b, (((1,), (1,)), ((), ())), preferred_element_type=F32)


def _dot_tn(a, b):
    return lax.dot_general(a, b, (((0,), (0,)), ((), ())), preferred_element_type=F32)


def _dot_nt_split(a, b):
    a0, a1, _ = _split3(a)
    b0, b1, _ = _split3(b)
    return _dot_nt(a0, b0) + (_dot_nt(a0, b1) + _dot_nt(a1, b0))


def _norm_up_body(xp_ref, xs_ref, g_ref, w_ref, xm_ref, z_ref, *, n_prompt_tiles, inner):
    i = pl.program_id(0)
    x = jnp.where(i < n_prompt_tiles, xp_ref[...], xs_ref[...])
    h = _rms(x, g_ref[...]).astype(BF16)
    u = _dot(h, w_ref[...])
    xm_ref[...] = u[:, :inner]
    z_ref[...] = u[:, inner:].astype(BF16)


def _two_src_specs(n_prompt_tiles, width, rows=ROW_TILE):
    sp = pl.BlockSpec((rows, width), lambda i: (jnp.minimum(i, n_prompt_tiles - 1), 0))
    ss = pl.BlockSpec((rows, width), lambda i: (jnp.maximum(i - n_prompt_tiles, 0), 0))
    return sp, ss


def _norm_up(xp, xs, g, w):
    tp, d = xp.shape
    ts = xs.shape[0]
    tt = tp + ts
    n2 = w.shape[1]
    inner = n2 // 2
    npt = tp // ROW_TILE
    sp, ss = _two_src_specs(npt, d)
    return pl.pallas_call(
        functools.partial(_norm_up_body, n_prompt_tiles=npt, inner=inner),
        grid=(tt // ROW_TILE,),
        in_specs=[sp, ss,
                  pl.BlockSpec((1, d), lambda i: (0, 0)),
                  pl.BlockSpec((d, n2), lambda i: (0, 0))],
        out_specs=[pl.BlockSpec((ROW_TILE, inner), lambda i: (i, 0)),
                   pl.BlockSpec((ROW_TILE, inner), lambda i: (i, 0))],
        out_shape=[jax.ShapeDtypeStruct((tt, inner), F32),
                   jax.ShapeDtypeStruct((tt, inner), BF16)],
        compiler_params=_cparams("parallel"),
        name="norm_up",
    )(xp, xs, g, w)


def _conv_qkv_body(xm_ref, st_ref, cw_ref, cb_ref, wqk_ref, wv_ref, wg_ref, bg_ref,
                   q_ref, k_ref, v_ref, xc_ref, gc_ref, bc_ref, st_out_ref,
                   *, row_stride, k_scale):
    t = pl.program_id(1)
    halo = st_ref.shape[1]
    rows, inner = xm_ref.shape

    @pl.when(t == 0)
    def _():
        st_out_ref[0] = st_ref[0]

    x = xm_ref[...]
    xe = jnp.concatenate([st_out_ref[0], x], axis=0)
    acc = x * cw_ref[CONV_WIDTH - 1:CONV_WIDTH, :] + cb_ref[...]
    for j in range(CONV_WIDTH - 1):
        lo = halo - (CONV_WIDTH - 1 - j) * row_stride
        acc = acc + xe[lo:lo + rows, :] * cw_ref[j:j + 1, :]
    st_out_ref[0] = xe[rows:rows + halo, :]
    xc = _silu(acc)
    xc_b = xc.astype(BF16)
    xm_b = x.astype(BF16)
    xc_ref[...] = xc_b

    gacc = jnp.zeros((rows, LANES), F32)
    for c in range(inner // MXU_WIDTH):
        sl = slice(c * MXU_WIDTH, (c + 1) * MXU_WIDTH)
        qk = _dot(xc_b[:, sl], wqk_ref[c])
        vv = _dot(xm_b[:, sl], wv_ref[c])
        qb = qk[:, :MXU_WIDTH].astype(BF16)
        kb = qk[:, MXU_WIDTH:].astype(BF16)
        vb = vv.astype(BF16)
        q_ref[:, sl] = qb
        k_ref[:, sl] = (qk[:, MXU_WIDTH:] * k_scale).astype(BF16)
        v_ref[:, sl] = vb
        gacc += _dot(qb, wg_ref[0, sl, :]) + _dot(kb, wg_ref[1, sl, :]) + _dot(vb, wg_ref[2, sl, :])
    g = gacc + bg_ref[...]
    lane = lax.broadcasted_iota(I32, g.shape, 1)
    logsig = jnp.minimum(g, 0.0) - jnp.log1p(jnp.exp(-jnp.abs(g)))
    gates = jnp.where(lane < N_GATES // 2, g, logsig)
    gc_ref[...] = gates
    bc_ref[...] = jnp.where(lane < N_GATES // 2, gates, _cumsum_rows(gates))


def _conv_qkv(xm, row_block_offset, state, cw, cb, wqk, wv, wg, bg, *, n_groups, tiles_per_group,
              row_stride, k_scale):
    inner = xm.shape[1]
    rows = n_groups * tiles_per_group * ROW_TILE
    halo = state.shape[1]
    nt = tiles_per_group
    row_spec = pl.BlockSpec((ROW_TILE, inner), lambda b, t: (b * nt + t, 0))
    gate_spec = pl.BlockSpec((ROW_TILE, LANES), lambda b, t: (b * nt + t, 0))
    full = lambda a: pl.BlockSpec(a.shape, lambda b, t: (0,) * a.ndim)
    return pl.pallas_call(
        functools.partial(_conv_qkv_body, row_stride=row_stride, k_scale=k_scale),
        grid=(n_groups, nt),
        in_specs=[pl.BlockSpec((ROW_TILE, inner), lambda b, t: (row_block_offset + b * nt + t, 0)),
                  pl.BlockSpec((1, halo, inner), lambda b, t: (b, 0, 0)),
                  full(cw), full(cb), full(wqk), full(wv), full(wg), full(bg)],
        out_specs=[row_spec, row_spec, row_spec, row_spec, gate_spec, gate_spec,
                   pl.BlockSpec((1, halo, inner), lambda b, t: (b, 0, 0))],
        out_shape=[jax.ShapeDtypeStruct((rows, inner), BF16)] * 4
        + [jax.ShapeDtypeStruct((rows, LANES), F32)] * 2
        + [jax.ShapeDtypeStruct((n_groups, halo, inner), F32)],
        compiler_params=_cparams("parallel", "arbitrary"),
        name="conv_qkv",
    )(xm, state, cw, cb, wqk, wv, wg, bg)


def _cumsum_rows(x):
    n = x.shape[0]
    tri = (lax.broadcasted_iota(I32, (n, n), 1) <= lax.broadcasted_iota(I32, (n, n), 0)).astype(BF16)
    p0, p1, p2 = _split3(x)
    return _dot(tri, p0) + _dot(tri, p1) + _dot(tri, p2)


def _cumsum_lanes(x):
    n = x.shape[1]
    tri = (lax.broadcasted_iota(I32, (n, n), 0) <= lax.broadcasted_iota(I32, (n, n), 1)).astype(BF16)
    p0, p1, p2 = _split3(x)
    return _dot(p0, tri) + _dot(p1, tri) + _dot(p2, tri)


def _mlstm_chunk(q, k, v, ig_col, bcum_col, ig_row, bcum_row, c_in, c_out, n_prev, m_prev):
    lq, lk = q.shape[0], k.shape[0]
    ri = lax.broadcasted_iota(I32, (lq, lk), 0)
    ci = lax.broadcasted_iota(I32, (lq, lk), 1)
    bcum_q = bcum_col[:lq]
    log_d = jnp.where(ci <= ri, bcum_q - bcum_row + ig_row, -jnp.inf)
    m_inter = bcum_q + m_prev
    m_t = jnp.maximum(m_inter, jnp.max(log_d, axis=-1, keepdims=True))
    s = _dot_nt(q, k) * jnp.exp(log_d - m_t)
    inter = jnp.exp(m_inter - m_t)
    m_new = m_t[lq - 1:lq, :]
    b_last = bcum_col[lk - 1:lk, :]
    w_col = jnp.exp(b_last - bcum_col + ig_col - m_new)
    decay = jnp.exp(b_last + m_prev - m_new)
    wk = w_col * k.astype(F32)
    c_prev = c_in[...]
    num = _dot(s.astype(BF16), v) + inter * _dot(q, c_prev.astype(BF16))
    c_out[...] = decay * c_prev + _dot_tn(wk.astype(BF16), v)
    den = jnp.sum(s, axis=-1, keepdims=True) + inter * jnp.sum(q.astype(F32) * n_prev, axis=-1, keepdims=True)
    h = num * (1.0 / jnp.maximum(jnp.abs(den), jnp.exp(-m_t)))
    n_new = decay * n_prev + jnp.sum(wk, axis=0, keepdims=True)
    return h, n_new, m_new


def _head_norm(h, w, eps=1e-5):
    mu = jnp.mean(h, axis=-1, keepdims=True)
    var = jnp.mean(jnp.square(h - mu), axis=-1, keepdims=True)
    return (h - mu) * lax.rsqrt(var + eps) * w


def _cell_prompt_body(q_ref, k_ref, v_ref, gc_ref, gr_ref, hw_ref,
                      h_ref, c_ref, n_ref, m_ref, *, heads):
    c = pl.program_id(1)

    @pl.when(c == 0)
    def _():
        c_ref[...] = jnp.zeros(c_ref.shape, F32)
        n_ref[...] = jnp.zeros(n_ref.shape, F32)
        m_ref[...] = jnp.full(m_ref.shape, STAB_INIT, F32)

    gc = gc_ref[...]
    gr = gr_ref[...]
    half = N_GATES // 2
    dh = q_ref.shape[1] // heads
    for hd in range(heads):
        sl = slice(hd * dh, (hd + 1) * dh)
        h, n_new, m_new = _mlstm_chunk(
            q_ref[:, sl], k_ref[:, sl], v_ref[:, sl],
            gc[:, hd:hd + 1], gc[:, hd + half:hd + half + 1],
            gr[hd:hd + 1, :], gr[hd + half:hd + half + 1, :],
            c_ref.at[0, hd], c_ref.at[0, hd], n_ref[0, hd], m_ref[0, hd, :, 0:1])
        h_ref[:, sl] = _head_norm(h, hw_ref[hd]).astype(BF16)
        n_ref[0, hd] = n_new
        m_ref[0, hd] = jnp.broadcast_to(m_new, m_ref.shape[2:])


def _cell_prompt(q, k, v, gc, gr, hn_w, *, batch, heads):
    tp, inner = q.shape
    dh = inner // heads
    seq = tp // batch
    ln = MLSTM_CHUNK_PROMPT
    nc = seq // ln
    qspec = pl.BlockSpec((ln, inner), lambda b, c: (b * nc + c, 0))
    return pl.pallas_call(
        functools.partial(_cell_prompt_body, heads=heads),
        grid=(batch, nc),
        in_specs=[qspec, qspec, qspec,
                  pl.BlockSpec((ln, LANES), lambda b, c: (b * nc + c, 0)),
                  pl.BlockSpec((SUBLANES, ln), lambda b, c: (0, b * nc + c)),
                  pl.BlockSpec((heads, 1, dh), lambda b, c: (0, 0, 0))],
        out_specs=[qspec,
                   pl.BlockSpec((1, heads, dh, dh), lambda b, c: (b, 0, 0, 0)),
                   pl.BlockSpec((1, heads, 1, dh), lambda b, c: (b, 0, 0, 0)),
                   pl.BlockSpec((1, heads, 1, LANES), lambda b, c: (b, 0, 0, 0))],
        out_shape=[jax.ShapeDtypeStruct((tp, inner), BF16),
                   jax.ShapeDtypeStruct((batch, heads, dh, dh), F32),
                   jax.ShapeDtypeStruct((batch, heads, 1, dh), F32),
                   jax.ShapeDtypeStruct((batch, heads, 1, LANES), F32)],
        compiler_params=_cparams("parallel", "arbitrary"),
        name="mlstm_cell_prompt",
    )(q, k, v, gc, gr, hn_w)


STATE_RING = 3


def _cell_sample_body(q_ref, k_ref, v_ref, gc_ref, gr_ref, hw_ref, c0_hbm, n0_ref, m0_ref,
                      h_ref, c_ref, n_ref, m_ref, c0_buf, c0_sem, *, heads, valid):
    b = pl.program_id(0)
    nb = pl.num_programs(0)

    def fetch(step):
        slot = step % STATE_RING
        return pltpu.make_async_copy(c0_hbm.at[step], c0_buf.at[slot], c0_sem.at[slot])

    @pl.when(b == 0)
    def _():
        for step in range(STATE_RING - 1):
            @pl.when(step < nb)
            def _():
                fetch(step).start()

    @pl.when(b + STATE_RING - 1 < nb)
    def _():
        fetch(b + STATE_RING - 1).start()

    fetch(b).wait()
    c0_ref = c0_buf.at[b % STATE_RING]
    ln = MLSTM_CHUNK_SAMPLE
    rows = q_ref.shape[1]
    dh = q_ref.shape[2] // heads
    half = N_GATES // 2
    gc = gc_ref[0]
    gr = gr_ref[0]
    row_c = lax.broadcasted_iota(I32, gc.shape, 0)
    lane_c = lax.broadcasted_iota(I32, gc.shape, 1)
    gc = jnp.where(row_c < valid, gc, jnp.where(lane_c < half, -jnp.inf, 0.0))
    row_r = lax.broadcasted_iota(I32, gr.shape, 0)
    lane_r = lax.broadcasted_iota(I32, gr.shape, 1)
    gr = jnp.where(lane_r < valid, gr, jnp.where(row_r < half, -jnp.inf, 0.0))
    bc = _cumsum_rows(jnp.where(lane_c < half, 0.0, gc))
    br = _cumsum_lanes(jnp.where(row_r < half, 0.0, gr))
    def padded(ref, sl):
        if ln == rows:
            return ref[0, :, sl]
        pad = jnp.zeros((ln - rows, dh), F32)
        return jnp.concatenate([ref[0, :, sl].astype(F32), pad], axis=0).astype(BF16)

    for hd in range(heads):
        sl = slice(hd * dh, (hd + 1) * dh)
        h, n_new, m_new = _mlstm_chunk(
            q_ref[0, :, sl], padded(k_ref, sl), padded(v_ref, sl),
            gc[:, hd:hd + 1], bc[:, hd + half:hd + half + 1],
            gr[hd:hd + 1, :], br[hd + half:hd + half + 1, :],
            c0_ref.at[hd], c_ref.at[0, hd], n0_ref[0, hd], m0_ref[0, hd, :, 0:1])
        h_ref[0, :, sl] = _head_norm(h, hw_ref[hd]).astype(BF16)
        n_ref[0, hd] = n_new
        m_ref[0, hd] = jnp.broadcast_to(m_new, m_ref.shape[2:])


def _cell_sample(q, k, v, gc, gr, hn_w, c0, n0, m0, *, heads, valid, group_batch):
    n_groups, rows, wide = q.shape
    inner = wide // group_batch
    dh = inner // heads
    nb = n_groups * group_batch
    ln = MLSTM_CHUNK_SAMPLE
    qspec = pl.BlockSpec((1, rows, inner), lambda b: (b // group_batch, 0, b % group_batch))
    cspec = pl.BlockSpec((1, heads, dh, dh), lambda b: (b, 0, 0, 0))
    nspec = pl.BlockSpec((1, heads, 1, dh), lambda b: (b, 0, 0, 0))
    mspec = pl.BlockSpec((1, heads, 1, LANES), lambda b: (b, 0, 0, 0))
    return pl.pallas_call(
        functools.partial(_cell_sample_body, heads=heads, valid=valid),
        grid=(nb,),
        in_specs=[qspec, qspec, qspec,
                  pl.BlockSpec((1, ln, LANES), lambda b: (b, 0, 0)),
                  pl.BlockSpec((1, SUBLANES, ln), lambda b: (b, 0, 0)),
                  pl.BlockSpec((heads, 1, dh), lambda b: (0, 0, 0)),
                  pl.BlockSpec(memory_space=pl.ANY), nspec, mspec],
        out_specs=[qspec, cspec, nspec, mspec],
        out_shape=[jax.ShapeDtypeStruct(q.shape, BF16),
                   jax.ShapeDtypeStruct((nb, heads, dh, dh), F32),
                   jax.ShapeDtypeStruct((nb, heads, 1, dh), F32),
                   jax.ShapeDtypeStruct((nb, heads, 1, LANES), F32)],
        scratch_shapes=[pltpu.VMEM((STATE_RING, heads, dh, dh), F32),
                        pltpu.SemaphoreType.DMA((STATE_RING,))],
        compiler_params=_cparams("arbitrary"),
        name="mlstm_cell_sample",
    )(q, k, v, gc, gr, hn_w, c0, n0, m0)


def _gate_down_body(hp_ref, hs_ref, cp_ref, cs_ref, z_ref, skip_ref, w_ref, xp_ref, xs_ref, o_ref,
                    *, n_prompt_tiles):
    i = pl.program_id(0)
    is_p = i < n_prompt_tiles
    hn = jnp.where(is_p, hp_ref[...], hs_ref[...]).astype(F32)
    xc = jnp.where(is_p, cp_ref[...], cs_ref[...]).astype(F32)
    xres = jnp.where(is_p, xp_ref[...], xs_ref[...])
    a = (hn + skip_ref[...] * xc) * _silu(z_ref[...].astype(F32))
    o_ref[...] = xres + _dot(a.astype(BF16), w_ref[...])


def _gate_down(hn_p, hn_s, xc_p, xc_s, z, skip, w, xp, xs):
    tp, inner = hn_p.shape
    tt = z.shape[0]
    d = w.shape[1]
    rows = GATE_DOWN_TILE
    assert tp % rows == 0 and (tt - tp) % rows == 0
    npt = tp // rows
    ip, is_ = _two_src_specs(npt, inner, rows)
    dp, ds = _two_src_specs(npt, d, rows)
    return pl.pallas_call(
        functools.partial(_gate_down_body, n_prompt_tiles=npt),
        grid=(tt // rows,),
        in_specs=[ip, is_, ip, is_,
                  pl.BlockSpec((rows, inner), lambda i: (i, 0)),
                  pl.BlockSpec((1, inner), lambda i: (0, 0)),
                  pl.BlockSpec((inner, d), lambda i: (0, 0)),
                  dp, ds],
        out_specs=pl.BlockSpec((rows, d), lambda i: (i, 0)),
        out_shape=jax.ShapeDtypeStruct((tt, d), F32),
        compiler_params=_cparams("parallel"),
        name="gate_down",
    )(hn_p, hn_s, xc_p, xc_s, z, skip, w, xp, xs)


def _cmlp_in_body(x_ref, g_ref, w_ref, b_ref, lw_ref, lb_ref, u_ref, v_ref, *, half):
    h = _rms(x_ref[...], g_ref[...]).astype(BF16)
    y = _dot(h, w_ref[...]) + b_ref[...]
    c = math.sqrt(2.0 / math.pi)
    hy = 0.5 * y
    zz = hy * jnp.tanh(y * (c + (0.044715 * c) * (y * y))) + hy
    u_ref[...] = zz[:, :half].astype(BF16)
    v = zz[:, half:]
    mu = jnp.mean(v, axis=-1, keepdims=True)
    var = jnp.mean(jnp.square(v - mu), axis=-1, keepdims=True)
    v_ref[...] = (v - mu) * lax.rsqrt(var + 1e-5) * lw_ref[...] + lb_ref[...]


def _cmlp_in(x, g, w, b, lw, lb):
    tt, d = x.shape
    n2 = w.shape[1]
    half = n2 // 2
    vec = lambda n: pl.BlockSpec((1, n), lambda i: (0, 0))
    return pl.pallas_call(
        functools.partial(_cmlp_in_body, half=half),
        grid=(tt // ROW_TILE,),
        in_specs=[pl.BlockSpec((ROW_TILE, d), lambda i: (i, 0)), vec(d),
                  pl.BlockSpec((d, n2), lambda i: (0, 0)), vec(n2), vec(half), vec(half)],
        out_specs=[pl.BlockSpec((ROW_TILE, half), lambda i: (i, 0))] * 2,
        out_shape=[jax.ShapeDtypeStruct((tt, half), BF16), jax.ShapeDtypeStruct((tt, half), F32)],
        compiler_params=_cparams("parallel"),
        name="cmlp_in",
    )(x, g, w, b, lw, lb)


def _cmlp_mix_out_body(u_ref, v_ref, mix_ref, bias_ref, w_ref, bo_ref, x_ref, o_ref, *, groups):
    half = u_ref.shape[1]
    gd = half // groups
    pieces = []
    for g in range(groups):
        sl = slice(g * gd, (g + 1) * gd)
        mixed = _dot(mix_ref[0, g], v_ref[:, sl].astype(BF16)) + bias_ref[0, :, g:g + 1]
        pieces.append((u_ref[:, sl].astype(F32) * mixed).astype(BF16))
    a = jnp.concatenate(pieces, axis=1)
    o_ref[...] = x_ref[...] + _dot(a, w_ref[...]) + bo_ref[...]


def _cmlp_mix_out(u, v, mix, bias, w, bo, x, *, n_prompt_tiles):
    tt, half = u.shape
    d = w.shape[1]
    groups = mix.shape[1]
    kind = lambda i: jnp.where(i < n_prompt_tiles, 0, 1)
    return pl.pallas_call(
        functools.partial(_cmlp_mix_out_body, groups=groups),
        grid=(tt // ROW_TILE,),
        in_specs=[pl.BlockSpec((ROW_TILE, half), lambda i: (i, 0)),
                  pl.BlockSpec((ROW_TILE, half), lambda i: (i, 0)),
                  pl.BlockSpec((1, groups, ROW_TILE, ROW_TILE), lambda i: (kind(i), 0, 0, 0)),
                  pl.BlockSpec((1, ROW_TILE, groups), lambda i: (kind(i), 0, 0)),
                  pl.BlockSpec((half, d), lambda i: (0, 0)),
                  pl.BlockSpec((1, d), lambda i: (0, 0)),
                  pl.BlockSpec((ROW_TILE, d), lambda i: (i, 0))],
        out_specs=pl.BlockSpec((ROW_TILE, d), lambda i: (i, 0)),
        out_shape=jax.ShapeDtypeStruct((tt, d), F32),
        compiler_params=_cparams("parallel"),
        name="cmlp_mix_out",
    )(u, v, mix, bias, w, bo, x)


GROUP_ROW_OFFSET = SUBLANES


def _first_argmax_rows(x, n):
    row = lax.broadcasted_iota(I32, x.shape, 0)
    mx = jnp.max(x, axis=0, keepdims=True)
    idx = jnp.min(jnp.where(x == mx, row, n), axis=0, keepdims=True)
    return idx, mx


def _router_body(x_ref, g_ref, wt_ref, bt_ref, route_ref, cnt_ref, *, n_groups, per_group):
    i = pl.program_id(0)
    n_exp = n_groups * per_group

    @pl.when(i == 0)
    def _():
        cnt_ref[...] = jnp.zeros(cnt_ref.shape, F32)

    h = _rms(x_ref[...], g_ref[...])
    logits = _dot_nt_split(wt_ref[...], h) + bt_ref[:, 0:1]
    tokens = logits.shape[1]
    gl = logits[0:n_groups, :]
    gmax = jnp.max(gl, axis=0, keepdims=True)
    ge = jnp.exp(gl - gmax)
    gp = ge / jnp.sum(ge, axis=0, keepdims=True)
    g_idx, g_w = _first_argmax_rows(gp, n_groups)
    e_sel = jnp.zeros((per_group, tokens), F32)
    for g in range(n_groups):
        lo = GROUP_ROW_OFFSET + g * per_group
        e_sel = e_sel + jnp.where(g_idx == g, logits[lo:lo + per_group, :], 0.0)
    i1, m1 = _first_argmax_rows(e_sel, per_group)
    row = lax.broadcasted_iota(I32, e_sel.shape, 0)
    i2, m2 = _first_argmax_rows(jnp.where(row == i1, -jnp.inf, e_sel), per_group)
    t = jnp.exp(m2 - m1)
    w1 = (1.0 / (1.0 + t)) * g_w
    w2 = (t / (1.0 + t)) * g_w
    e1 = g_idx * per_group + i1
    e2 = g_idx * per_group + i2

    erow = lax.broadcasted_iota(I32, (n_exp, tokens), 0)
    oh1 = erow == e1
    oh2 = erow == e2
    onehot = jnp.where(oh1 | oh2, 1.0, 0.0)
    strict_upper = (lax.broadcasted_iota(I32, (tokens, tokens), 0)
                    < lax.broadcasted_iota(I32, (tokens, tokens), 1)).astype(BF16)
    before = _dot(onehot.astype(BF16), strict_upper) + cnt_ref[:, 0:1]
    r1 = jnp.sum(jnp.where(oh1, before, 0.0), axis=0, keepdims=True)
    r2 = jnp.sum(jnp.where(oh2, before, 0.0), axis=0, keepdims=True)
    cnt_ref[...] = cnt_ref[...] + jnp.sum(onehot, axis=1, keepdims=True)
    zero = jnp.zeros_like(w1)
    route_ref[...] = jnp.concatenate(
        [e1.astype(F32), e2.astype(F32), r1, r2, w1, w2, zero, zero], axis=0)


def _wide_tile(n_rows, max_tiles):
    n = n_rows // ROW_TILE
    return ROW_TILE * max(k for k in range(1, max_tiles + 1) if n % k == 0)


def _router(x, g, wt, bt, *, n_groups, per_group):
    tt, d = x.shape
    n_exp = n_groups * per_group
    rt = _wide_tile(tt, 2)
    return pl.pallas_call(
        functools.partial(_router_body, n_groups=n_groups, per_group=per_group),
        grid=(tt // rt,),
        in_specs=[pl.BlockSpec((rt, d), lambda i: (i, 0)),
                  pl.BlockSpec((1, d), lambda i: (0, 0)),
                  pl.BlockSpec(wt.shape, lambda i: (0, 0)),
                  pl.BlockSpec(bt.shape, lambda i: (0, 0))],
        out_specs=[pl.BlockSpec((SUBLANES, rt), lambda i: (0, i)),
                   pl.BlockSpec((n_exp, LANES), lambda i: (0, 0))],
        out_shape=[jax.ShapeDtypeStruct((SUBLANES, tt), F32),
                   jax.ShapeDtypeStruct((n_exp, LANES), F32)],
        compiler_params=_cparams("arbitrary"),
        name="moe_router",
    )(x, g, wt, bt)


def _positions_body(route_ref, off_ref, pos_ref):
    r = route_ref[...]
    n_exp = off_ref.shape[0]
    tokens = r.shape[1]
    erow = lax.broadcasted_iota(I32, (n_exp, tokens), 0)
    off = off_ref[:, 0:1]
    out = []
    for k in range(2):
        e = r[k:k + 1, :].astype(I32)
        base = jnp.sum(jnp.where(erow == e, off, 0.0), axis=0, keepdims=True)
        out.append((base + r[2 + k:3 + k, :]).astype(I32))
    pos_ref[...] = jnp.concatenate(out, axis=0)


def _positions(route, off):
    tt = route.shape[1]
    pt = _wide_tile(tt, 16)
    return pl.pallas_call(
        _positions_body,
        grid=(tt // pt,),
        in_specs=[pl.BlockSpec((SUBLANES, pt), lambda i: (0, i)),
                  pl.BlockSpec(off.shape, lambda i: (0, 0))],
        out_specs=pl.BlockSpec((2, pt), lambda i: (0, i)),
        out_shape=jax.ShapeDtypeStruct((2, tt), I32),
        compiler_params=_cparams("parallel"),
        name="moe_positions",
    )(route, off)


def _for_each_row(n_rows, fn):
    def group(g, carry):
        start = pl.multiple_of(g * SUBLANES, SUBLANES)
        for u in range(SUBLANES):
            fn(start, u)
        return carry

    lax.fori_loop(0, n_rows // SUBLANES, group, 0, unroll=2)


N_DMA_PRIORITIES = 2


def _start_alternating(copies):
    for i, cp in enumerate(copies):
        cp.start(priority=i % N_DMA_PRIORITIES)


def _tile_row(ref, group_start, u):
    return ref.at[pl.ds(group_start, SUBLANES)].at[pl.ds(u, 1)]


def _dispatch_body(pos_ref, x_ref, xs_hbm, sem, *, n_tokens):
    rows = x_ref.shape[0]
    base = pl.program_id(0) * rows

    def copies(start, u):
        return [pltpu.make_async_copy(_tile_row(x_ref, start, u),
                                      xs_hbm.at[pl.ds(pos_ref[k * n_tokens + base + start + u], 1)], sem)
                for k in range(2)]

    _for_each_row(rows, lambda start, u: _start_alternating(copies(start, u)))
    _for_each_row(rows, lambda start, u: [cp.wait() for cp in copies(start, u)])


def _dispatch(pos_flat, x):
    tt, d = x.shape
    rows = _wide_tile(tt, 2)
    return pl.pallas_call(
        functools.partial(_dispatch_body, n_tokens=tt),
        grid_spec=pltpu.PrefetchScalarGridSpec(
            num_scalar_prefetch=1,
            grid=(tt // rows,),
            in_specs=[pl.BlockSpec((rows, d), lambda i, p: (i, 0))],
            out_specs=pl.BlockSpec(memory_space=pl.ANY),
            scratch_shapes=[pltpu.SemaphoreType.DMA]),
        out_shape=jax.ShapeDtypeStruct((2 * tt, d), F32),
        compiler_params=_cparams("arbitrary"),
        name="moe_dispatch",
    )(pos_flat, x)


def _experts_body(tile_ref, exp_ref, lo_ref, hi_ref, nwork_ref,
                  xs_ref, g_ref, wg_ref, wu_ref, wd_ref, o_ref, wg_b, wu_b, wd_b):
    w = pl.program_id(0)
    prev = jnp.maximum(w - 1, 0)
    first = jnp.logical_or(w == 0, tile_ref[w] != tile_ref[prev])
    new_expert = jnp.logical_or(w == 0, exp_ref[w] != exp_ref[prev])

    @pl.when(w < nwork_ref[0])
    def _():
        @pl.when(new_expert)
        def _():
            wg_b[...] = wg_ref[0].astype(BF16)
            wu_b[...] = wu_ref[0].astype(BF16)
            wd_b[...] = wd_ref[0].astype(BF16)

        xb = _rms(xs_ref[...], g_ref[...]).astype(BF16)
        a = _dot(xb, wg_b[...])
        b = _dot(xb, wu_b[...])
        hg = (_silu(a) * b).astype(BF16)
        y = _dot(hg, wd_b[...])
        lo, hi = lo_ref[w], hi_ref[w]
        whole = jnp.logical_and(lo == 0, hi == EXPERT_TILE)
        row = lax.broadcasted_iota(I32, y.shape, 0)
        mine = (row >= lo) & (row < hi)

        @pl.when(whole)
        def _():
            o_ref[...] = y

        @pl.when(jnp.logical_and(first, jnp.logical_not(whole)))
        def _():
            o_ref[...] = jnp.where(mine, y, 0.0)

        @pl.when(jnp.logical_not(first))
        def _():
            o_ref[...] = jnp.where(mine, y, o_ref[...])


def _experts(meta, xs, g, wg, wu, wd):
    rows, d = xs.shape
    n_work = meta[0].shape[0]
    f = wg.shape[2]
    return pl.pallas_call(
        _experts_body,
        grid_spec=pltpu.PrefetchScalarGridSpec(
            num_scalar_prefetch=5,
            grid=(n_work,),
            in_specs=[pl.BlockSpec((EXPERT_TILE, d), lambda w, tl, ex, lo, hi, nw: (tl[w], 0)),
                      pl.BlockSpec((1, d), lambda w, tl, ex, lo, hi, nw: (0, 0)),
                      pl.BlockSpec((1, d, f), lambda w, tl, ex, lo, hi, nw: (ex[w], 0, 0)),
                      pl.BlockSpec((1, d, f), lambda w, tl, ex, lo, hi, nw: (ex[w], 0, 0)),
                      pl.BlockSpec((1, f, d), lambda w, tl, ex, lo, hi, nw: (ex[w], 0, 0))],
            out_specs=pl.BlockSpec((EXPERT_TILE, d), lambda w, tl, ex, lo, hi, nw: (tl[w], 0)),
            scratch_shapes=[pltpu.VMEM((d, f), BF16), pltpu.VMEM((d, f), BF16), pltpu.VMEM((f, d), BF16)]),
        out_shape=jax.ShapeDtypeStruct((rows, d), F32),
        compiler_params=_cparams("arbitrary"),
        name="moe_experts",
    )(*meta, xs, g, wg, wu, wd)


COMBINE_SLOTS = 2


def _combine_rows(pos_ref, x_ref, route_ref, ys_hbm, buf, sem, n_tokens):
    i = pl.program_id(0)

    def copies(tile, start, u):
        slot = tile % COMBINE_SLOTS
        t = tile * ROW_TILE + start + u
        return [pltpu.make_async_copy(ys_hbm.at[pl.ds(pos_ref[k * n_tokens + t], 1)],
                                      _tile_row(buf.at[slot, k], start, u), sem.at[slot])
                for k in range(2)]

    def request(tile):
        _for_each_row(ROW_TILE, lambda start, u: _start_alternating(copies(tile, start, u)))

    @pl.when(i == 0)
    def _():
        request(0)

    @pl.when(i + 1 < pl.num_programs(0))
    def _():
        request(i + 1)

    r = route_ref[...]
    rt = jnp.concatenate([r, jnp.zeros((LANES - SUBLANES, r.shape[1]), F32)], axis=0).T
    _for_each_row(ROW_TILE, lambda start, u: [cp.wait() for cp in copies(i, start, u)])
    slot = i % COMBINE_SLOTS
    return x_ref[...] + rt[:, 4:5] * buf[slot, 0] + rt[:, 5:6] * buf[slot, 1]


def _combine_body(pos_ref, x_ref, route_ref, ys_hbm, o_ref, buf, sem, *, n_tokens):
    o_ref[...] = _combine_rows(pos_ref, x_ref, route_ref, ys_hbm, buf, sem, n_tokens)


def _combine_norm_body(pos_ref, x_ref, route_ref, g_ref, ys_hbm, op_ref, os_ref, buf, sem,
                       *, n_tokens, n_prompt_tiles):
    y = _rms(_combine_rows(pos_ref, x_ref, route_ref, ys_hbm, buf, sem, n_tokens), g_ref[...])
    is_prompt = pl.program_id(0) < n_prompt_tiles

    @pl.when(is_prompt)
    def _():
        op_ref[...] = y

    @pl.when(jnp.logical_not(is_prompt))
    def _():
        os_ref[...] = y


def _combine(pos_flat, x, route, ys, final_g=None, n_prompt_tiles=None):
    tt, d = x.shape
    in_specs = [pl.BlockSpec((ROW_TILE, d), lambda i, p: (i, 0)),
                pl.BlockSpec((SUBLANES, ROW_TILE), lambda i, p: (0, i))]
    args = [pos_flat, x, route]
    if final_g is None:
        body = functools.partial(_combine_body, n_tokens=tt)
        out_specs = pl.BlockSpec((ROW_TILE, d), lambda i, p: (i, 0))
        out_shape = jax.ShapeDtypeStruct((tt, d), F32)
    else:
        npt = n_prompt_tiles
        body = functools.partial(_combine_norm_body, n_tokens=tt, n_prompt_tiles=npt)
        in_specs.append(pl.BlockSpec((1, d), lambda i, p: (0, 0)))
        args.append(final_g)
        out_specs = [pl.BlockSpec((ROW_TILE, d), lambda i, p: (jnp.minimum(i, npt - 1), 0)),
                     pl.BlockSpec((ROW_TILE, d), lambda i, p: (jnp.maximum(i - npt, 0), 0))]
        out_shape = [jax.ShapeDtypeStruct((npt * ROW_TILE, d), F32),
                     jax.ShapeDtypeStruct((tt - npt * ROW_TILE, d), F32)]
    in_specs.append(pl.BlockSpec(memory_space=pl.ANY))
    args.append(ys)
    return pl.pallas_call(
        body,
        grid_spec=pltpu.PrefetchScalarGridSpec(
            num_scalar_prefetch=1,
            grid=(tt // ROW_TILE,),
            in_specs=in_specs,
            out_specs=out_specs,
            scratch_shapes=[pltpu.VMEM((COMBINE_SLOTS, 2, ROW_TILE, d), F32),
                            pltpu.SemaphoreType.DMA((COMBINE_SLOTS,))]),
        out_shape=out_shape,
        compiler_params=_cparams("arbitrary"),
        name="moe_combine",
    )(*args)


def _work_items(counts, n_rows):
    n_exp = counts.shape[0]
    n_tiles = n_rows // EXPERT_TILE
    n_work = n_tiles + n_exp - 1
    ids = jnp.arange(n_exp, dtype=I32)
    lower = ids[None, :] <= ids[:, None]

    def cumsum(v):
        return jnp.sum(jnp.where(lower, v[None, :], 0), axis=1)

    end = cumsum(counts)
    off = end - counts
    first_tile = off // EXPERT_TILE
    last_tile = (end - 1) // EXPERT_TILE
    per_exp = jnp.where(counts > 0, last_tile - first_tile + 1, 0)
    wend = cumsum(per_exp)
    wstart = wend - per_exp
    total = jnp.sum(per_exp)
    w = jnp.arange(n_work, dtype=I32)
    e = jnp.minimum(jnp.sum((wend[None, :] <= w[:, None]).astype(I32), axis=1), n_exp - 1)
    sel = e[:, None] == ids[None, :]

    def pick(v):
        return jnp.sum(jnp.where(sel, v[None, :], 0), axis=1)

    tile = pick(first_tile) + (w - pick(wstart))
    lo = jnp.maximum(pick(off), tile * EXPERT_TILE) - tile * EXPERT_TILE
    hi = jnp.minimum(pick(end), (tile + 1) * EXPERT_TILE) - tile * EXPERT_TILE
    live = w < total
    tile = jnp.where(live, tile, n_tiles - 1)
    lo = jnp.where(live, lo, 0)
    hi = jnp.where(live, hi, 0)
    return off, (tile.astype(I32), e.astype(I32), lo.astype(I32), hi.astype(I32),
                 total.reshape(1).astype(I32))


def _moe(x, g, layer, w_rg, b_rg, w_re, b_re, w_gate, w_up, w_down, final_g=None, n_prompt_tiles=None):
    tt, d = x.shape
    n_groups, per_group = w_re.shape[0], w_re.shape[2]
    n_exp = n_groups * per_group
    f = w_gate.shape[-1]
    pad_rows = jnp.zeros((GROUP_ROW_OFFSET - n_groups, d), F32)
    wt = jnp.concatenate([w_rg.T, pad_rows, jnp.transpose(w_re, (0, 2, 1)).reshape(n_exp, d)], axis=0)
    bt = jnp.concatenate([b_rg, jnp.zeros((GROUP_ROW_OFFSET - n_groups,), F32), b_re.reshape(n_exp)])
    bt = jnp.broadcast_to(bt[:, None], (wt.shape[0], LANES))
    g2 = g.reshape(1, d)

    route, cnt = _router(x, g2, wt, bt, n_groups=n_groups, per_group=per_group)
    counts = cnt[:, 0].astype(I32)
    off, (tile, exp, lo, hi, total) = _work_items(counts, 2 * tt)
    off_b = jnp.broadcast_to(off.astype(F32)[:, None], (n_exp, LANES))
    pos = _positions(route, off_b).reshape(2 * tt)
    xs = _dispatch(pos, x)
    ys = _experts((tile, exp + layer * n_exp, lo, hi, total), xs, g2,
                  w_gate.reshape(-1, d, f), w_up.reshape(-1, d, f), w_down.reshape(-1, f, d))
    return _combine(pos, x, route, ys, final_g, n_prompt_tiles)


def _block_diag_tiles(w, tile):
    nb, c, _ = w.shape
    rows = w.reshape(nb * c // tile, tile, c)
    blk = jnp.arange(tile, dtype=I32) // c
    return jnp.where(blk[:, None] == blk[None, :], jnp.tile(rows, (1, 1, tile // c)), 0.0)


def _sample_to_rows(x, group_batch):
    db, s, w = x.shape
    return x.reshape(db // group_batch, group_batch, s, w).transpose(0, 2, 1, 3).reshape(db * s, w)


def _rows_to_sample(x, group_batch, s):
    rows, w = x.shape
    db = rows // s
    return x.reshape(db // group_batch, s, group_batch, w).transpose(0, 2, 1, 3).reshape(db, s, w)


def kernel(x_prompt, x_sample, state_mlstm_C, state_mlstm_n, state_mlstm_m, state_mlstm_conv,
           norm_mix, norm_ffn, norm_final,
           mlstm_w_up, mlstm_conv_w, mlstm_conv_b, mlstm_w_q, mlstm_w_k, mlstm_w_v,
           mlstm_w_ig, mlstm_b_ig, mlstm_w_fg, mlstm_b_fg, mlstm_skip, mlstm_hn_w, mlstm_w_down,
           cmlp_w_in, cmlp_b_in, cmlp_ln_w, cmlp_ln_b, cmlp_w_s, cmlp_b_s, cmlp_w_out, cmlp_b_out,
           moe_w_rg, moe_b_rg, moe_w_re, moe_b_re, moe_w_gate, moe_w_up, moe_w_down):
    batch, seq, d = x_prompt.shape
    dec_batch, dec_seq, _ = x_sample.shape
    heads = state_mlstm_C.shape[2]
    dh = state_mlstm_C.shape[3]
    inner = heads * dh
    tp, ts = batch * seq, dec_batch * dec_seq
    npt = tp // ROW_TILE
    gb = ROW_TILE // dec_seq
    n_sgroups = dec_batch // gb
    half_g = N_GATES // 2
    assert heads == half_g and tp % ROW_TILE == 0 and ts % ROW_TILE == 0
    assert ROW_TILE % dec_seq == 0 and dec_batch % gb == 0 and seq % MLSTM_CHUNK_PROMPT == 0
    assert seq % ROW_TILE == 0 and ROW_TILE % CMLP_CHUNK == 0 and dec_seq <= SUBLANES
    assert MLSTM_CHUNK_PROMPT == ROW_TILE

    xp = x_prompt.reshape(tp, d)
    xs = _sample_to_rows(x_sample, gb)

    xm, z = _norm_up(xp, xs, norm_mix[0].reshape(1, d), mlstm_w_up[0].astype(BF16))

    wq = _block_diag_tiles(mlstm_w_q[0], MXU_WIDTH)
    wk = _block_diag_tiles(mlstm_w_k[0], MXU_WIDTH)
    wqk = jnp.concatenate([wq, wk], axis=2).astype(BF16)
    wv = _block_diag_tiles(mlstm_w_v[0], MXU_WIDTH).astype(BF16)
    wg = jnp.concatenate([mlstm_w_ig[0], mlstm_w_fg[0]], axis=1)
    wg = jnp.pad(wg, ((0, 0), (0, LANES - N_GATES))).reshape(3, inner, LANES).astype(BF16)
    bg = jnp.pad(jnp.concatenate([mlstm_b_ig[0], mlstm_b_fg[0]]), (0, LANES - N_GATES)).reshape(1, LANES)
    cw, cb = mlstm_conv_w[0], mlstm_conv_b[0].reshape(1, inner)
    k_scale = float(dh) ** -0.5

    conv0_p = jnp.zeros((batch, SUBLANES, inner), F32)
    q_p, k_p, v_p, xc_p, _, bc_p, conv_p = _conv_qkv(
        xm, 0, conv0_p, cw, cb, wqk, wv, wg, bg,
        n_groups=batch, tiles_per_group=seq // ROW_TILE, row_stride=1, k_scale=k_scale)
    halo_s = (CONV_WIDTH - 1) * gb
    conv0_s = state_mlstm_conv[0].reshape(n_sgroups, gb, CONV_WIDTH - 1, inner)
    conv0_s = conv0_s.transpose(0, 2, 1, 3).reshape(n_sgroups, halo_s, inner)
    q_s, k_s, v_s, xc_s, gc_s, _, conv_s = _conv_qkv(
        xm, npt, conv0_s, cw, cb, wqk, wv, wg, bg,
        n_groups=n_sgroups, tiles_per_group=1, row_stride=gb, k_scale=k_scale)

    hn_w = mlstm_hn_w[0].reshape(heads, 1, dh)
    br_p = jnp.transpose(bc_p[:, :SUBLANES])
    hn_p, c_p, n_p, m_p = _cell_prompt(q_p, k_p, v_p, bc_p, br_p, hn_w, batch=batch, heads=heads)

    def pad_steps(a):
        a = a.reshape(n_sgroups, dec_seq, gb * inner)
        return jnp.pad(a, ((0, 0), (0, SUBLANES - dec_seq), (0, 0)))

    ln_s = MLSTM_CHUNK_SAMPLE
    gcs = gc_s.reshape(n_sgroups, dec_seq, gb, LANES).transpose(0, 2, 1, 3).reshape(dec_batch, dec_seq, LANES)
    gcs = jnp.pad(gcs, ((0, 0), (0, ln_s - dec_seq), (0, 0)))
    grs = jnp.transpose(gcs[:, :, :SUBLANES], (0, 2, 1))
    m0 = jnp.broadcast_to(state_mlstm_m[0][:, :, None, None], (dec_batch, heads, 1, LANES))
    hn_s, c_s, n_s, m_s = _cell_sample(
        pad_steps(q_s), pad_steps(k_s), pad_steps(v_s), gcs, grs, hn_w,
        state_mlstm_C[0], state_mlstm_n[0].reshape(dec_batch, heads, 1, dh), m0,
        heads=heads, valid=dec_seq, group_batch=gb)
    hn_s = hn_s[:, :dec_seq].reshape(ts, inner)

    x1 = _gate_down(hn_p, hn_s, xc_p, xc_s, z, mlstm_skip[0].reshape(1, inner),
                    mlstm_w_down[0].astype(BF16), xp, xs)
    x2 = _moe(x1, norm_ffn[0], 0, moe_w_rg[0], moe_b_rg[0], moe_w_re[0], moe_b_re[0],
              moe_w_gate, moe_w_up, moe_w_down)

    half = cmlp_w_in.shape[2] // 2
    groups = cmlp_w_s.shape[1]
    u, vn = _cmlp_in(x2, norm_mix[1].reshape(1, d), cmlp_w_in[0].astype(BF16),
                     cmlp_b_in[0].reshape(1, 2 * half), cmlp_ln_w[0].reshape(1, half),
                     cmlp_ln_b[0].reshape(1, half))
    causal = jnp.tril(jnp.ones((CMLP_CHUNK, CMLP_CHUNK), dtype=bool))
    ws = jnp.where(causal, cmlp_w_s[0], 0.0)
    reps = ROW_TILE // CMLP_CHUNK
    rid = jnp.arange(ROW_TILE, dtype=I32)
    same_chunk = (rid[:, None] // CMLP_CHUNK) == (rid[None, :] // CMLP_CHUNK)
    same_seq = (rid[:, None] % gb) == (rid[None, :] % gb)
    mix_p = jnp.where(same_chunk, jnp.tile(ws, (1, reps, reps)), 0.0)
    ws_s = ws[:, :dec_seq, :dec_seq]
    mix_s = jnp.where(same_seq, jnp.repeat(jnp.repeat(ws_s, gb, axis=1), gb, axis=2), 0.0)
    mix = jnp.stack([mix_p, mix_s]).astype(BF16)
    bias_t = jnp.transpose(cmlp_b_s[0])
    bias = jnp.stack([jnp.tile(bias_t, (reps, 1)), jnp.repeat(bias_t[:dec_seq], gb, axis=0)])
    x3 = _cmlp_mix_out(u, vn, mix, bias, cmlp_w_out[0].astype(BF16), cmlp_b_out[0].reshape(1, d), x2,
                       n_prompt_tiles=npt)
    y_p, y_s = _moe(x3, norm_ffn[1], 1, moe_w_rg[1], moe_b_rg[1], moe_w_re[1], moe_b_re[1],
                    moe_w_gate, moe_w_up, moe_w_down,
                    final_g=norm_final.reshape(1, d), n_prompt_tiles=npt)
    y_prompt = y_p.reshape(batch, seq, d)
    y_sample = _rows_to_sample(y_s, gb, dec_seq)

    conv_prompt = conv_p[:, SUBLANES - (CONV_WIDTH - 1):, :][None]
    conv_sample = conv_s.reshape(n_sgroups, CONV_WIDTH - 1, gb, inner).transpose(0, 2, 1, 3)
    conv_sample = conv_sample.reshape(dec_batch, CONV_WIDTH - 1, inner)[None]
    v_sample = _rows_to_sample(vn[tp:], gb, dec_seq)[None]
    return (y_prompt, y_sample,
            c_p[None], n_p[:, :, 0, :][None], m_p[:, :, 0, 0][None], conv_prompt,
            c_s[None], n_s[:, :, 0, :][None], m_s[:, :, 0, 0][None], conv_sample,
            v_sample)
```

```python
import functools
import math

import jax
import jax.numpy as jnp
from jax import lax
from jax.experimental import pallas as pl
from jax.experimental.pallas import tpu as pltpu

F32 = jnp.float32
BF16 = jnp.bfloat16
I32 = jnp.int32

LANES = 128
SUBLANES = 8
ROW_TILE = 256
EXPERT_TILE = 512
GATE_DOWN_TILE = 512
MIX_OUT_TILE = 512
MXU_WIDTH = 256
VMEM_LIMIT = 56 * 1024 * 1024
CONV_WIDTH = 4
QKV_BLOCK = 4
MLSTM_CHUNK_PROMPT = 256
MLSTM_CHUNK_SAMPLE = 8
CMLP_CHUNK = 128
STAB_INIT = -1e30
N_GATES = 8


def _cparams(*sem):
    return pltpu.CompilerParams(dimension_semantics=sem, vmem_limit_bytes=VMEM_LIMIT)


def _rms(x, g, eps=1e-6):
    return x * lax.rsqrt(jnp.mean(x * x, axis=-1, keepdims=True) + eps) * g


def _silu(x):
    hx = 0.5 * x
    return hx * jnp.tanh(hx) + hx


def _split3(x):
    p0 = x.astype(BF16)
    r = x - p0.astype(F32)
    p1 = r.astype(BF16)
    p2 = (r - p1.astype(F32)).astype(BF16)
    return p0, p1, p2


def _dot(a, b):
    return jnp.dot(a, b, preferred_element_type=F32)


def _dot_nt(a, b):
    return lax.dot_general(a, b, (((1,), (1,)), ((), ())), preferred_element_type=F32)


def _dot_tn(a, b):
    return lax.dot_general(a, b, (((0,), (0,)), ((), ())), preferred_element_type=F32)


def _dot_nt_split(a, b):
    a0, a1, _ = _split3(a)
    b0, b1, _ = _split3(b)
    return _dot_nt(a0, b0) + (_dot_nt(a0, b1) + _dot_nt(a1, b0))


def _norm_up_body(xp_ref, xs_ref, g_ref, w_ref, xm_ref, z_ref, *, n_prompt_tiles, inner):
    i = pl.program_id(0)
    x = jnp.where(i < n_prompt_tiles, xp_ref[...], xs_ref[...])
    h = _rms(x, g_ref[...]).astype(BF16)
    u = _dot(h, w_ref[...])
    xm_ref[...] = u[:, :inner]
    z_ref[...] = u[:, inner:].astype(BF16)


def _two_src_specs(n_prompt_tiles, width, rows=ROW_TILE):
    sp = pl.BlockSpec((rows, width), lambda i: (jnp.minimum(i, n_prompt_tiles - 1), 0))
    ss = pl.BlockSpec((rows, width), lambda i: (jnp.maximum(i - n_prompt_tiles, 0), 0))
    return sp, ss


def _norm_up(xp, xs, g, w):
    tp, d = xp.shape
    ts = xs.shape[0]
    tt = tp + ts
    n2 = w.shape[1]
    inner = n2 // 2
    npt = tp // ROW_TILE
    sp, ss = _two_src_specs(npt, d)
    return pl.pallas_call(
        functools.partial(_norm_up_body, n_prompt_tiles=npt, inner=inner),
        grid=(tt // ROW_TILE,),
        in_specs=[sp, ss,
                  pl.BlockSpec((1, d), lambda i: (0, 0)),
                  pl.BlockSpec((d, n2), lambda i: (0, 0))],
        out_specs=[pl.BlockSpec((ROW_TILE, inner), lambda i: (i, 0)),
                   pl.BlockSpec((ROW_TILE, inner), lambda i: (i, 0))],
        out_shape=[jax.ShapeDtypeStruct((tt, inner), F32),
                   jax.ShapeDtypeStruct((tt, inner), BF16)],
        compiler_params=_cparams("parallel"),
        name="norm_up",
    )(xp, xs, g, w)


def _conv_qkv_body(xm_ref, st_ref, cw_ref, cb_ref, wqk_ref, wv_ref, wg_ref, bg_ref,
                   q_ref, k_ref, v_ref, xc_ref, gc_ref, bc_ref, st_out_ref,
                   *, row_stride, k_scale):
    t = pl.program_id(1)
    halo = st_ref.shape[1]
    rows, inner = xm_ref.shape

    @pl.when(t == 0)
    def _():
        st_out_ref[0] = st_ref[0]

    x = xm_ref[...]
    xe = jnp.concatenate([st_out_ref[0], x], axis=0)
    acc = x * cw_ref[CONV_WIDTH - 1:CONV_WIDTH, :] + cb_ref[...]
    for j in range(CONV_WIDTH - 1):
        lo = halo - (CONV_WIDTH - 1 - j) * row_stride
        acc = acc + xe[lo:lo + rows, :] * cw_ref[j:j + 1, :]
    st_out_ref[0] = xe[rows:rows + halo, :]
    xc = _silu(acc)
    xc_b = xc.astype(BF16)
    xm_b = x.astype(BF16)
    xc_ref[...] = xc_b

    gacc = jnp.zeros((rows, LANES), F32)
    for c in range(inner // MXU_WIDTH):
        sl = slice(c * MXU_WIDTH, (c + 1) * MXU_WIDTH)
        qk = _dot(xc_b[:, sl], wqk_ref[c])
        vv = _dot(xm_b[:, sl], wv_ref[c])
        qb = qk[:, :MXU_WIDTH].astype(BF16)
        kb = qk[:, MXU_WIDTH:].astype(BF16)
        vb = vv.astype(BF16)
        q_ref[:, sl] = qb
        k_ref[:, sl] = (qk[:, MXU_WIDTH:] * k_scale).astype(BF16)
        v_ref[:, sl] = vb
        gacc += _dot(qb, wg_ref[0, sl, :]) + _dot(kb, wg_ref[1, sl, :]) + _dot(vb, wg_ref[2, sl, :])
    g = gacc + bg_ref[...]
    lane = lax.broadcasted_iota(I32, g.shape, 1)
    logsig = jnp.minimum(g, 0.0) - jnp.log1p(jnp.exp(-jnp.abs(g)))
    gates = jnp.where(lane < N_GATES // 2, g, logsig)
    gc_ref[...] = gates
    bc_ref[...] = jnp.where(lane < N_GATES // 2, gates, _cumsum_rows(gates))


def _conv_qkv(xm, row_block_offset, state, cw, cb, wqk, wv, wg, bg, *, n_groups, tiles_per_group,
              row_stride, k_scale):
    inner = xm.shape[1]
    rows = n_groups * tiles_per_group * ROW_TILE
    halo = state.shape[1]
    nt = tiles_per_group
    row_spec = pl.BlockSpec((ROW_TILE, inner), lambda b, t: (b * nt + t, 0))
    gate_spec = pl.BlockSpec((ROW_TILE, LANES), lambda b, t: (b * nt + t, 0))
    full = lambda a: pl.BlockSpec(a.shape, lambda b, t: (0,) * a.ndim)
    return pl.pallas_call(
        functools.partial(_conv_qkv_body, row_stride=row_stride, k_scale=k_scale),
        grid=(n_groups, nt),
        in_specs=[pl.BlockSpec((ROW_TILE, inner), lambda b, t: (row_block_offset + b * nt + t, 0)),
                  pl.BlockSpec((1, halo, inner), lambda b, t: (b, 0, 0)),
                  full(cw), full(cb), full(wqk), full(wv), full(wg), full(bg)],
        out_specs=[row_spec, row_spec, row_spec, row_spec, gate_spec, gate_spec,
                   pl.BlockSpec((1, halo, inner), lambda b, t: (b, 0, 0))],
        out_shape=[jax.ShapeDtypeStruct((rows, inner), BF16)] * 4
        + [jax.ShapeDtypeStruct((rows, LANES), F32)] * 2
        + [jax.ShapeDtypeStruct((n_groups, halo, inner), F32)],
        compiler_params=_cparams("parallel", "arbitrary"),
        name="conv_qkv",
    )(xm, state, cw, cb, wqk, wv, wg, bg)


def _cumsum_rows(x):
    n = x.shape[0]
    tri = (lax.broadcasted_iota(I32, (n, n), 1) <= lax.broadcasted_iota(I32, (n, n), 0)).astype(BF16)
    p0, p1, p2 = _split3(x)
    return _dot(tri, p0) + _dot(tri, p1) + _dot(tri, p2)


def _cumsum_lanes(x):
    n = x.shape[1]
    tri = (lax.broadcasted_iota(I32, (n, n), 0) <= lax.broadcasted_iota(I32, (n, n), 1)).astype(BF16)
    p0, p1, p2 = _split3(x)
    return _dot(p0, tri) + _dot(p1, tri) + _dot(p2, tri)


def _mlstm_chunk(q, k, v, ig_col, bcum_col, ig_row, bcum_row, c_in, c_out, n_prev, m_prev):
    lq, lk = q.shape[0], k.shape[0]
    ri = lax.broadcasted_iota(I32, (lq, lk), 0)
    ci = lax.broadcasted_iota(I32, (lq, lk), 1)
    bcum_q = bcum_col[:lq]
    log_d = jnp.where(ci <= ri, bcum_q - bcum_row + ig_row, -jnp.inf)
    m_inter = bcum_q + m_prev
    m_t = jnp.maximum(m_inter, jnp.max(log_d, axis=-1, keepdims=True))
    s = _dot_nt(q, k) * jnp.exp(log_d - m_t)
    inter = jnp.exp(m_inter - m_t)
    m_new = m_t[lq - 1:lq, :]
    b_last = bcum_col[lk - 1:lk, :]
    w_col = jnp.exp(b_last - bcum_col + ig_col - m_new)
    decay = jnp.exp(b_last + m_prev - m_new)
    wk = w_col * k.astype(F32)
    c_prev = c_in[...]
    num = _dot(s.astype(BF16), v) + inter * _dot(q, c_prev.astype(BF16))
    c_out[...] = decay * c_prev + _dot_tn(wk.astype(BF16), v)
    den = jnp.sum(s, axis=-1, keepdims=True) + inter * jnp.sum(q.astype(F32) * n_prev, axis=-1, keepdims=True)
    h = num * (1.0 / jnp.maximum(jnp.abs(den), jnp.exp(-m_t)))
    n_new = decay * n_prev + jnp.sum(wk, axis=0, keepdims=True)
    return h, n_new, m_new


def _head_norm(h, w, eps=1e-5):
    mu = jnp.mean(h, axis=-1, keepdims=True)
    var = jnp.mean(jnp.square(h - mu), axis=-1, keepdims=True)
    return (h - mu) * lax.rsqrt(var + eps) * w


def _cell_prompt_body(q_ref, k_ref, v_ref, gc_ref, gr_ref, hw_ref,
                      h_ref, c_ref, n_ref, m_ref, *, heads):
    c = pl.program_id(1)

    @pl.when(c == 0)
    def _():
        c_ref[...] = jnp.zeros(c_ref.shape, F32)
        n_ref[...] = jnp.zeros(n_ref.shape, F32)
        m_ref[...] = jnp.full(m_ref.shape, STAB_INIT, F32)

    gc = gc_ref[...]
    gr = gr_ref[...]
    half = N_GATES // 2
    dh = q_ref.shape[1] // heads
    for hd in range(heads):
        sl = slice(hd * dh, (hd + 1) * dh)
        h, n_new, m_new = _mlstm_chunk(
            q_ref[:, sl], k_ref[:, sl], v_ref[:, sl],
            gc[:, hd:hd + 1], gc[:, hd + half:hd + half + 1],
            gr[hd:hd + 1, :], gr[hd + half:hd + half + 1, :],
            c_ref.at[0, hd], c_ref.at[0, hd], n_ref[0, hd], m_ref[0, hd, :, 0:1])
        h_ref[:, sl] = _head_norm(h, hw_ref[hd]).astype(BF16)
        n_ref[0, hd] = n_new
        m_ref[0, hd] = jnp.broadcast_to(m_new, m_ref.shape[2:])


def _cell_prompt(q, k, v, gc, gr, hn_w, *, batch, heads):
    tp, inner = q.shape
    dh = inner // heads
    seq = tp // batch
    ln = MLSTM_CHUNK_PROMPT
    nc = seq // ln
    qspec = pl.BlockSpec((ln, inner), lambda b, c: (b * nc + c, 0))
    return pl.pallas_call(
        functools.partial(_cell_prompt_body, heads=heads),
        grid=(batch, nc),
        in_specs=[qspec, qspec, qspec,
                  pl.BlockSpec((ln, LANES), lambda b, c: (b * nc + c, 0)),
                  pl.BlockSpec((SUBLANES, ln), lambda b, c: (0, b * nc + c)),
                  pl.BlockSpec((heads, 1, dh), lambda b, c: (0, 0, 0))],
        out_specs=[qspec,
                   pl.BlockSpec((1, heads, dh, dh), lambda b, c: (b, 0, 0, 0)),
                   pl.BlockSpec((1, heads, 1, dh), lambda b, c: (b, 0, 0, 0)),
                   pl.BlockSpec((1, heads, 1, LANES), lambda b, c: (b, 0, 0, 0))],
        out_shape=[jax.ShapeDtypeStruct((tp, inner), BF16),
                   jax.ShapeDtypeStruct((batch, heads, dh, dh), F32),
                   jax.ShapeDtypeStruct((batch, heads, 1, dh), F32),
                   jax.ShapeDtypeStruct((batch, heads, 1, LANES), F32)],
        compiler_params=_cparams("parallel", "arbitrary"),
        name="mlstm_cell_prompt",
    )(q, k, v, gc, gr, hn_w)


STATE_RING = 3


def _cell_sample_body(q_ref, k_ref, v_ref, gc_ref, gr_ref, hw_ref, c0_hbm, n0_ref, m0_ref,
                      h_ref, c_ref, n_ref, m_ref, c0_buf, c0_sem, *, heads, valid):
    b = pl.program_id(0)
    nb = pl.num_programs(0)

    def fetch(step):
        slot = step % STATE_RING
        return pltpu.make_async_copy(c0_hbm.at[step], c0_buf.at[slot], c0_sem.at[slot])

    @pl.when(b == 0)
    def _():
        for step in range(STATE_RING - 1):
            @pl.when(step < nb)
            def _():
                fetch(step).start()

    @pl.when(b + STATE_RING - 1 < nb)
    def _():
        fetch(b + STATE_RING - 1).start()

    fetch(b).wait()
    c0_ref = c0_buf.at[b % STATE_RING]
    ln = MLSTM_CHUNK_SAMPLE
    rows = q_ref.shape[1]
    dh = q_ref.shape[2] // heads
    half = N_GATES // 2
    gc = gc_ref[0]
    gr = gr_ref[0]
    row_c = lax.broadcasted_iota(I32, gc.shape, 0)
    lane_c = lax.broadcasted_iota(I32, gc.shape, 1)
    gc = jnp.where(row_c < valid, gc, jnp.where(lane_c < half, -jnp.inf, 0.0))
    row_r = lax.broadcasted_iota(I32, gr.shape, 0)
    lane_r = lax.broadcasted_iota(I32, gr.shape, 1)
    gr = jnp.where(lane_r < valid, gr, jnp.where(row_r < half, -jnp.inf, 0.0))
    bc = _cumsum_rows(jnp.where(lane_c < half, 0.0, gc))
    br = _cumsum_lanes(jnp.where(row_r < half, 0.0, gr))
    def padded(ref, sl):
        if ln == rows:
            return ref[0, :, sl]
        pad = jnp.zeros((ln - rows, dh), F32)
        return jnp.concatenate([ref[0, :, sl].astype(F32), pad], axis=0).astype(BF16)

    for hd in range(heads):
        sl = slice(hd * dh, (hd + 1) * dh)
        h, n_new, m_new = _mlstm_chunk(
            q_ref[0, :, sl], padded(k_ref, sl), padded(v_ref, sl),
            gc[:, hd:hd + 1], bc[:, hd + half:hd + half + 1],
            gr[hd:hd + 1, :], br[hd + half:hd + half + 1, :],
            c0_ref.at[hd], c_ref.at[0, hd], n0_ref[0, hd], m0_ref[0, hd, :, 0:1])
        h_ref[0, :, sl] = _head_norm(h, hw_ref[hd]).astype(BF16)
        n_ref[0, hd] = n_new
        m_ref[0, hd] = jnp.broadcast_to(m_new, m_ref.shape[2:])


def _cell_sample(q, k, v, gc, gr, hn_w, c0, n0, m0, *, heads, valid, group_batch):
    n_groups, rows, wide = q.shape
    inner = wide // group_batch
    dh = inner // heads
    nb = n_groups * group_batch
    ln = MLSTM_CHUNK_SAMPLE
    qspec = pl.BlockSpec((1, rows, inner), lambda b: (b // group_batch, 0, b % group_batch))
    cspec = pl.BlockSpec((1, heads, dh, dh), lambda b: (b, 0, 0, 0))
    nspec = pl.BlockSpec((1, heads, 1, dh), lambda b: (b, 0, 0, 0))
    mspec = pl.BlockSpec((1, heads, 1, LANES), lambda b: (b, 0, 0, 0))
    return pl.pallas_call(
        functools.partial(_cell_sample_body, heads=heads, valid=valid),
        grid=(nb,),
        in_specs=[qspec, qspec, qspec,
                  pl.BlockSpec((1, ln, LANES), lambda b: (b, 0, 0)),
                  pl.BlockSpec((1, SUBLANES, ln), lambda b: (b, 0, 0)),
                  pl.BlockSpec((heads, 1, dh), lambda b: (0, 0, 0)),
                  pl.BlockSpec(memory_space=pl.ANY), nspec, mspec],
        out_specs=[qspec, cspec, nspec, mspec],
        out_shape=[jax.ShapeDtypeStruct(q.shape, BF16),
                   jax.ShapeDtypeStruct((nb, heads, dh, dh), F32),
                   jax.ShapeDtypeStruct((nb, heads, 1, dh), F32),
                   jax.ShapeDtypeStruct((nb, heads, 1, LANES), F32)],
        scratch_shapes=[pltpu.VMEM((STATE_RING, heads, dh, dh), F32),
                        pltpu.SemaphoreType.DMA((STATE_RING,))],
        compiler_params=_cparams("arbitrary"),
        name="mlstm_cell_sample",
    )(q, k, v, gc, gr, hn_w, c0, n0, m0)


def _gate_down_body(hp_ref, hs_ref, cp_ref, cs_ref, z_ref, skip_ref, w_ref, xp_ref, xs_ref, o_ref,
                    *, n_prompt_tiles):
    i = pl.program_id(0)
    is_p = i < n_prompt_tiles
    hn = jnp.where(is_p, hp_ref[...], hs_ref[...]).astype(F32)
    xc = jnp.where(is_p, cp_ref[...], cs_ref[...]).astype(F32)
    xres = jnp.where(is_p, xp_ref[...], xs_ref[...])
    a = (hn + skip_ref[...] * xc) * _silu(z_ref[...].astype(F32))
    o_ref[...] = xres + _dot(a.astype(BF16), w_ref[...])


def _gate_down(hn_p, hn_s, xc_p, xc_s, z, skip, w, xp, xs):
    tp, inner = hn_p.shape
    tt = z.shape[0]
    d = w.shape[1]
    rows = GATE_DOWN_TILE
    assert tp % rows == 0 and (tt - tp) % rows == 0
    npt = tp // rows
    ip, is_ = _two_src_specs(npt, inner, rows)
    dp, ds = _two_src_specs(npt, d, rows)
    return pl.pallas_call(
        functools.partial(_gate_down_body, n_prompt_tiles=npt),
        grid=(tt // rows,),
        in_specs=[ip, is_, ip, is_,
                  pl.BlockSpec((rows, inner), lambda i: (i, 0)),
                  pl.BlockSpec((1, inner), lambda i: (0, 0)),
                  pl.BlockSpec((inner, d), lambda i: (0, 0)),
                  dp, ds],
        out_specs=pl.BlockSpec((rows, d), lambda i: (i, 0)),
        out_shape=jax.ShapeDtypeStruct((tt, d), F32),
        compiler_params=_cparams("parallel"),
        name="gate_down",
    )(hn_p, hn_s, xc_p, xc_s, z, skip, w, xp, xs)


def _cmlp_in_body(x_ref, g_ref, w_ref, b_ref, lw_ref, lb_ref, u_ref, v_ref, *, half):
    h = _rms(x_ref[...], g_ref[...]).astype(BF16)
    y = _dot(h, w_ref[...]) + b_ref[...]
    c = math.sqrt(2.0 / math.pi)
    hy = 0.5 * y
    zz = hy * jnp.tanh(y * (c + (0.044715 * c) * (y * y))) + hy
    u_ref[...] = zz[:, :half].astype(BF16)
    v = zz[:, half:]
    mu = jnp.mean(v, axis=-1, keepdims=True)
    var = jnp.mean(jnp.square(v - mu), axis=-1, keepdims=True)
    v_ref[...] = (v - mu) * lax.rsqrt(var + 1e-5) * lw_ref[...] + lb_ref[...]


def _cmlp_in(x, g, w, b, lw, lb):
    tt, d = x.shape
    n2 = w.shape[1]
    half = n2 // 2
    vec = lambda n: pl.BlockSpec((1, n), lambda i: (0, 0))
    return pl.pallas_call(
        functools.partial(_cmlp_in_body, half=half),
        grid=(tt // ROW_TILE,),
        in_specs=[pl.BlockSpec((ROW_TILE, d), lambda i: (i, 0)), vec(d),
                  pl.BlockSpec((d, n2), lambda i: (0, 0)), vec(n2), vec(half), vec(half)],
        out_specs=[pl.BlockSpec((ROW_TILE, half), lambda i: (i, 0))] * 2,
        out_shape=[jax.ShapeDtypeStruct((tt, half), BF16), jax.ShapeDtypeStruct((tt, half), F32)],
        compiler_params=_cparams("parallel"),
        name="cmlp_in",
    )(x, g, w, b, lw, lb)


def _cmlp_mix_out_body(u_ref, v_ref, mix_ref, bias_ref, w_ref, bo_ref, x_ref, o_ref, *, groups):
    rows, half = u_ref.shape
    gd = half // groups
    blocks = []
    for s in range(rows // ROW_TILE):
        rs = slice(s * ROW_TILE, (s + 1) * ROW_TILE)
        pieces = []
        for g in range(groups):
            sl = slice(g * gd, (g + 1) * gd)
            mixed = _dot(mix_ref[0, g], v_ref[rs, sl].astype(BF16)) + bias_ref[0, :, g:g + 1]
            pieces.append((u_ref[rs, sl].astype(F32) * mixed).astype(BF16))
        blocks.append(jnp.concatenate(pieces, axis=1))
    a = jnp.concatenate(blocks, axis=0)
    o_ref[...] = x_ref[...] + _dot(a, w_ref[...]) + bo_ref[...]


def _cmlp_mix_out(u, v, mix, bias, w, bo, x, *, n_prompt_tiles):
    tt, half = u.shape
    d = w.shape[1]
    groups = mix.shape[1]
    rows = MIX_OUT_TILE
    n_prompt_rows = n_prompt_tiles * ROW_TILE
    assert n_prompt_rows % rows == 0 and (tt - n_prompt_rows) % rows == 0
    kind = lambda i: jnp.where(i < n_prompt_rows // rows, 0, 1)
    return pl.pallas_call(
        functools.partial(_cmlp_mix_out_body, groups=groups),
        grid=(tt // rows,),
        in_specs=[pl.BlockSpec((rows, half), lambda i: (i, 0)),
                  pl.BlockSpec((rows, half), lambda i: (i, 0)),
                  pl.BlockSpec((1, groups, ROW_TILE, ROW_TILE), lambda i: (kind(i), 0, 0, 0)),
                  pl.BlockSpec((1, ROW_TILE, groups), lambda i: (kind(i), 0, 0)),
                  pl.BlockSpec((half, d), lambda i: (0, 0)),
                  pl.BlockSpec((1, d), lambda i: (0, 0)),
                  pl.BlockSpec((rows, d), lambda i: (i, 0))],
        out_specs=pl.BlockSpec((rows, d), lambda i: (i, 0)),
        out_shape=jax.ShapeDtypeStruct((tt, d), F32),
        compiler_params=_cparams("parallel"),
        name="cmlp_mix_out",
    )(u, v, mix, bias, w, bo, x)


GROUP_ROW_OFFSET = SUBLANES


def _first_argmax_rows(x, n):
    row = lax.broadcasted_iota(I32, x.shape, 0)
    mx = jnp.max(x, axis=0, keepdims=True)
    idx = jnp.min(jnp.where(x == mx, row, n), axis=0, keepdims=True)
    return idx, mx


def _router_body(x_ref, g_ref, wt_ref, bt_ref, route_ref, cnt_ref, *, n_groups, per_group):
    i = pl.program_id(0)
    n_exp = n_groups * per_group

    @pl.when(i == 0)
    def _():
        cnt_ref[...] = jnp.zeros(cnt_ref.shape, F32)

    h = _rms(x_ref[...], g_ref[...])
    logits = _dot_nt_split(wt_ref[...], h) + bt_ref[:, 0:1]
    tokens = logits.shape[1]
    gl = logits[0:n_groups, :]
    gmax = jnp.max(gl, axis=0, keepdims=True)
    ge = jnp.exp(gl - gmax)
    gp = ge / jnp.sum(ge, axis=0, keepdims=True)
    g_idx, g_w = _first_argmax_rows(gp, n_groups)
    e_sel = jnp.zeros((per_group, tokens), F32)
    for g in range(n_groups):
        lo = GROUP_ROW_OFFSET + g * per_group
        e_sel = e_sel + jnp.where(g_idx == g, logits[lo:lo + per_group, :], 0.0)
    i1, m1 = _first_argmax_rows(e_sel, per_group)
    row = lax.broadcasted_iota(I32, e_sel.shape, 0)
    i2, m2 = _first_argmax_rows(jnp.where(row == i1, -jnp.inf, e_sel), per_group)
    t = jnp.exp(m2 - m1)
    w1 = (1.0 / (1.0 + t)) * g_w
    w2 = (t / (1.0 + t)) * g_w
    e1 = g_idx * per_group + i1
    e2 = g_idx * per_group + i2

    erow = lax.broadcasted_iota(I32, (n_exp, tokens), 0)
    oh1 = erow == e1
    oh2 = erow == e2
    onehot = jnp.where(oh1 | oh2, 1.0, 0.0)
    strict_upper = (lax.broadcasted_iota(I32, (tokens, tokens), 0)
                    < lax.broadcasted_iota(I32, (tokens, tokens), 1)).astype(BF16)
    before = _dot(onehot.astype(BF16), strict_upper) + cnt_ref[:, 0:1]
    r1 = jnp.sum(jnp.where(oh1, before, 0.0), axis=0, keepdims=True)
    r2 = jnp.sum(jnp.where(oh2, before, 0.0), axis=0, keepdims=True)
    cnt_ref[...] = cnt_ref[...] + jnp.sum(onehot, axis=1, keepdims=True)
    zero = jnp.zeros_like(w1)
    route_ref[...] = jnp.concatenate(
        [e1.astype(F32), e2.astype(F32), r1, r2, w1, w2, zero, zero], axis=0)


def _wide_tile(n_rows, max_tiles):
    n = n_rows // ROW_TILE
    return ROW_TILE * max(k for k in range(1, max_tiles + 1) if n % k == 0)


def _router(x, g, wt, bt, *, n_groups, per_group):
    tt, d = x.shape
    n_exp = n_groups * per_group
    rt = _wide_tile(tt, 2)
    return pl.pallas_call(
        functools.partial(_router_body, n_groups=n_groups, per_group=per_group),
        grid=(tt // rt,),
        in_specs=[pl.BlockSpec((rt, d), lambda i: (i, 0)),
                  pl.BlockSpec((1, d), lambda i: (0, 0)),
                  pl.BlockSpec(wt.shape, lambda i: (0, 0)),
                  pl.BlockSpec(bt.shape, lambda i: (0, 0))],
        out_specs=[pl.BlockSpec((SUBLANES, rt), lambda i: (0, i)),
                   pl.BlockSpec((n_exp, LANES), lambda i: (0, 0))],
        out_shape=[jax.ShapeDtypeStruct((SUBLANES, tt), F32),
                   jax.ShapeDtypeStruct((n_exp, LANES), F32)],
        compiler_params=_cparams("arbitrary"),
        name="moe_router",
    )(x, g, wt, bt)


def _positions_body(route_ref, off_ref, pos_ref):
    r = route_ref[...]
    n_exp = off_ref.shape[0]
    tokens = r.shape[1]
    erow = lax.broadcasted_iota(I32, (n_exp, tokens), 0)
    off = off_ref[:, 0:1]
    out = []
    for k in range(2):
        e = r[k:k + 1, :].astype(I32)
        base = jnp.sum(jnp.where(erow == e, off, 0.0), axis=0, keepdims=True)
        out.append((base + r[2 + k:3 + k, :]).astype(I32))
    pos_ref[...] = jnp.concatenate(out, axis=0)


def _positions(route, off):
    tt = route.shape[1]
    pt = _wide_tile(tt, 16)
    return pl.pallas_call(
        _positions_body,
        grid=(tt // pt,),
        in_specs=[pl.BlockSpec((SUBLANES, pt), lambda i: (0, i)),
                  pl.BlockSpec(off.shape, lambda i: (0, 0))],
        out_specs=pl.BlockSpec((2, pt), lambda i: (0, i)),
        out_shape=jax.ShapeDtypeStruct((2, tt), I32),
        compiler_params=_cparams("parallel"),
        name="moe_positions",
    )(route, off)


def _for_each_row(n_rows, fn):
    def group(g, carry):
        start = pl.multiple_of(g * SUBLANES, SUBLANES)
        for u in range(SUBLANES):
            fn(start, u)
        return carry

    lax.fori_loop(0, n_rows // SUBLANES, group, 0, unroll=2)


N_DMA_PRIORITIES = 2


def _start_alternating(copies):
    for i, cp in enumerate(copies):
        cp.start(priority=i % N_DMA_PRIORITIES)


def _tile_row(ref, group_start, u):
    return ref.at[pl.ds(group_start, SUBLANES)].at[pl.ds(u, 1)]


def _dispatch_body(pos_ref, x_ref, xs_hbm, sem, *, n_tokens):
    rows = x_ref.shape[0]
    base = pl.program_id(0) * rows

    def copies(start, u):
        return [pltpu.make_async_copy(_tile_row(x_ref, start, u),
                                      xs_hbm.at[pl.ds(pos_ref[k * n_tokens + base + start + u], 1)], sem)
                for k in range(2)]

    _for_each_row(rows, lambda start, u: _start_alternating(copies(start, u)))
    _for_each_row(rows, lambda start, u: [cp.wait() for cp in copies(start, u)])


def _dispatch(pos_flat, x):
    tt, d = x.shape
    rows = _wide_tile(tt, 2)
    return pl.pallas_call(
        functools.partial(_dispatch_body, n_tokens=tt),
        grid_spec=pltpu.PrefetchScalarGridSpec(
            num_scalar_prefetch=1,
            grid=(tt // rows,),
            in_specs=[pl.BlockSpec((rows, d), lambda i, p: (i, 0))],
            out_specs=pl.BlockSpec(memory_space=pl.ANY),
            scratch_shapes=[pltpu.SemaphoreType.DMA]),
        out_shape=jax.ShapeDtypeStruct((2 * tt, d), F32),
        compiler_params=_cparams("arbitrary"),
        name="moe_dispatch",
    )(pos_flat, x)


def _experts_body(tile_ref, exp_ref, lo_ref, hi_ref, nwork_ref,
                  xs_ref, g_ref, wg_ref, wu_ref, wd_ref, o_ref, wg_b, wu_b, wd_b):
    w = pl.program_id(0)
    prev = jnp.maximum(w - 1, 0)
    first = jnp.logical_or(w == 0, tile_ref[w] != tile_ref[prev])
    new_expert = jnp.logical_or(w == 0, exp_ref[w] != exp_ref[prev])

    @pl.when(w < nwork_ref[0])
    def _():
        @pl.when(new_expert)
        def _():
            wg_b[...] = wg_ref[0].astype(BF16)
            wu_b[...] = wu_ref[0].astype(BF16)
            wd_b[...] = wd_ref[0].astype(BF16)

        xb = _rms(xs_ref[...], g_ref[...]).astype(BF16)
        a = _dot(xb, wg_b[...])
        b = _dot(xb, wu_b[...])
        hg = (_silu(a) * b).astype(BF16)
        y = _dot(hg, wd_b[...])
        lo, hi = lo_ref[w], hi_ref[w]
        whole = jnp.logical_and(lo == 0, hi == EXPERT_TILE)
        row = lax.broadcasted_iota(I32, y.shape, 0)
        mine = (row >= lo) & (row < hi)

        @pl.when(whole)
        def _():
            o_ref[...] = y

        @pl.when(jnp.logical_and(first, jnp.logical_not(whole)))
        def _():
            o_ref[...] = jnp.where(mine, y, 0.0)

        @pl.when(jnp.logical_not(first))
        def _():
            o_ref[...] = jnp.where(mine, y, o_ref[...])


def _experts(meta, xs, g, wg, wu, wd):
    rows, d = xs.shape
    n_work = meta[0].shape[0]
    f = wg.shape[2]
    return pl.pallas_call(
        _experts_body,
        grid_spec=pltpu.PrefetchScalarGridSpec(
            num_scalar_prefetch=5,
            grid=(n_work,),
            in_specs=[pl.BlockSpec((EXPERT_TILE, d), lambda w, tl, ex, lo, hi, nw: (tl[w], 0)),
                      pl.BlockSpec((1, d), lambda w, tl, ex, lo, hi, nw: (0, 0)),
                      pl.BlockSpec((1, d, f), lambda w, tl, ex, lo, hi, nw: (ex[w], 0, 0)),
                      pl.BlockSpec((1, d, f), lambda w, tl, ex, lo, hi, nw: (ex[w], 0, 0)),
                      pl.BlockSpec((1, f, d), lambda w, tl, ex, lo, hi, nw: (ex[w], 0, 0))],
            out_specs=pl.BlockSpec((EXPERT_TILE, d), lambda w, tl, ex, lo, hi, nw: (tl[w], 0)),
            scratch_shapes=[pltpu.VMEM((d, f), BF16), pltpu.VMEM((d, f), BF16), pltpu.VMEM((f, d), BF16)]),
        out_shape=jax.ShapeDtypeStruct((rows, d), F32),
        compiler_params=_cparams("arbitrary"),
        name="moe_experts",
    )(*meta, xs, g, wg, wu, wd)


COMBINE_SLOTS = 2
COMBINE_TILE = 512


def _combine_rows(pos_ref, x_ref, route_ref, ys_hbm, buf, sem, n_tokens):
    i = pl.program_id(0)
    rows = x_ref.shape[0]

    def copies(tile, start, u):
        slot = tile % COMBINE_SLOTS
        t = tile * rows + start + u
        return [pltpu.make_async_copy(ys_hbm.at[pl.ds(pos_ref[k * n_tokens + t], 1)],
                                      _tile_row(buf.at[slot, k], start, u), sem.at[slot])
                for k in range(2)]

    def request(tile):
        _for_each_row(rows, lambda start, u: _start_alternating(copies(tile, start, u)))

    @pl.when(i == 0)
    def _():
        request(0)

    @pl.when(i + 1 < pl.num_programs(0))
    def _():
        request(i + 1)

    r = route_ref[...]
    rt = jnp.concatenate([r, jnp.zeros((LANES - SUBLANES, r.shape[1]), F32)], axis=0).T
    _for_each_row(rows, lambda start, u: [cp.wait() for cp in copies(i, start, u)])
    slot = i % COMBINE_SLOTS
    return x_ref[...] + rt[:, 4:5] * buf[slot, 0] + rt[:, 5:6] * buf[slot, 1]


def _combine_body(pos_ref, x_ref, route_ref, ys_hbm, o_ref, buf, sem, *, n_tokens):
    o_ref[...] = _combine_rows(pos_ref, x_ref, route_ref, ys_hbm, buf, sem, n_tokens)


def _combine_norm_body(pos_ref, x_ref, route_ref, g_ref, ys_hbm, op_ref, os_ref, buf, sem,
                       *, n_tokens, n_prompt_tiles):
    y = _rms(_combine_rows(pos_ref, x_ref, route_ref, ys_hbm, buf, sem, n_tokens), g_ref[...])
    is_prompt = pl.program_id(0) < n_prompt_tiles

    @pl.when(is_prompt)
    def _():
        op_ref[...] = y

    @pl.when(jnp.logical_not(is_prompt))
    def _():
        os_ref[...] = y


def _combine(pos_flat, x, route, ys, final_g=None, n_prompt_tiles=None):
    tt, d = x.shape
    rows = COMBINE_TILE
    in_specs = [pl.BlockSpec((rows, d), lambda i, p: (i, 0)),
                pl.BlockSpec((SUBLANES, rows), lambda i, p: (0, i))]
    args = [pos_flat, x, route]
    if final_g is None:
        body = functools.partial(_combine_body, n_tokens=tt)
        out_specs = pl.BlockSpec((rows, d), lambda i, p: (i, 0))
        out_shape = jax.ShapeDtypeStruct((tt, d), F32)
    else:
        n_prompt_rows = n_prompt_tiles * ROW_TILE
        assert n_prompt_rows % rows == 0 and (tt - n_prompt_rows) % rows == 0
        npt = n_prompt_rows // rows
        body = functools.partial(_combine_norm_body, n_tokens=tt, n_prompt_tiles=npt)
        in_specs.append(pl.BlockSpec((1, d), lambda i, p: (0, 0)))
        args.append(final_g)
        out_specs = [pl.BlockSpec((rows, d), lambda i, p: (jnp.minimum(i, npt - 1), 0)),
                     pl.BlockSpec((rows, d), lambda i, p: (jnp.maximum(i - npt, 0), 0))]
        out_shape = [jax.ShapeDtypeStruct((n_prompt_rows, d), F32),
                     jax.ShapeDtypeStruct((tt - n_prompt_rows, d), F32)]
    in_specs.append(pl.BlockSpec(memory_space=pl.ANY))
    args.append(ys)
    return pl.pallas_call(
        body,
        grid_spec=pltpu.PrefetchScalarGridSpec(
            num_scalar_prefetch=1,
            grid=(tt // rows,),
            in_specs=in_specs,
            out_specs=out_specs,
            scratch_shapes=[pltpu.VMEM((COMBINE_SLOTS, 2, rows, d), F32),
                            pltpu.SemaphoreType.DMA((COMBINE_SLOTS,))]),
        out_shape=out_shape,
        compiler_params=_cparams("arbitrary"),
        name="moe_combine",
    )(*args)


def _work_items(counts, n_rows):
    n_exp = counts.shape[0]
    n_tiles = n_rows // EXPERT_TILE
    n_work = n_tiles + n_exp - 1
    ids = jnp.arange(n_exp, dtype=I32)
    lower = ids[None, :] <= ids[:, None]

    def cumsum(v):
        return jnp.sum(jnp.where(lower, v[None, :], 0), axis=1)

    end = cumsum(counts)
    off = end - counts
    first_tile = off // EXPERT_TILE
    last_tile = (end - 1) // EXPERT_TILE
    per_exp = jnp.where(counts > 0, last_tile - first_tile + 1, 0)
    wend = cumsum(per_exp)
    wstart = wend - per_exp
    total = jnp.sum(per_exp)
    w = jnp.arange(n_work, dtype=I32)
    e = jnp.minimum(jnp.sum((wend[None, :] <= w[:, None]).astype(I32), axis=1), n_exp - 1)
    sel = e[:, None] == ids[None, :]

    def pick(v):
        return jnp.sum(jnp.where(sel, v[None, :], 0), axis=1)

    tile = pick(first_tile) + (w - pick(wstart))
    lo = jnp.maximum(pick(off), tile * EXPERT_TILE) - tile * EXPERT_TILE
    hi = jnp.minimum(pick(end), (tile + 1) * EXPERT_TILE) - tile * EXPERT_TILE
    live = w < total
    tile = jnp.where(live, tile, n_tiles - 1)
    lo = jnp.where(live, lo, 0)
    hi = jnp.where(live, hi, 0)
    return off, (tile.astype(I32), e.astype(I32), lo.astype(I32), hi.astype(I32),
                 total.reshape(1).astype(I32))


def _moe(x, g, layer, w_rg, b_rg, w_re, b_re, w_gate, w_up, w_down, final_g=None, n_prompt_tiles=None):
    tt, d = x.shape
    n_groups, per_group = w_re.shape[0], w_re.shape[2]
    n_exp = n_groups * per_group
    f = w_gate.shape[-1]
    pad_rows = jnp.zeros((GROUP_ROW_OFFSET - n_groups, d), F32)
    wt = jnp.concatenate([w_rg.T, pad_rows, jnp.transpose(w_re, (0, 2, 1)).reshape(n_exp, d)], axis=0)
    bt = jnp.concatenate([b_rg, jnp.zeros((GROUP_ROW_OFFSET - n_groups,), F32), b_re.reshape(n_exp)])
    bt = jnp.broadcast_to(bt[:, None], (wt.shape[0], LANES))
    g2 = g.reshape(1, d)

    route, cnt = _router(x, g2, wt, bt, n_groups=n_groups, per_group=per_group)
    counts = cnt[:, 0].astype(I32)
    off, (tile, exp, lo, hi, total) = _work_items(counts, 2 * tt)
    off_b = jnp.broadcast_to(off.astype(F32)[:, None], (n_exp, LANES))
    pos = _positions(route, off_b).reshape(2 * tt)
    xs = _dispatch(pos, x)
    ys = _experts((tile, exp + layer * n_exp, lo, hi, total), xs, g2,
                  w_gate.reshape(-1, d, f), w_up.reshape(-1, d, f), w_down.reshape(-1, f, d))
    return _combine(pos, x, route, ys, final_g, n_prompt_tiles)


def _block_diag_tiles(w, tile):
    nb, c, _ = w.shape
    rows = w.reshape(nb * c // tile, tile, c)
    blk = jnp.arange(tile, dtype=I32) // c
    return jnp.where(blk[:, None] == blk[None, :], jnp.tile(rows, (1, 1, tile // c)), 0.0)


def _sample_to_rows(x, group_batch):
    db, s, w = x.shape
    return x.reshape(db // group_batch, group_batch, s, w).transpose(0, 2, 1, 3).reshape(db * s, w)


def _rows_to_sample(x, group_batch, s):
    rows, w = x.shape
    db = rows // s
    return x.reshape(db // group_batch, s, group_batch, w).transpose(0, 2, 1, 3).reshape(db, s, w)


def kernel(x_prompt, x_sample, state_mlstm_C, state_mlstm_n, state_mlstm_m, state_mlstm_conv,
           norm_mix, norm_ffn, norm_final,
           mlstm_w_up, mlstm_conv_w, mlstm_conv_b, mlstm_w_q, mlstm_w_k, mlstm_w_v,
           mlstm_w_ig, mlstm_b_ig, mlstm_w_fg, mlstm_b_fg, mlstm_skip, mlstm_hn_w, mlstm_w_down,
           cmlp_w_in, cmlp_b_in, cmlp_ln_w, cmlp_ln_b, cmlp_w_s, cmlp_b_s, cmlp_w_out, cmlp_b_out,
           moe_w_rg, moe_b_rg, moe_w_re, moe_b_re, moe_w_gate, moe_w_up, moe_w_down):
    batch, seq, d = x_prompt.shape
    dec_batch, dec_seq, _ = x_sample.shape
    heads = state_mlstm_C.shape[2]
    dh = state_mlstm_C.shape[3]
    inner = heads * dh
    tp, ts = batch * seq, dec_batch * dec_seq
    npt = tp // ROW_TILE
    gb = ROW_TILE // dec_seq
    n_sgroups = dec_batch // gb
    half_g = N_GATES // 2
    assert heads == half_g and tp % ROW_TILE == 0 and ts % ROW_TILE == 0
    assert ROW_TILE % dec_seq == 0 and dec_batch % gb == 0 and seq % MLSTM_CHUNK_PROMPT == 0
    assert seq % ROW_TILE == 0 and ROW_TILE % CMLP_CHUNK == 0 and dec_seq <= SUBLANES
    assert MLSTM_CHUNK_PROMPT == ROW_TILE

    xp = x_prompt.reshape(tp, d)
    xs = _sample_to_rows(x_sample, gb)

    xm, z = _norm_up(xp, xs, norm_mix[0].reshape(1, d), mlstm_w_up[0].astype(BF16))

    wq = _block_diag_tiles(mlstm_w_q[0], MXU_WIDTH)
    wk = _block_diag_tiles(mlstm_w_k[0], MXU_WIDTH)
    wqk = jnp.concatenate([wq, wk], axis=2).astype(BF16)
    wv = _block_diag_tiles(mlstm_w_v[0], MXU_WIDTH).astype(BF16)
    wg = jnp.concatenate([mlstm_w_ig[0], mlstm_w_fg[0]], axis=1)
    wg = jnp.pad(wg, ((0, 0), (0, LANES - N_GATES))).reshape(3, inner, LANES).astype(BF16)
    bg = jnp.pad(jnp.concatenate([mlstm_b_ig[0], mlstm_b_fg[0]]), (0, LANES - N_GATES)).reshape(1, LANES)
    cw, cb = mlstm_conv_w[0], mlstm_conv_b[0].reshape(1, inner)
    k_scale = float(dh) ** -0.5

    conv0_p = jnp.zeros((batch, SUBLANES, inner), F32)
    q_p, k_p, v_p, xc_p, _, bc_p, conv_p = _conv_qkv(
        xm, 0, conv0_p, cw, cb, wqk, wv, wg, bg,
        n_groups=batch, tiles_per_group=seq // ROW_TILE, row_stride=1, k_scale=k_scale)
    halo_s = (CONV_WIDTH - 1) * gb
    conv0_s = state_mlstm_conv[0].reshape(n_sgroups, gb, CONV_WIDTH - 1, inner)
    conv0_s = conv0_s.transpose(0, 2, 1, 3).reshape(n_sgroups, halo_s, inner)
    q_s, k_s, v_s, xc_s, gc_s, _, conv_s = _conv_qkv(
        xm, npt, conv0_s, cw, cb, wqk, wv, wg, bg,
        n_groups=n_sgroups, tiles_per_group=1, row_stride=gb, k_scale=k_scale)

    hn_w = mlstm_hn_w[0].reshape(heads, 1, dh)
    br_p = jnp.transpose(bc_p[:, :SUBLANES])
    hn_p, c_p, n_p, m_p = _cell_prompt(q_p, k_p, v_p, bc_p, br_p, hn_w, batch=batch, heads=heads)

    def pad_steps(a):
        a = a.reshape(n_sgroups, dec_seq, gb * inner)
        return jnp.pad(a, ((0, 0), (0, SUBLANES - dec_seq), (0, 0)))

    ln_s = MLSTM_CHUNK_SAMPLE
    gcs = gc_s.reshape(n_sgroups, dec_seq, gb, LANES).transpose(0, 2, 1, 3).reshape(dec_batch, dec_seq, LANES)
    gcs = jnp.pad(gcs, ((0, 0), (0, ln_s - dec_seq), (0, 0)))
    grs = jnp.transpose(gcs[:, :, :SUBLANES], (0, 2, 1))
    m0 = jnp.broadcast_to(state_mlstm_m[0][:, :, None, None], (dec_batch, heads, 1, LANES))
    hn_s, c_s, n_s, m_s = _cell_sample(
        pad_steps(q_s), pad_steps(k_s), pad_steps(v_s), gcs, grs, hn_w,
        state_mlstm_C[0], state_mlstm_n[0].reshape(dec_batch, heads, 1, dh), m0,
        heads=heads, valid=dec_seq, group_batch=gb)
    hn_s = hn_s[:, :dec_seq].reshape(ts, inner)

    x1 = _gate_down(hn_p, hn_s, xc_p, xc_s, z, mlstm_skip[0].reshape(1, inner),
                    mlstm_w_down[0].astype(BF16), xp, xs)
    x2 = _moe(x1, norm_ffn[0], 0, moe_w_rg[0], moe_b_rg[0], moe_w_re[0], moe_b_re[0],
              moe_w_gate, moe_w_up, moe_w_down)

    half = cmlp_w_in.shape[2] // 2
    groups = cmlp_w_s.shape[1]
    u, vn = _cmlp_in(x2, norm_mix[1].reshape(1, d), cmlp_w_in[0].astype(BF16),
                     cmlp_b_in[0].reshape(1, 2 * half), cmlp_ln_w[0].reshape(1, half),
                     cmlp_ln_b[0].reshape(1, half))
    causal = jnp.tril(jnp.ones((CMLP_CHUNK, CMLP_CHUNK), dtype=bool))
    ws = jnp.where(causal, cmlp_w_s[0], 0.0)
    reps = ROW_TILE // CMLP_CHUNK
    rid = jnp.arange(ROW_TILE, dtype=I32)
    same_chunk = (rid[:, None] // CMLP_CHUNK) == (rid[None, :] // CMLP_CHUNK)
    same_seq = (rid[:, None] % gb) == (rid[None, :] % gb)
    mix_p = jnp.where(same_chunk, jnp.tile(ws, (1, reps, reps)), 0.0)
    ws_s = ws[:, :dec_seq, :dec_seq]
    mix_s = jnp.where(same_seq, jnp.repeat(jnp.repeat(ws_s, gb, axis=1), gb, axis=2), 0.0)
    mix = jnp.stack([mix_p, mix_s]).astype(BF16)
    bias_t = jnp.transpose(cmlp_b_s[0])
    bias = jnp.stack([jnp.tile(bias_t, (reps, 1)), jnp.repeat(bias_t[:dec_seq], gb, axis=0)])
    x3 = _cmlp_mix_out(u, vn, mix, bias, cmlp_w_out[0].astype(BF16), cmlp_b_out[0].reshape(1, d), x2,
                       n_prompt_tiles=npt)
    y_p, y_s = _moe(x3, norm_ffn[1], 1, moe_w_rg[1], moe_b_rg[1], moe_w_re[1], moe_b_re[1],
                    moe_w_gate, moe_w_up, moe_w_down,
                    final_g=norm_final.reshape(1, d), n_prompt_tiles=npt)
    y_prompt = y_p.reshape(batch, seq, d)
    y_sample = _rows_to_sample(y_s, gb, dec_seq)

    conv_prompt = conv_p[:, SUBLANES - (CONV_WIDTH - 1):, :][None]
    conv_sample = conv_s.reshape(n_sgroups, CONV_WIDTH - 1, gb, inner).transpose(0, 2, 1, 3)
    conv_sample = conv_sample.reshape(dec_batch, CONV_WIDTH - 1, inner)[None]
    v_sample = _rows_to_sample(vn[tp:], gb, dec_seq)[None]
    return (y_prompt, y_sample,
            c_p[None], n_p[:, :, 0, :][None], m_p[:, :, 0, 0][None], conv_prompt,
            c_s[None], n_s[:, :, 0, :][None], m_s[:, :, 0, 0][None], conv_sample,
            v_sample)
```
